```python
import numpy as np
import jax
import jax.numpy as jnp
from jax import lax

D_MODEL = 1024
BATCH = 8
SEQ = 2048
DEPTH = 2

HEAD_DIM = 64
N_HEADS_MIX = 4
MIX_WIDTH = N_HEADS_MIX * HEAD_DIM
N_BRANCH = 4
ROPE_THETA = 10000.0
NORM_EPS = 1e-6
Q_BLOCK = 128
GATHER_QBLOCK = 64
NEG_INF = -1e30

NSA_CMP_LEN = 32
NSA_CMP_STRIDE = 16
NSA_CMP_HIDDEN = 256
NSA_SEL_BLOCK = 64
NSA_N_SEL = 16
NSA_WINDOW = 512
NSA_N_GATES = 3
NSA_FORCE_SCORE = 1e4

FOX_BIAS_INIT = 3.0

DSA_KV_RANK = 128
DSA_IDX_HEADS = 4
DSA_IDX_DIM = 64
DSA_TOPK_MAX = 256

SWA_WINDOW = 128
SWA_KV_HEADS = 2

MOE_GROUPS = 8
MOE_EXPERTS_PER_GROUP = 8
MOE_N_EXPERTS = MOE_GROUPS * MOE_EXPERTS_PER_GROUP
MOE_TOPK = 2
MOE_D_EXPERT = 256
MOE_ROW_BLOCK = 256

IN_SPLITS = (
    ('q_a', MIX_WIDTH), ('k_cmp', HEAD_DIM), ('v_cmp', HEAD_DIM),
    ('k_sel', HEAD_DIM), ('v_sel', HEAD_DIM), ('k_win', HEAD_DIM), ('v_win', HEAD_DIM),
    ('g_a', N_HEADS_MIX * NSA_N_GATES),
    ('q_b', MIX_WIDTH), ('k_b', MIX_WIDTH), ('v_b', MIX_WIDTH), ('f_b', N_HEADS_MIX),
    ('q_c', MIX_WIDTH), ('ckv_c', DSA_KV_RANK), ('qi_c', DSA_IDX_HEADS * DSA_IDX_DIM),
    ('ki_c', DSA_IDX_DIM), ('wi_c', DSA_IDX_HEADS),
    ('q_d', MIX_WIDTH), ('k_d', SWA_KV_HEADS * HEAD_DIM), ('v_d', SWA_KV_HEADS * HEAD_DIM),
)
D_IN = sum(w for _, w in IN_SPLITS)

kernel_name = 'hybrid_nsa_fox_dsa_swa_hmoe'


def rms_norm(x, gain):
    xf = x.astype(jnp.float32)
    y = xf * lax.rsqrt(jnp.mean(xf * xf, axis=-1, keepdims=True) + NORM_EPS)
    return (y * gain.astype(jnp.float32)).astype(x.dtype)


def rope(x, positions):
    half = x.shape[-1] // 2
    inv_freq = ROPE_THETA ** (-jnp.arange(half, dtype=jnp.float32) / half)
    ang = positions.astype(jnp.float32)[..., None] * inv_freq
    cos = jnp.cos(ang)[:, :, None, :]
    sin = jnp.sin(ang)[:, :, None, :]
    xf = x.astype(jnp.float32)
    x1, x2 = xf[..., :half], xf[..., half:]
    return jnp.concatenate([x1 * cos - x2 * sin, x2 * cos + x1 * sin], axis=-1).astype(x.dtype)


def masked_softmax(logits, mask):
    p = jax.nn.softmax(jnp.where(mask, logits, NEG_INF), axis=-1)
    return jnp.where(mask, p, 0.0)


def split_in(z):
    out = {}
    off = 0
    for name, width in IN_SPLITS:
        out[name] = z[..., off:off + width]
        off += width
    return out


def banded_attention(q, k, v, window, sinks=None):
    B, S, H, dh = q.shape
    KV = k.shape[2]
    G = H // KV
    nb, nw = S // Q_BLOCK, window // Q_BLOCK
    pad = ((0, 0), (window, 0), (0, 0), (0, 0))
    kp = jnp.pad(k, pad).reshape(B, nb + nw, Q_BLOCK, KV, dh)
    vp = jnp.pad(v, pad).reshape(B, nb + nw, Q_BLOCK, KV, dh)
    k_band = jnp.concatenate([kp[:, j:j + nb] for j in range(nw + 1)], axis=2)
    v_band = jnp.concatenate([vp[:, j:j + nb] for j in range(nw + 1)], axis=2)
    s_all = np.arange(-window, S).reshape(nb + nw, Q_BLOCK)
    s_band = np.concatenate([s_all[j:j + nb] for j in range(nw + 1)], axis=1)
    t_blk = np.arange(S).reshape(nb, Q_BLOCK)
    rel = t_blk[:, :, None] - s_band[:, None, :]
    mask = (rel >= 0) & (rel < window) & (s_band[:, None, :] >= 0)
    qb = q.reshape(B, nb, Q_BLOCK, KV, G, dh)
    s = jnp.einsum('bnqhgd,bnshd->bnhgqs', qb, k_band).astype(jnp.float32) * dh ** -0.5
    s = jnp.where(mask[None, :, None, None], s, NEG_INF)
    if sinks is not None:
        sink = jnp.broadcast_to(sinks.astype(jnp.float32).reshape(1, 1, KV, G, 1, 1), s.shape[:-1] + (1,))
        p = jax.nn.softmax(jnp.concatenate([s, sink], axis=-1), axis=-1)[..., :-1]
    else:
        p = jax.nn.softmax(s, axis=-1)
    o = jnp.einsum('bnhgqs,bnshd->bnqhgd', p.astype(v.dtype), v_band)
    return o.reshape(B, S, H, dh)


def nsa_compress(blocks, pe, w1, w2):
    B, n, L, dh = blocks.shape
    z = (blocks + pe).reshape(B, n, L * dh)
    return jax.nn.gelu(z @ w1) @ w2


def nsa_mixer(q, k_cmp, v_cmp, k_sel, v_sel, k_win, v_win, gate_logits, pe_k, w1_k, w2_k, pe_v, w1_v, w2_v):
    B, S, H, dh = q.shape
    scale = dh ** -0.5
    t = np.arange(S)
    n_cmp = (S - NSA_CMP_LEN) // NSA_CMP_STRIDE + 1
    win = np.arange(n_cmp)[:, None] * NSA_CMP_STRIDE + np.arange(NSA_CMP_LEN)[None, :]
    kc = nsa_compress(k_cmp[:, win], pe_k, w1_k, w2_k)
    vc = nsa_compress(v_cmp[:, win], pe_v, w1_v, w2_v)
    mask_c = win[:, -1][None, :] <= t[:, None]
    s_c = jnp.einsum('bthd,bcd->bhtc', q, kc).astype(jnp.float32) * scale
    p_c = masked_softmax(s_c, mask_c)
    o_cmp = jnp.einsum('bhtc,bcd->bthd', p_c.astype(vc.dtype), vc)
    n_blk = S // NSA_SEL_BLOCK
    c0 = np.arange(n_cmp) * NSA_CMP_STRIDE
    b0 = np.arange(n_blk) * NSA_SEL_BLOCK
    overlap = ((c0[:, None] < b0[None, :] + NSA_SEL_BLOCK) & (c0[:, None] + NSA_CMP_LEN > b0[None, :])).astype(np.float32)
    imp = jnp.einsum('bhtc,cj->btj', p_c, jnp.asarray(overlap))
    cur = (t // NSA_SEL_BLOCK)[:, None]
    j = np.arange(n_blk)[None, :]
    forced = (j == 0) | (j == cur) | (j == cur - 1)
    imp = jnp.where(forced, NSA_FORCE_SCORE, imp)
    imp = jnp.where(j > cur, -NSA_FORCE_SCORE, imp)
    n_sel = min(NSA_N_SEL, n_blk)
    _, sel = lax.top_k(imp, n_sel)
    kb = k_sel.reshape(B, n_blk, NSA_SEL_BLOCK, dh)
    vb = v_sel.reshape(B, n_blk, NSA_SEL_BLOCK, dh)
    ng = S // GATHER_QBLOCK
    q_g = q.reshape(B, ng, GATHER_QBLOCK, H, dh).swapaxes(0, 1)
    sel_g = sel.reshape(B, ng, GATHER_QBLOCK, n_sel).swapaxes(0, 1)
    t_g = jnp.arange(S).reshape(ng, GATHER_QBLOCK)
    offs = jnp.arange(NSA_SEL_BLOCK)
    take = jax.vmap(lambda tbl, idx: tbl[idx])
    n_keys = n_sel * NSA_SEL_BLOCK

    def sel_block(args):
        q_i, idx_i, t_i = args
        k_i = take(kb, idx_i).reshape(B, GATHER_QBLOCK, n_keys, dh)
        v_i = take(vb, idx_i).reshape(B, GATHER_QBLOCK, n_keys, dh)
        pos = (idx_i[..., None] * NSA_SEL_BLOCK + offs).reshape(B, GATHER_QBLOCK, n_keys)
        valid = pos <= t_i[None, :, None]
        s = jnp.einsum('bqhd,bqkd->bhqk', q_i, k_i).astype(jnp.float32) * scale
        p = masked_softmax(s, valid[:, None])
        return jnp.einsum('bhqk,bqkd->bqhd', p.astype(v_i.dtype), v_i)

    o_sel = lax.map(sel_block, (q_g, sel_g, t_g)).swapaxes(0, 1).reshape(B, S, H, dh)
    o_win = banded_attention(q, k_win[:, :, None, :], v_win[:, :, None, :], NSA_WINDOW)
    g = jax.nn.sigmoid(gate_logits.astype(jnp.float32)).astype(q.dtype)
    return g[..., 0:1] * o_cmp + g[..., 1:2] * o_sel + g[..., 2:3] * o_win


def fox_mixer(q, k, v, f_logit, f_bias):
    B, S, H, dh = q.shape
    scale = dh ** -0.5
    logf = jax.nn.log_sigmoid(f_logit.astype(jnp.float32) + f_bias.astype(jnp.float32))
    cum = jnp.cumsum(logf, axis=1).swapaxes(1, 2)
    nb = S // Q_BLOCK
    q_b = q.reshape(B, nb, Q_BLOCK, H, dh).swapaxes(0, 1)
    c_b = cum.reshape(B, H, nb, Q_BLOCK).transpose(2, 0, 1, 3)
    t_b = jnp.arange(S).reshape(nb, Q_BLOCK)
    key_pos = jnp.arange(S)

    def blk(args):
        q_i, c_i, t_i = args
        s = jnp.einsum('bqhd,bshd->bhqs', q_i, k).astype(jnp.float32) * scale
        s = s + (c_i[..., None] - cum[:, :, None, :])
        p = masked_softmax(s, key_pos[None, :] <= t_i[:, None])
        return jnp.einsum('bhqs,bshd->bqhd', p.astype(v.dtype), v)

    return lax.map(blk, (q_b, c_b, t_b)).swapaxes(0, 1).reshape(B, S, H, dh)


def dsa_mixer(q, k, v, q_idx, k_idx, w_idx):
    B, S, H, dh = q.shape
    n_ih, d_i = q_idx.shape[2], q_idx.shape[3]
    scale = dh ** -0.5
    top_k = min(DSA_TOPK_MAX, S // 4)
    ng = S // GATHER_QBLOCK
    w = w_idx.astype(jnp.float32) * n_ih ** -0.5
    q_g = q.reshape(B, ng, GATHER_QBLOCK, H, dh).swapaxes(0, 1)
    qi_g = q_idx.reshape(B, ng, GATHER_QBLOCK, n_ih, d_i).swapaxes(0, 1)
    w_g = w.reshape(B, ng, GATHER_QBLOCK, n_ih).swapaxes(0, 1)
    t_g = jnp.arange(S).reshape(ng, GATHER_QBLOCK)
    key_pos = jnp.arange(S)
    take = jax.vmap(lambda tbl, idx: tbl[idx])

    def blk(args):
        q_i, qi_i, w_i, t_i = args
        logit = jnp.einsum('bqhd,bsd->bqhs', qi_i, k_idx).astype(jnp.float32) * d_i ** -0.5
        score = jnp.einsum('bqh,bqhs->bqs', w_i, jax.nn.relu(logit))
        score = jnp.where(key_pos[None, None, :] <= t_i[None, :, None], score, NEG_INF)
        _, idx = lax.top_k(score, top_k)
        valid = idx <= t_i[None, :, None]
        k_i = take(k, idx)
        v_i = take(v, idx)
        s = jnp.einsum('bqhd,bqkhd->bhqk', q_i, k_i).astype(jnp.float32) * scale
        p = masked_softmax(s, valid[:, None])
        return jnp.einsum('bhqk,bqkhd->bqhd', p.astype(v_i.dtype), v_i)

    return lax.map(blk, (q_g, qi_g, w_g, t_g)).swapaxes(0, 1).reshape(B, S, H, dh)


def hier_moe(x, w_group, b_group, w_expert, b_expert, w_gate, w_up, w_down):
    B, S, D = x.shape
    T = B * S
    N = T * MOE_TOPK
    xf = x.reshape(T, D)
    g_logits = (xf @ w_group).astype(jnp.float32) + b_group.astype(jnp.float32)
    grp = jnp.argmax(g_logits, axis=-1)
    p_grp = jnp.take_along_axis(jax.nn.softmax(g_logits, axis=-1), grp[:, None], axis=1)[:, 0]
    e_logits = ((xf @ w_expert).astype(jnp.float32) + b_expert.astype(jnp.float32)).reshape(T, MOE_GROUPS, MOE_EXPERTS_PER_GROUP)
    e_in_grp = jnp.take_along_axis(e_logits, grp[:, None, None], axis=1)[:, 0]
    top_val, top_loc = lax.top_k(e_in_grp, MOE_TOPK)
    weights = (p_grp[:, None] * jax.nn.softmax(top_val, axis=-1)).reshape(N)
    expert_id = (grp[:, None] * MOE_EXPERTS_PER_GROUP + top_loc).reshape(N)
    token_id = jnp.repeat(jnp.arange(T, dtype=jnp.int32), MOE_TOPK)
    order = jnp.argsort(expert_id)
    e_sorted = expert_id[order]
    counts = jnp.bincount(expert_id, length=MOE_N_EXPERTS)
    padded = (counts + MOE_ROW_BLOCK - 1) // MOE_ROW_BLOCK * MOE_ROW_BLOCK
    start = jnp.cumsum(counts) - counts
    p_end = jnp.cumsum(padded)
    p_start = p_end - padded
    dest = p_start[e_sorted] + jnp.arange(N) - start[e_sorted]
    n_blocks = -(-(N + MOE_N_EXPERTS * (MOE_ROW_BLOCK - 1)) // MOE_ROW_BLOCK)
    rows = n_blocks * MOE_ROW_BLOCK
    row_tok = jnp.zeros((rows,), jnp.int32).at[dest].set(token_id[order])
    row_w = jnp.zeros((rows,), jnp.float32).at[dest].set(weights[order])
    blk_expert = jnp.minimum(jnp.searchsorted(p_end, jnp.arange(n_blocks) * MOE_ROW_BLOCK, side='right'), MOE_N_EXPERTS - 1)

    def expert_block(args):
        tok, e = args
        xt = xf[tok]
        return (jax.nn.silu(xt @ w_gate[e]) * (xt @ w_up[e])) @ w_down[e]

    y = lax.map(expert_block, (row_tok.reshape(n_blocks, MOE_ROW_BLOCK), blk_expert)).reshape(rows, D)
    out = jnp.zeros((T, D), x.dtype).at[row_tok].add(y * row_w[:, None].astype(y.dtype))
    return out.reshape(B, S, D)


def setup_inputs(seed: int = 0) -> dict:
    key = jax.random.key(seed)
    ks = iter(jax.random.split(key, 40))
    nrm = lambda shape, scale: jax.random.normal(next(ks), shape, jnp.float32) * scale
    L, D, dh, W = DEPTH, D_MODEL, HEAD_DIM, MIX_WIDTH
    E, DE = MOE_N_EXPERTS, MOE_D_EXPERT
    return {
        'x': nrm((BATCH, SEQ, D), 1.0),
        'positions': jnp.broadcast_to(jnp.arange(SEQ, dtype=jnp.int32), (BATCH, SEQ)),
        'attn_norm': 1.0 + nrm((L, D), 0.02),
        'w_in': nrm((L, D, D_IN), D ** -0.5),
        'nsa_pe_k': nrm((L, NSA_CMP_LEN, dh), 0.02),
        'nsa_w1_k': nrm((L, NSA_CMP_LEN * dh, NSA_CMP_HIDDEN), (NSA_CMP_LEN * dh) ** -0.5),
        'nsa_w2_k': nrm((L, NSA_CMP_HIDDEN, dh), NSA_CMP_HIDDEN ** -0.5),
        'nsa_pe_v': nrm((L, NSA_CMP_LEN, dh), 0.02),
        'nsa_w1_v': nrm((L, NSA_CMP_LEN * dh, NSA_CMP_HIDDEN), (NSA_CMP_LEN * dh) ** -0.5),
        'nsa_w2_v': nrm((L, NSA_CMP_HIDDEN, dh), NSA_CMP_HIDDEN ** -0.5),
        'fox_forget_bias': FOX_BIAS_INIT + nrm((L, N_HEADS_MIX), 0.1),
        'dsa_kv_norm': 1.0 + nrm((L, DSA_KV_RANK), 0.02),
        'dsa_w_ukv': nrm((L, DSA_KV_RANK, 2 * W), DSA_KV_RANK ** -0.5),
        'swa_sinks': nrm((L, N_HEADS_MIX), 0.5),
        'w_branch': nrm((L, N_BRANCH, W, D), W ** -0.5),
        'w_gate': nrm((L, N_BRANCH, D, D), D ** -0.5),
        'w_out': nrm((L, D, D), D ** -0.5),
        'ffn_norm': 1.0 + nrm((L, D), 0.02),
        'moe_w_group': nrm((L, D, MOE_GROUPS), D ** -0.5),
        'moe_b_group': nrm((L, MOE_GROUPS), 0.01),
        'moe_w_expert': nrm((L, D, E), D ** -0.5),
        'moe_b_expert': nrm((L, E), 0.01),
        'moe_w_gate': nrm((L, E, D, DE), D ** -0.5),
        'moe_w_up': nrm((L, E, D, DE), D ** -0.5),
        'moe_w_down': nrm((L, E, DE, D), DE ** -0.5),
        'final_norm': 1.0 + nrm((D,), 0.02),
    }


def reference(x, positions, attn_norm, w_in, nsa_pe_k, nsa_w1_k, nsa_w2_k, nsa_pe_v, nsa_w1_v, nsa_w2_v,
              fox_forget_bias, dsa_kv_norm, dsa_w_ukv, swa_sinks, w_branch, w_gate, w_out, ffn_norm,
              moe_w_group, moe_b_group, moe_w_expert, moe_b_expert, moe_w_gate, moe_w_up, moe_w_down,
              final_norm):
    B, S, D = x.shape
    H = N_HEADS_MIX
    heads = lambda a, n: a.reshape(B, S, n, HEAD_DIM)
    rope1 = lambda a: rope(a[:, :, None, :], positions)[:, :, 0, :]
    h = x
    for l in range(DEPTH):
        hn = rms_norm(h, attn_norm[l])
        z = split_in(hn @ w_in[l])
        o_a = nsa_mixer(
            rope(heads(z['q_a'], H), positions),
            rope1(z['k_cmp']), z['v_cmp'], rope1(z['k_sel']), z['v_sel'], rope1(z['k_win']), z['v_win'],
            z['g_a'].reshape(B, S, H, NSA_N_GATES),
            nsa_pe_k[l], nsa_w1_k[l], nsa_w2_k[l], nsa_pe_v[l], nsa_w1_v[l], nsa_w2_v[l])
        o_b = fox_mixer(heads(z['q_b'], H), heads(z['k_b'], H), heads(z['v_b'], H), z['f_b'], fox_forget_bias[l])
        kv_c = rms_norm(z['ckv_c'], dsa_kv_norm[l]) @ dsa_w_ukv[l]
        o_c = dsa_mixer(
            rope(heads(z['q_c'], H), positions),
            rope(heads(kv_c[..., :MIX_WIDTH], H), positions),
            heads(kv_c[..., MIX_WIDTH:], H),
            rope(z['qi_c'].reshape(B, S, DSA_IDX_HEADS, DSA_IDX_DIM), positions),
            rope1(z['ki_c']), z['wi_c'])
        o_d = banded_attention(
            rope(heads(z['q_d'], H), positions),
            rope(heads(z['k_d'], SWA_KV_HEADS), positions),
            heads(z['v_d'], SWA_KV_HEADS), SWA_WINDOW, swa_sinks[l])
        merged = jnp.zeros_like(h)
        for i, o in enumerate((o_a, o_b, o_c, o_d)):
            gate = jax.nn.sigmoid((hn @ w_gate[l, i]).astype(jnp.float32)).astype(h.dtype)
            merged = merged + gate * (o.reshape(B, S, MIX_WIDTH) @ w_branch[l, i])
        h = h + merged @ w_out[l]
        h = h + hier_moe(rms_norm(h, ffn_norm[l]), moe_w_group[l], moe_b_group[l], moe_w_expert[l],
                         moe_b_expert[l], moe_w_gate[l], moe_w_up[l], moe_w_down[l])
    return rms_norm(h, final_norm)
```

```python
import functools

import numpy as np
import jax
import jax.numpy as jnp
from jax import lax
from jax.experimental import pallas as pl
from jax.experimental.pallas import tpu as pltpu

F32 = jnp.float32
BF16 = jnp.bfloat16
I32 = jnp.int32

HEAD_DIM = 64
N_HEADS = 4
MIX_WIDTH = N_HEADS * HEAD_DIM
ROPE_THETA = 10000.0
NORM_EPS = 1e-6
NEG_INF = -1e30
INT_MIN = -2 ** 31

NSA_CMP_LEN = 32
NSA_CMP_STRIDE = 16
NSA_CMP_HIDDEN = 256
NSA_SEL_BLOCK = 64
NSA_N_SEL = 16
NSA_WINDOW = 512
NSA_FORCE_SCORE = 1e4
DSA_KV_RANK = 128
DSA_TOPK_MAX = 256
SWA_WINDOW = 128
SWA_KV_HEADS = 2
MOE_GROUPS = 8
MOE_EXPERTS_PER_GROUP = 8
MOE_N_EXPERTS = 64
MOE_D_EXPERT = 256

LANES = 128
VMEM_LIMIT = 48 * 1024 * 1024

N_ROPE_COLS = 1408
N_PLAIN_COLS = 1216
N_SMALL_COLS = 128
N_PROJ_COLS = N_ROPE_COLS + N_PLAIN_COLS + N_SMALL_COLS
SM_GATE0, SM_F0, SM_W0 = 0, 12, 16

TM_PROJ = 512
TQ = 128
TM_MERGE = 512
MOE_ROW_BLOCK = 256
TD_DISPATCH = 1024
TC_COMBINE = 256


def _cparams(sem):
    return pltpu.CompilerParams(dimension_semantics=sem, vmem_limit_bytes=VMEM_LIMIT)


def _dot(a, b):
    return jnp.dot(a, b, preferred_element_type=F32)


def _dot_t(a, b):
    return lax.dot_general(a, b, (((1,), (1,)), ((), ())), preferred_element_type=F32)


def _iota(shape, dim):
    return lax.broadcasted_iota(I32, shape, dim)


def _rope_slab(x, cos, sin_signed, first_half):
    rot = jnp.where(first_half, pltpu.roll(x, 96, 1), pltpu.roll(x, 32, 1))
    return x * cos + rot * sin_signed


def _proj_kernel(h_ref, pos_ref, invf_ref, gain_ref, w_ref, fbias_ref, kvn_ref, wukv_ref,
                 qa_ref, qc_ref, qi_ref, qd_ref, kd_ref, kcmp_ref, ksel_ref, kwin_ref, ki_ref,
                 qb_ref, kb_ref, vb_ref, vd_ref, vcmp_ref, vsel_ref, vwin_ref, kc_ref, vc_ref,
                 sm_ref, carry_ref):
    j = pl.program_id(1)
    tm = h_ref.shape[0]
    h = h_ref[...]
    hn = h * lax.rsqrt(jnp.mean(h * h, axis=-1, keepdims=True) + NORM_EPS) * gain_ref[...]
    hb = hn.astype(BF16)

    ang = pos_ref[...].astype(F32) * invf_ref[...]
    cos = jnp.cos(ang)
    sin = jnp.sin(ang)
    lane = _iota((tm, LANES), 1)
    first_half = (lane % HEAD_DIM) < (HEAD_DIM // 2)
    sin_signed = jnp.where(first_half, -sin, sin)
    rope = lambda x: _rope_slab(x, cos, sin_signed, first_half)

    zr = _dot(hb, w_ref[:, 0:N_ROPE_COLS])
    slab = lambda k: rope(zr[:, k * LANES:(k + 1) * LANES])
    scale = HEAD_DIM ** -0.5
    for q_ref, k0 in ((qa_ref, 0), (qc_ref, 2), (qi_ref, 4), (qd_ref, 6)):
        q_ref[...] = jnp.concatenate([slab(k0), slab(k0 + 1)], axis=1).astype(BF16) * scale
    s8 = slab(8).astype(BF16)
    kd_ref[0] = s8[:, :HEAD_DIM]
    kd_ref[1] = s8[:, HEAD_DIM:]
    s9 = slab(9).astype(BF16)
    kcmp_ref[...] = s9[:, :HEAD_DIM]
    ksel_ref[...] = s9[:, HEAD_DIM:]
    s10 = slab(10).astype(BF16)
    kwin_ref[...] = s10[:, :HEAD_DIM]
    ki_ref[...] = s10[:, HEAD_DIM:]

    zp = _dot(hb, w_ref[:, N_ROPE_COLS:N_ROPE_COLS + N_PLAIN_COLS])
    qb_ref[...] = zp[:, 0:256].astype(BF16) * scale
    for hh in range(N_HEADS):
        kb_ref[hh] = zp[:, 256 + hh * 64:256 + (hh + 1) * 64].astype(BF16)
        vb_ref[hh] = zp[:, 512 + hh * 64:512 + (hh + 1) * 64].astype(BF16)
    vd_ref[0] = zp[:, 768:832].astype(BF16)
    vd_ref[1] = zp[:, 832:896].astype(BF16)
    ckv = zp[:, 896:1024]
    vcmp_ref[...] = zp[:, 1024:1088].astype(BF16)
    vsel_ref[...] = zp[:, 1088:1152].astype(BF16)
    vwin_ref[...] = zp[:, 1152:1216].astype(BF16)

    ckvn = ckv * lax.rsqrt(jnp.mean(ckv * ckv, axis=-1, keepdims=True) + NORM_EPS) * kvn_ref[...]
    kvc = _dot(ckvn.astype(BF16), wukv_ref[...])
    for k in range(2):
        kk = rope(kvc[:, k * LANES:(k + 1) * LANES]).astype(BF16)
        kc_ref[2 * k] = kk[:, :HEAD_DIM]
        kc_ref[2 * k + 1] = kk[:, HEAD_DIM:]
    for hh in range(N_HEADS):
        vc_ref[hh] = kvc[:, 256 + hh * 64:256 + (hh + 1) * 64].astype(BF16)

    zs = _dot(hb, w_ref[:, N_ROPE_COLS + N_PLAIN_COLS:N_PROJ_COLS])
    sig = 1.0 / (1.0 + jnp.exp(-zs))
    xf = zs + fbias_ref[...]
    logf = jnp.minimum(xf, 0.0) - jnp.log(1.0 + jnp.exp(-jnp.abs(xf)))
    is_f = (lane >= SM_F0) & (lane < SM_W0)
    logf = jnp.where(is_f, logf, 0.0)

    @pl.when(j == 0)
    def _():
        carry_ref[...] = jnp.zeros_like(carry_ref)

    tri = (_iota((tm, tm), 0) >= _iota((tm, tm), 1)).astype(F32)
    cum = jnp.dot(tri, logf, preferred_element_type=F32, precision=lax.Precision.HIGHEST)
    cum = cum + carry_ref[0:1, :]
    carry_ref[0:1, :] = cum[tm - 1:tm, :]
    sm_ref[...] = jnp.where(lane < SM_F0, sig,
                            jnp.where(is_f, cum, jnp.where(lane < SM_W0 + 4, 0.5 * zs, 0.0)))


def _proj_call(h, pos, invf, gain, w, fbias, kvn, wukv, B, S):
    T = B * S
    tm = TM_PROJ
    nj = S // tm
    row = lambda b, j: (b * nj + j, 0)
    hrow = lambda b, j: (0, b * nj + j, 0)
    const2 = lambda b, j: (0, 0)
    tok = lambda w_, dt: jax.ShapeDtypeStruct((T, w_), dt)
    hm = lambda n: jax.ShapeDtypeStruct((n, T, HEAD_DIM), BF16)
    out_shape = [tok(256, BF16)] * 4 + [hm(2)] + [tok(64, BF16)] * 4 + [tok(256, BF16), hm(4), hm(4), hm(2)] \
        + [tok(64, BF16)] * 3 + [hm(4), hm(4), tok(LANES, F32)]
    spec_tok = lambda w_: pl.BlockSpec((tm, w_), row)
    spec_hm = lambda n: pl.BlockSpec((n, tm, HEAD_DIM), hrow)
    out_specs = [spec_tok(256)] * 4 + [spec_hm(2)] + [spec_tok(64)] * 4 + [spec_tok(256), spec_hm(4), spec_hm(4), spec_hm(2)] \
        + [spec_tok(64)] * 3 + [spec_hm(4), spec_hm(4), spec_tok(LANES)]
    in_specs = [
        pl.BlockSpec((tm, h.shape[1]), row),
        pl.BlockSpec((tm, 1), row),
        pl.BlockSpec((1, LANES), const2),
        pl.BlockSpec((1, h.shape[1]), const2),
        pl.BlockSpec(w.shape, const2),
        pl.BlockSpec((1, LANES), const2),
        pl.BlockSpec((1, DSA_KV_RANK), const2),
        pl.BlockSpec(wukv.shape, const2),
    ]
    return pl.pallas_call(
        _proj_kernel, grid=(B, nj), in_specs=in_specs, out_specs=out_specs, out_shape=out_shape,
        scratch_shapes=[pltpu.VMEM((8, LANES), F32)],
        compiler_params=_cparams(("arbitrary", "arbitrary")),
    )(h, pos, invf, gain, w, fbias, kvn, wukv)


def _gelu_tanh(x):
    return 0.5 * x * (1.0 + jnp.tanh(np.sqrt(2.0 / np.pi).astype(np.float32) * (x + 0.044715 * (x * x * x))))


def _cmp_kernel(ck_ref, cv_ref, pek_ref, w1k_ref, w2k_ref, pev_ref, w1v_ref, w2v_ref, kc_ref, vc_ref):
    half = NSA_CMP_STRIDE * HEAD_DIM
    for c_ref, pe_ref, w1_ref, w2_ref, o_ref in ((ck_ref, pek_ref, w1k_ref, w2k_ref, kc_ref),
                                                 (cv_ref, pev_ref, w1v_ref, w2v_ref, vc_ref)):
        c = c_ref[...]
        a = _dot(c, w1_ref[0:half, :])
        b = _dot(c, w1_ref[half:2 * half, :])
        n = b.shape[0]
        pe_h = _dot(pe_ref[...], w1_ref[...])
        hid = a + pltpu.roll(b, n - 1, 0) + pe_h[0:1, :]
        o_ref[...] = _dot(_gelu_tanh(hid).astype(BF16), w2_ref[...]).astype(BF16)


def _cmp_call(ck, cv, pek, w1k, w2k, pev, w1v, w2v, B, S):
    nch = S // NSA_CMP_STRIDE
    blk = lambda b: (b, 0)
    const = lambda b: (0, 0)
    cw = NSA_CMP_STRIDE * HEAD_DIM
    in_specs = [pl.BlockSpec((nch, cw), blk), pl.BlockSpec((nch, cw), blk)]
    for _ in range(2):
        in_specs += [pl.BlockSpec((8, 2 * cw), const), pl.BlockSpec((2 * cw, NSA_CMP_HIDDEN), const),
                     pl.BlockSpec((NSA_CMP_HIDDEN, HEAD_DIM), const)]
    out = jax.ShapeDtypeStruct((B * nch, HEAD_DIM), BF16)
    return pl.pallas_call(
        _cmp_kernel, grid=(B,), in_specs=in_specs,
        out_specs=[pl.BlockSpec((nch, HEAD_DIM), blk)] * 2, out_shape=[out, out],
        compiler_params=_cparams(("arbitrary",)),
    )(ck, cv, pek, w1k, w2k, pev, w1v, w2v)


def _masked_attn(q, k, v, allowed, extra_logit=None):
    s = jnp.where(allowed, _dot_t(q, k), NEG_INF)
    m = jnp.max(s, axis=-1, keepdims=True)
    if extra_logit is not None:
        m = jnp.maximum(m, extra_logit)
    e = jnp.where(allowed, jnp.exp(s - m), 0.0)
    den = jnp.sum(e, axis=-1, keepdims=True)
    if extra_logit is not None:
        den = den + jnp.exp(extra_logit - m)
    o = _dot(e.astype(BF16), v)
    return o * jnp.where(den > 0.0, 1.0 / den, 0.0)


def _nsa_kernel(q_ref, sm_ref, kc_ref, vc_ref, ov_ref, ex_ref, ksel_ref, vsel_ref, kwin_ref, vwin_ref, o_ref):
    i = pl.program_id(1)
    tq = q_ref.shape[0]
    S = ksel_ref.shape[0]
    ncmp = kc_ref.shape[0]
    q = q_ref[...]
    sm = sm_ref[...]
    t_col = i * tq + _iota((tq, 1), 0)

    c_end = _iota((1, ncmp), 1) * NSA_CMP_STRIDE + (NSA_CMP_LEN - 1)
    allowed_c = c_end <= t_col
    kc = kc_ref[...]
    vc = vc_ref[...]
    o_cmp = []
    imp = jnp.zeros((tq, LANES), F32)
    for hh in range(N_HEADS):
        qh = q[:, hh * 64:(hh + 1) * 64]
        s = jnp.where(allowed_c, _dot_t(qh, kc), NEG_INF)
        m = jnp.max(s, axis=-1, keepdims=True)
        e = jnp.where(allowed_c, jnp.exp(s - m), 0.0)
        den = jnp.sum(e, axis=-1, keepdims=True)
        p = e * jnp.where(den > 0.0, 1.0 / den, 0.0)
        pb = p.astype(BF16)
        o_cmp.append(_dot(pb, vc))
        imp = imp + _dot(pb, ov_ref[...])

    n_blk = S // NSA_SEL_BLOCK
    jl = _iota((tq, LANES), 1)
    cur = t_col // NSA_SEL_BLOCK
    forced = (jl == 0) | (jl == cur) | (jl == cur - 1)
    imp = jnp.where(forced, NSA_FORCE_SCORE, imp)
    imp = jnp.where(jl > cur, -NSA_FORCE_SCORE, imp)
    rank = jnp.zeros((tq, LANES), F32)
    for jp in range(n_blk):
        col = imp[:, jp:jp + 1]
        ahead = (col > imp) | ((col == imp) & (jp < jl))
        rank = rank + jnp.where(ahead, 1.0, 0.0)
    sel = jnp.where((rank < float(min(NSA_N_SEL, n_blk))) & (jl < n_blk), 1.0, 0.0).astype(BF16)
    sel_keys = _dot(sel, ex_ref[...])

    s_row = _iota((1, S), 1)
    allowed_sel = (sel_keys > 0.5) & (s_row <= t_col)
    ksel = ksel_ref[...]
    vsel = vsel_ref[...]

    nband = NSA_WINDOW + tq
    start = pl.multiple_of(jnp.maximum(i * tq - NSA_WINDOW, 0), tq)
    kwin = kwin_ref[pl.ds(start, nband), :]
    vwin = vwin_ref[pl.ds(start, nband), :]
    s_band = start + _iota((1, nband), 1)
    allowed_win = (s_band <= t_col) & (s_band > t_col - NSA_WINDOW)

    outs = []
    for hh in range(N_HEADS):
        qh = q[:, hh * 64:(hh + 1) * 64]
        o_sel = _masked_attn(qh, ksel, vsel, allowed_sel)
        o_win = _masked_attn(qh, kwin, vwin, allowed_win)
        g = lambda k: sm[:, SM_GATE0 + 3 * hh + k:SM_GATE0 + 3 * hh + k + 1]
        outs.append(g(0) * o_cmp[hh] + g(1) * o_sel + g(2) * o_win)
    o_ref[...] = jnp.concatenate(outs, axis=1).astype(BF16)


def _nsa_call(qa, sm, kcc, vcc, overlap, expand, ksel, vsel, kwin, vwin, B, S):
    T = B * S
    nq = S // TQ
    ncmp = S // NSA_CMP_STRIDE
    tile = lambda b, i: (b * nq + i, 0)
    seq = lambda b, i: (b, 0)
    const = lambda b, i: (0, 0)
    in_specs = [pl.BlockSpec((TQ, 256), tile), pl.BlockSpec((TQ, LANES), tile),
                pl.BlockSpec((ncmp, 64), seq), pl.BlockSpec((ncmp, 64), seq),
                pl.BlockSpec(overlap.shape, const), pl.BlockSpec(expand.shape, const)] \
        + [pl.BlockSpec((S, 64), seq)] * 4
    return pl.pallas_call(
        _nsa_kernel, grid=(B, nq), in_specs=in_specs, out_specs=pl.BlockSpec((TQ, 256), tile),
        out_shape=jax.ShapeDtypeStruct((T, 256), BF16),
        compiler_params=_cparams(("arbitrary", "arbitrary")),
    )(qa, sm, kcc, vcc, overlap, expand, ksel, vsel, kwin, vwin)


def _fox_kernel(q_ref, sm_ref, crow_ref, k_ref, v_ref, o_ref):
    i = pl.program_id(1)
    tq = q_ref.shape[0]
    S = k_ref.shape[1]
    q = q_ref[...]
    sm = sm_ref[...]
    t_col = i * tq + _iota((tq, 1), 0)
    causal = _iota((1, S), 1) <= t_col
    outs = []
    for hh in range(N_HEADS):
        qh = q[:, hh * 64:(hh + 1) * 64]
        bias = sm[:, SM_F0 + hh:SM_F0 + hh + 1] - crow_ref[0, hh:hh + 1, :]
        s = jnp.where(causal, _dot_t(qh, k_ref[hh]) + bias, NEG_INF)
        m = jnp.max(s, axis=-1, keepdims=True)
        e = jnp.exp(s - m)
        den = jnp.sum(e, axis=-1, keepdims=True)
        outs.append(_dot(e.astype(BF16), v_ref[hh]) / den)
    o_ref[...] = jnp.concatenate(outs, axis=1).astype(BF16)


def _fox_call(qb, sm, crow, kb, vb, B, S):
    T = B * S
    nq = S // TQ
    tile = lambda b, i: (b * nq + i, 0)
    in_specs = [pl.BlockSpec((TQ, 256), tile), pl.BlockSpec((TQ, LANES), tile),
                pl.BlockSpec((1, N_HEADS, S), lambda b, i: (b, 0, 0)),
                pl.BlockSpec((N_HEADS, S, 64), lambda b, i: (0, b, 0)),
                pl.BlockSpec((N_HEADS, S, 64), lambda b, i: (0, b, 0))]
    return pl.pallas_call(
        _fox_kernel, grid=(B, nq), in_specs=in_specs, out_specs=pl.BlockSpec((TQ, 256), tile),
        out_shape=jax.ShapeDtypeStruct((T, 256), BF16),
        compiler_params=_cparams(("arbitrary", "arbitrary")),
    )(qb, sm, crow, kb, vb)


def _dsa_kernel(q_ref, qi_ref, sm_ref, ki_ref, cnt_ref, k_ref, v_ref, o_ref, key_ref):
    i = pl.program_id(1)
    tq = q_ref.shape[0]
    S = ki_ref.shape[0]
    sm = sm_ref[...]
    qi = qi_ref[...]
    ki = ki_ref[...]
    t_col = i * tq + _iota((tq, 1), 0)
    causal = _iota((1, S), 1) <= t_col

    score = jnp.zeros((tq, S), F32)
    for hh in range(N_HEADS):
        lg = _dot_t(qi[:, hh * 64:(hh + 1) * 64], ki)
        score = score + sm[:, SM_W0 + hh:SM_W0 + hh + 1] * jnp.maximum(lg, 0.0)
    score = jnp.where(score == 0.0, 0.0, score)
    bits = lax.bitcast_convert_type(score, I32)
    key = bits ^ (lax.shift_right_arithmetic(bits, 31) & 0x7FFFFFFF)
    key_ref[...] = jnp.where(causal, key, INT_MIN)

    top_k = min(DSA_TOPK_MAX, S // 4)
    k_eff = jnp.minimum(t_col + 1, top_k).astype(F32)

    def search(it, ans):
        cand = ans | lax.shift_left(jnp.int32(1), 31 - it)
        thr = cand ^ INT_MIN
        cnt = jnp.sum(jnp.where(key_ref[...] >= thr, 1.0, 0.0), axis=-1, keepdims=True)
        return jnp.where(cnt >= k_eff, cand, ans)

    ans = lax.fori_loop(0, 32, search, jnp.zeros((tq, 1), I32))
    thr = ans ^ INT_MIN
    key = key_ref[...]
    gt = key > thr
    eq = key == thr
    need = k_eff - jnp.sum(jnp.where(gt, 1.0, 0.0), axis=-1, keepdims=True)
    eqb = jnp.where(eq, 1.0, 0.0).astype(BF16)
    run = jnp.zeros((tq, LANES), F32)
    sel_chunks = []
    for c in range(S // LANES):
        r = _dot(eqb[:, c * LANES:(c + 1) * LANES], cnt_ref[...])
        prefix = r[:, :LANES] + run
        run = run + r[:, LANES:]
        sl = slice(c * LANES, (c + 1) * LANES)
        sel_chunks.append(gt[:, sl] | (eq[:, sl] & (prefix < need)))
    allowed = jnp.concatenate(sel_chunks, axis=1)

    q = q_ref[...]
    outs = [_masked_attn(q[:, hh * 64:(hh + 1) * 64], k_ref[hh], v_ref[hh], allowed) for hh in range(N_HEADS)]
    o_ref[...] = jnp.concatenate(outs, axis=1).astype(BF16)


def _dsa_call(qc, qi, sm, ki, cntmat, kc, vc, B, S):
    T = B * S
    nq = S // TQ
    tile = lambda b, i: (b * nq + i, 0)
    in_specs = [pl.BlockSpec((TQ, 256), tile), pl.BlockSpec((TQ, 256), tile), pl.BlockSpec((TQ, LANES), tile),
                pl.BlockSpec((S, 64), lambda b, i: (b, 0)),
                pl.BlockSpec(cntmat.shape, lambda b, i: (0, 0)),
                pl.BlockSpec((N_HEADS, S, 64), lambda b, i: (0, b, 0)),
                pl.BlockSpec((N_HEADS, S, 64), lambda b, i: (0, b, 0))]
    return pl.pallas_call(
        _dsa_kernel, grid=(B, nq), in_specs=in_specs, out_specs=pl.BlockSpec((TQ, 256), tile),
        out_shape=jax.ShapeDtypeStruct((T, 256), BF16),
        scratch_shapes=[pltpu.VMEM((TQ, S), I32)],
        compiler_params=_cparams(("arbitrary", "arbitrary")),
    )(qc, qi, sm, ki, cntmat, kc, vc)


def _swa_kernel(q_ref, sink_ref, k_ref, v_ref, o_ref):
    i = pl.program_id(1)
    tq = q_ref.shape[0]
    q = q_ref[...]
    t_col = i * tq + _iota((tq, 1), 0)
    nband = SWA_WINDOW + tq
    start = pl.multiple_of(jnp.maximum(i * tq - SWA_WINDOW, 0), tq)
    s_band = start + _iota((1, nband), 1)
    allowed = (s_band <= t_col) & (s_band > t_col - SWA_WINDOW)
    group = N_HEADS // SWA_KV_HEADS
    outs = []
    for hh in range(N_HEADS):
        k = k_ref[hh // group, pl.ds(start, nband), :]
        v = v_ref[hh // group, pl.ds(start, nband), :]
        sink = jnp.zeros((tq, 1), F32) + sink_ref[0:1, hh:hh + 1]
        outs.append(_masked_attn(q[:, hh * 64:(hh + 1) * 64], k, v, allowed, extra_logit=sink))
    o_ref[...] = jnp.concatenate(outs, axis=1).astype(BF16)


def _swa_call(qd, sinks, kd, vd, B, S):
    T = B * S
    nq = S // TQ
    tile = lambda b, i: (b * nq + i, 0)
    in_specs = [pl.BlockSpec((TQ, 256), tile), pl.BlockSpec((1, LANES), lambda b, i: (0, 0)),
                pl.BlockSpec((SWA_KV_HEADS, S, 64), lambda b, i: (0, b, 0)),
                pl.BlockSpec((SWA_KV_HEADS, S, 64), lambda b, i: (0, b, 0))]
    return pl.pallas_call(
        _swa_kernel, grid=(B, nq), in_specs=in_specs, out_specs=pl.BlockSpec((TQ, 256), tile),
        out_shape=jax.ShapeDtypeStruct((T, 256), BF16),
        compiler_params=_cparams(("arbitrary", "arbitrary")),
    )(qd, sinks, kd, vd)


RT_E1, RT_E2, RT_W1, RT_W2, RT_R1, RT_R2 = 0, 1, 2, 3, 4, 5
ROUTER_E0 = MOE_GROUPS


def _merge_kernel(h_ref, oa_ref, ob_ref, oc_ref, od_ref, gain_ref, wg_ref, wb_ref, wo_ref,
                  fgain_ref, wr_ref, br_ref, h2_ref, xn_ref, rt_ref, cnt_ref, base_ref):
    step = pl.program_id(0)
    tm = h_ref.shape[0]
    h = h_ref[...]
    hn = h * lax.rsqrt(jnp.mean(h * h, axis=-1, keepdims=True) + NORM_EPS) * gain_ref[...]
    hb = hn.astype(BF16)
    merged = jnp.zeros_like(h)
    for i, o_ref in enumerate((oa_ref, ob_ref, oc_ref, od_ref)):
        gate = 1.0 / (1.0 + jnp.exp(-_dot(hb, wg_ref[i])))
        merged = merged + gate * _dot(o_ref[...], wb_ref[i])
    h2 = h + _dot(merged.astype(BF16), wo_ref[...])
    h2_ref[...] = h2

    xn = h2 * lax.rsqrt(jnp.mean(h2 * h2, axis=-1, keepdims=True) + NORM_EPS) * fgain_ref[...]
    xn_ref[...] = xn
    logits = jnp.dot(xn, wr_ref[...], preferred_element_type=F32, precision=lax.Precision.HIGHEST) + br_ref[...]

    lane = _iota((tm, LANES), 1)
    lanef = lane.astype(F32)
    big = float(LANES)
    is_g = lane < MOE_GROUPS
    gl = jnp.where(is_g, logits, NEG_INF)
    gmax = jnp.max(gl, axis=-1, keepdims=True)
    grp = jnp.min(jnp.where(is_g & (gl == gmax), lanef, big), axis=-1, keepdims=True)
    p_grp = 1.0 / jnp.sum(jnp.where(is_g, jnp.exp(gl - gmax), 0.0), axis=-1, keepdims=True)
    lo = ROUTER_E0 + grp * MOE_EXPERTS_PER_GROUP
    in_grp = (lanef >= lo) & (lanef < lo + MOE_EXPERTS_PER_GROUP)
    el = jnp.where(in_grp, logits, NEG_INF)
    v1 = jnp.max(el, axis=-1, keepdims=True)
    l1 = jnp.min(jnp.where(in_grp & (el == v1), lanef, big), axis=-1, keepdims=True)
    el2 = jnp.where(lanef == l1, NEG_INF, el)
    v2 = jnp.max(el2, axis=-1, keepdims=True)
    l2 = jnp.min(jnp.where(in_grp & (lanef != l1) & (el2 == v2), lanef, big), axis=-1, keepdims=True)
    e21 = jnp.exp(v2 - v1)
    w1 = p_grp / (1.0 + e21)
    w2 = p_grp * e21 / (1.0 + e21)

    oh1 = jnp.where(lanef == l1, 1.0, 0.0)
    oh2 = jnp.where(lanef == l2, 1.0, 0.0)

    @pl.when(step == 0)
    def _():
        base_ref[...] = jnp.zeros_like(base_ref)

    both = oh1 + oh2
    strict = (_iota((tm, tm), 0) > _iota((tm, tm), 1)).astype(BF16)
    before = _dot(strict, both.astype(BF16)) + base_ref[0:1, :]
    r1 = jnp.sum(oh1 * before, axis=-1, keepdims=True)
    r2 = jnp.sum(oh2 * before, axis=-1, keepdims=True)
    total = base_ref[0:1, :] + jnp.sum(both, axis=0, keepdims=True)
    base_ref[0:1, :] = total
    cnt_ref[...] = jnp.broadcast_to(total, cnt_ref.shape)

    rt = jnp.zeros((tm, LANES), F32)
    for ln, val in ((RT_E1, l1 - ROUTER_E0), (RT_E2, l2 - ROUTER_E0), (RT_W1, w1), (RT_W2, w2), (RT_R1, r1), (RT_R2, r2)):
        rt = jnp.where(lane == ln, val, rt)
    rt_ref[...] = rt


def _merge_call(h, oa, ob, oc, od, gain, wg, wb, wo, fgain, wr, br):
    T, D = h.shape
    tm = TM_MERGE
    tile = lambda i: (i, 0)
    c2 = lambda i: (0, 0)
    c3 = lambda i: (0, 0, 0)
    once = pl.Buffered(1)
    in_specs = [pl.BlockSpec((tm, D), tile)] + [pl.BlockSpec((tm, 256), tile)] * 4 + [
        pl.BlockSpec((1, D), c2),
        pl.BlockSpec(wg.shape, c3, pipeline_mode=once),
        pl.BlockSpec(wb.shape, c3, pipeline_mode=once),
        pl.BlockSpec(wo.shape, c2, pipeline_mode=once),
        pl.BlockSpec((1, D), c2),
        pl.BlockSpec(wr.shape, c2),
        pl.BlockSpec((1, LANES), c2),
    ]
    out_shape = [jax.ShapeDtypeStruct((T, D), F32), jax.ShapeDtypeStruct((T, D), F32),
                 jax.ShapeDtypeStruct((T, LANES), F32), jax.ShapeDtypeStruct((8, LANES), F32)]
    out_specs = [pl.BlockSpec((tm, D), tile), pl.BlockSpec((tm, D), tile), pl.BlockSpec((tm, LANES), tile),
                 pl.BlockSpec((8, LANES), c2)]
    return pl.pallas_call(
        _merge_kernel, grid=(T // tm,), in_specs=in_specs, out_specs=out_specs, out_shape=out_shape,
        scratch_shapes=[pltpu.VMEM((8, LANES), F32)],
        compiler_params=_cparams(("arbitrary",)),
    )(h, oa, ob, oc, od, gain, wg, wb, wo, fgain, wr, br)


def _row_copy(src, src_row, dst, dst_row, sem):
    return pltpu.make_async_copy(src.at[pl.ds(src_row, 1)], dst.at[pl.ds(dst_row, 1)], sem)


def _dispatch_kernel(e1_ref, e2_ref, r1_ref, r2_ref, ps_ref, xn_ref, xs_in_ref, xs_ref, sem):
    del xs_in_ref
    base = pl.program_id(0) * TD_DISPATCH

    def copies(r):
        t = base + r
        return (_row_copy(xn_ref, t, xs_ref, ps_ref[e1_ref[t]] + r1_ref[t], sem),
                _row_copy(xn_ref, t, xs_ref, ps_ref[e2_ref[t]] + r2_ref[t], sem))

    def start(r, c):
        for cp in copies(r):
            cp.start()
        return c

    def wait(r, c):
        for cp in copies(r):
            cp.wait()
        return c

    lax.fori_loop(0, TD_DISPATCH, start, 0)
    lax.fori_loop(0, TD_DISPATCH, wait, 0)


def _dispatch_call(e1, e2, r1, r2, pstart, xn, xs_zero):
    T = xn.shape[0]
    any_spec = pl.BlockSpec(memory_space=pl.ANY)
    grid_spec = pltpu.PrefetchScalarGridSpec(
        num_scalar_prefetch=5, grid=(T // TD_DISPATCH,), in_specs=[any_spec, any_spec], out_specs=any_spec,
        scratch_shapes=[pltpu.SemaphoreType.DMA(())])
    return pl.pallas_call(
        _dispatch_kernel, grid_spec=grid_spec, out_shape=jax.ShapeDtypeStruct(xs_zero.shape, xs_zero.dtype),
        input_output_aliases={6: 0},
        compiler_params=pltpu.CompilerParams(dimension_semantics=("arbitrary",), has_side_effects=True),
    )(e1, e2, r1, r2, pstart, xn, xs_zero)


def _expert_kernel(be_ref, used_ref, x_ref, wg_ref, wu_ref, wd_ref, y_ref):
    i = pl.program_id(0)

    @pl.when(i < used_ref[0])
    def _():
        x = x_ref[...].astype(BF16)
        g = _dot(x, wg_ref[0].astype(BF16))
        u = _dot(x, wu_ref[0].astype(BF16))
        act = g / (1.0 + jnp.exp(-g)) * u
        y_ref[...] = _dot(act.astype(BF16), wd_ref[0].astype(BF16))

    @pl.when(i >= used_ref[0])
    def _():
        y_ref[...] = jnp.zeros_like(y_ref)


def _expert_call(blk_expert, used, xs, wg, wu, wd):
    rows, D = xs.shape
    rb = MOE_ROW_BLOCK
    DE = wg.shape[2]
    xmap = lambda i, be, used: (jnp.minimum(i, used[0] - 1), 0)
    wmap = lambda i, be, used: (be[i], 0, 0)
    grid_spec = pltpu.PrefetchScalarGridSpec(
        num_scalar_prefetch=2, grid=(rows // rb,),
        in_specs=[pl.BlockSpec((rb, D), xmap), pl.BlockSpec((1, D, DE), wmap), pl.BlockSpec((1, D, DE), wmap),
                  pl.BlockSpec((1, DE, D), wmap)],
        out_specs=pl.BlockSpec((rb, D), lambda i, be, used: (i, 0)))
    return pl.pallas_call(
        _expert_kernel, grid_spec=grid_spec, out_shape=jax.ShapeDtypeStruct((rows, D), F32),
        compiler_params=_cparams(("arbitrary",)),
    )(blk_expert, used, xs, wg, wu, wd)


def _combine_kernel(final_norm, e1_ref, e2_ref, r1_ref, r2_ref, ps_ref, h_ref, rt_ref, gain_ref, ys_ref, o_ref,
                    buf_ref, sem):
    base = pl.program_id(0) * TC_COMBINE

    def copies(r):
        t = base + r
        return (_row_copy(ys_ref, ps_ref[e1_ref[t]] + r1_ref[t], buf_ref.at[0], r, sem),
                _row_copy(ys_ref, ps_ref[e2_ref[t]] + r2_ref[t], buf_ref.at[1], r, sem))

    def start(r, c):
        for cp in copies(r):
            cp.start()
        return c

    def wait(r, c):
        for cp in copies(r):
            cp.wait()
        return c

    lax.fori_loop(0, TC_COMBINE, start, 0)
    lax.fori_loop(0, TC_COMBINE, wait, 0)
    rt = rt_ref[...]
    out = h_ref[...] + rt[:, RT_W1:RT_W1 + 1] * buf_ref[0] + rt[:, RT_W2:RT_W2 + 1] * buf_ref[1]
    if final_norm:
        out = out * lax.rsqrt(jnp.mean(out * out, axis=-1, keepdims=True) + NORM_EPS) * gain_ref[...]
    o_ref[...] = out


def _combine_call(e1, e2, r1, r2, pstart, h, rt, gain, ys, final_norm):
    T, D = h.shape
    tc = TC_COMBINE
    tile = lambda i, *_: (i, 0)
    grid_spec = pltpu.PrefetchScalarGridSpec(
        num_scalar_prefetch=5, grid=(T // tc,),
        in_specs=[pl.BlockSpec((tc, D), tile), pl.BlockSpec((tc, LANES), tile),
                  pl.BlockSpec((1, D), lambda i, *_: (0, 0)), pl.BlockSpec(memory_space=pl.ANY)],
        out_specs=pl.BlockSpec((tc, D), tile),
        scratch_shapes=[pltpu.VMEM((2, tc, D), F32), pltpu.SemaphoreType.DMA(())])
    return pl.pallas_call(
        functools.partial(_combine_kernel, final_norm), grid_spec=grid_spec,
        out_shape=jax.ShapeDtypeStruct((T, D), F32),
        compiler_params=_cparams(("arbitrary",)),
    )(e1, e2, r1, r2, pstart, h, rt, gain, ys)


def _in_proj_columns():
    widths = (('q_a', 256), ('k_cmp', 64), ('v_cmp', 64), ('k_sel', 64), ('v_sel', 64), ('k_win', 64), ('v_win', 64),
              ('g_a', 12), ('q_b', 256), ('k_b', 256), ('v_b', 256), ('f_b', 4), ('q_c', 256), ('ckv_c', 128),
              ('qi_c', 256), ('ki_c', 64), ('wi_c', 4), ('q_d', 256), ('k_d', 128), ('v_d', 128))
    off, cols = 0, {}
    for name, w in widths:
        cols[name] = np.arange(off, off + w)
        off += w
    cat = lambda names: np.concatenate([cols[n] for n in names])
    rope = cat(('q_a', 'q_c', 'qi_c', 'q_d', 'k_d', 'k_cmp', 'k_sel', 'k_win', 'ki_c'))
    plain = cat(('q_b', 'k_b', 'v_b', 'v_d', 'ckv_c', 'v_cmp', 'v_sel', 'v_win'))
    small = cat(('g_a', 'f_b', 'wi_c'))
    assert rope.size == N_ROPE_COLS and plain.size == N_PLAIN_COLS
    return np.concatenate([rope, plain, small]), N_SMALL_COLS - small.size


def _static_tables(S):
    n_cmp_pad = S // NSA_CMP_STRIDE
    n_blk = S // NSA_SEL_BLOCK
    c0 = np.arange(n_cmp_pad) * NSA_CMP_STRIDE
    b0 = np.arange(n_blk) * NSA_SEL_BLOCK
    ov = ((c0[:, None] < b0[None, :] + NSA_SEL_BLOCK) & (c0[:, None] + NSA_CMP_LEN > b0[None, :])).astype(np.float32)
    overlap = np.zeros((n_cmp_pad, LANES), np.float32)
    overlap[:, :n_blk] = ov
    expand = np.zeros((LANES, S), np.float32)
    expand[np.arange(S) // NSA_SEL_BLOCK, np.arange(S)] = 1.0
    strict_upper = (np.arange(LANES)[:, None] < np.arange(LANES)[None, :]).astype(np.float32)
    cntmat = np.concatenate([strict_upper, np.ones((LANES, LANES), np.float32)], axis=1)
    return jnp.asarray(overlap, BF16), jnp.asarray(expand, BF16), jnp.asarray(cntmat, BF16)


def kernel(x, positions, attn_norm, w_in, nsa_pe_k, nsa_w1_k, nsa_w2_k, nsa_pe_v, nsa_w1_v, nsa_w2_v, fox_forget_bias, dsa_kv_norm, dsa_w_ukv, swa_sinks, w_branch, w_gate, w_out, ffn_norm, moe_w_group, moe_b_group, moe_w_expert, moe_b_expert, moe_w_gate, moe_w_up, moe_w_down, final_norm):
    B, S, D = x.shape
    T = B * S
    depth = w_in.shape[0]
    perm, n_pad = _in_proj_columns()
    overlap, expand, cntmat = _static_tables(S)
    half = HEAD_DIM // 2
    inv_freq = ROPE_THETA ** (-jnp.arange(half, dtype=F32) / half)
    invf = jnp.tile(inv_freq, LANES // half).reshape(1, LANES)
    pos = positions.reshape(T, 1).astype(I32)
    small_pad = lambda v, off: jnp.zeros((1, LANES), F32).at[0, off:off + v.shape[0]].set(v.astype(F32))

    n_rows = -(-(T * 2 + MOE_N_EXPERTS * (MOE_ROW_BLOCK - 1)) // MOE_ROW_BLOCK) * MOE_ROW_BLOCK
    h = x.reshape(T, D)
    for l in range(depth):
        w = jnp.pad(w_in[l][:, perm], ((0, 0), (0, n_pad))).astype(BF16)
        (qa, qc, qi, qd, kd, kcmp, ksel, kwin, ki, qb, kb, vb, vd, vcmp, vsel, vwin, kc, vc, sm) = _proj_call(
            h, pos, invf, attn_norm[l].reshape(1, D), w, small_pad(fox_forget_bias[l], SM_F0),
            dsa_kv_norm[l].reshape(1, DSA_KV_RANK), dsa_w_ukv[l].astype(BF16), B, S)

        cw = NSA_CMP_STRIDE * HEAD_DIM
        pe_rows = lambda pe: jnp.broadcast_to(pe.reshape(1, 2 * cw), (8, 2 * cw)).astype(BF16)
        kcc, vcc = _cmp_call(kcmp.reshape(T // NSA_CMP_STRIDE, cw), vcmp.reshape(T // NSA_CMP_STRIDE, cw),
                             pe_rows(nsa_pe_k[l]), nsa_w1_k[l].astype(BF16), nsa_w2_k[l].astype(BF16),
                             pe_rows(nsa_pe_v[l]), nsa_w1_v[l].astype(BF16), nsa_w2_v[l].astype(BF16), B, S)
        o_a = _nsa_call(qa, sm, kcc, vcc, overlap, expand, ksel, vsel, kwin, vwin, B, S)

        crow = sm[:, SM_F0:SM_F0 + N_HEADS].reshape(B, S, N_HEADS).transpose(0, 2, 1)
        o_b = _fox_call(qb, sm, crow, kb, vb, B, S)
        o_c = _dsa_call(qc, qi, sm, ki, cntmat, kc, vc, B, S)
        o_d = _swa_call(qd, small_pad(swa_sinks[l], 0), kd, vd, B, S)

        wr = jnp.zeros((D, LANES), F32).at[:, :MOE_GROUPS].set(moe_w_group[l]) \
            .at[:, ROUTER_E0:ROUTER_E0 + MOE_N_EXPERTS].set(moe_w_expert[l])
        br = jnp.zeros((1, LANES), F32).at[0, :MOE_GROUPS].set(moe_b_group[l]) \
            .at[0, ROUTER_E0:ROUTER_E0 + MOE_N_EXPERTS].set(moe_b_expert[l])
        h2, xn, rt, cnt = _merge_call(h, o_a, o_b, o_c, o_d, attn_norm[l].reshape(1, D), w_gate[l].astype(BF16),
                                      w_branch[l].astype(BF16), w_out[l].astype(BF16), ffn_norm[l].reshape(1, D),
                                      wr, br)

        counts = cnt[0, ROUTER_E0:ROUTER_E0 + MOE_N_EXPERTS].astype(I32)
        padded = (counts + MOE_ROW_BLOCK - 1) // MOE_ROW_BLOCK * MOE_ROW_BLOCK
        p_end = jnp.cumsum(padded)
        p_start = (p_end - padded).astype(I32)
        n_blocks = n_rows // MOE_ROW_BLOCK
        blk_first_row = jnp.arange(n_blocks, dtype=I32) * MOE_ROW_BLOCK
        blk_expert = jnp.minimum(jnp.sum((p_end[None, :] <= blk_first_row[:, None]).astype(I32), axis=1),
                                 MOE_N_EXPERTS - 1).astype(I32)
        used = (p_end[-1:] // MOE_ROW_BLOCK).astype(I32)
        ids = rt[:, :8].astype(I32)
        e1, e2, r1, r2 = ids[:, RT_E1], ids[:, RT_E2], ids[:, RT_R1], ids[:, RT_R2]

        xs = _dispatch_call(e1, e2, r1, r2, p_start, xn, jnp.zeros((n_rows, D), F32))
        ys = _expert_call(blk_expert, used, xs, moe_w_gate[l], moe_w_up[l], moe_w_down[l])
        last = l == depth - 1
        h = _combine_call(e1, e2, r1, r2, p_start, h2, rt, final_norm.reshape(1, D), ys, last)
    return h.reshape(B, S, D)
```

```python
import functools

import numpy as np
import jax
import jax.numpy as jnp
from jax import lax
from jax.experimental import pallas as pl
from jax.experimental.pallas import tpu as pltpu

F32 = jnp.float32
BF16 = jnp.bfloat16
I32 = jnp.int32

HEAD_DIM = 64
N_HEADS = 4
MIX_WIDTH = N_HEADS * HEAD_DIM
ROPE_THETA = 10000.0
NORM_EPS = 1e-6
NEG_INF = -1e30
INT_MIN = -2 ** 31
LOG2E = float(np.log2(np.e))

NSA_CMP_LEN = 32
NSA_CMP_STRIDE = 16
NSA_CMP_HIDDEN = 256
NSA_SEL_BLOCK = 64
NSA_N_SEL = 16
NSA_WINDOW = 512
NSA_FORCE_SCORE = 1e4
DSA_KV_RANK = 128
DSA_TOPK_MAX = 256
SWA_WINDOW = 128
SWA_KV_HEADS = 2
MOE_GROUPS = 8
MOE_EXPERTS_PER_GROUP = 8
MOE_N_EXPERTS = 64
MOE_D_EXPERT = 256

LANES = 128
VMEM_LIMIT = 48 * 1024 * 1024

N_ROPE_COLS = 1408
N_PLAIN_COLS = 1216
N_SMALL_COLS = 128
N_PROJ_COLS = N_ROPE_COLS + N_PLAIN_COLS + N_SMALL_COLS
SM_GATE0, SM_F0, SM_W0 = 0, 12, 16

TM_PROJ = 512
TQ = 128
TM_MERGE = 512
MOE_ROW_BLOCK = 256
TD_DISPATCH = 256
TC_COMBINE = 256


def _cparams(sem):
    return pltpu.CompilerParams(dimension_semantics=sem, vmem_limit_bytes=VMEM_LIMIT)


def _dot(a, b):
    return jnp.dot(a, b, preferred_element_type=F32)


def _dot_t(a, b):
    return lax.dot_general(a, b, (((1,), (1,)), ((), ())), preferred_element_type=F32)


def _iota(shape, dim):
    return lax.broadcasted_iota(I32, shape, dim)


def _rope_slab(x, cos, sin_signed, first_half):
    rot = jnp.where(first_half, pltpu.roll(x, 96, 1), pltpu.roll(x, 32, 1))
    return x * cos + rot * sin_signed


def _proj_kernel(h_ref, pos_ref, invf_ref, gain_ref, w_ref, fbias_ref, kvn_ref, wukv_ref,
                 qa_ref, qc_ref, qi_ref, qd_ref, kd_ref, kcmp_ref, ksel_ref, kwin_ref, ki_ref,
                 qb_ref, kb_ref, vb_ref, vd_ref, vcmp_ref, vsel_ref, vwin_ref, kc_ref, vc_ref,
                 sm_ref, carry_ref):
    j = pl.program_id(1)
    tm = h_ref.shape[0]
    h = h_ref[...]
    hn = h * lax.rsqrt(jnp.mean(h * h, axis=-1, keepdims=True) + NORM_EPS) * gain_ref[...]
    hb = hn.astype(BF16)

    ang = pos_ref[...].astype(F32) * invf_ref[...]
    cos = jnp.cos(ang)
    sin = jnp.sin(ang)
    lane = _iota((tm, LANES), 1)
    first_half = (lane % HEAD_DIM) < (HEAD_DIM // 2)
    sin_signed = jnp.where(first_half, -sin, sin)
    rope = lambda x: _rope_slab(x, cos, sin_signed, first_half)

    zr = _dot(hb, w_ref[:, 0:N_ROPE_COLS])
    slab = lambda k: rope(zr[:, k * LANES:(k + 1) * LANES])
    scale = HEAD_DIM ** -0.5
    softmax_scale = scale * LOG2E
    for q_ref, k0, sc in ((qa_ref, 0, softmax_scale), (qc_ref, 2, softmax_scale), (qi_ref, 4, scale),
                          (qd_ref, 6, softmax_scale)):
        q_ref[...] = (jnp.concatenate([slab(k0), slab(k0 + 1)], axis=1) * sc).astype(BF16)
    s8 = slab(8).astype(BF16)
    kd_ref[0] = s8[:, :HEAD_DIM]
    kd_ref[1] = s8[:, HEAD_DIM:]
    s9 = slab(9).astype(BF16)
    kcmp_ref[...] = s9[:, :HEAD_DIM]
    ksel_ref[...] = s9[:, HEAD_DIM:]
    s10 = slab(10).astype(BF16)
    kwin_ref[...] = s10[:, :HEAD_DIM]
    ki_ref[...] = s10[:, HEAD_DIM:]

    zp = _dot(hb, w_ref[:, N_ROPE_COLS:N_ROPE_COLS + N_PLAIN_COLS])
    qb_ref[...] = (zp[:, 0:256] * softmax_scale).astype(BF16)
    for hh in range(N_HEADS):
        kb_ref[hh] = zp[:, 256 + hh * 64:256 + (hh + 1) * 64].astype(BF16)
        vb_ref[hh] = zp[:, 512 + hh * 64:512 + (hh + 1) * 64].astype(BF16)
    vd_ref[0] = zp[:, 768:832].astype(BF16)
    vd_ref[1] = zp[:, 832:896].astype(BF16)
    ckv = zp[:, 896:1024]
    vcmp_ref[...] = zp[:, 1024:1088].astype(BF16)
    vsel_ref[...] = zp[:, 1088:1152].astype(BF16)
    vwin_ref[...] = zp[:, 1152:1216].astype(BF16)

    ckvn = ckv * lax.rsqrt(jnp.mean(ckv * ckv, axis=-1, keepdims=True) + NORM_EPS) * kvn_ref[...]
    kvc = _dot(ckvn.astype(BF16), wukv_ref[...])
    for k in range(2):
        kk = rope(kvc[:, k * LANES:(k + 1) * LANES]).astype(BF16)
        kc_ref[2 * k] = kk[:, :HEAD_DIM]
        kc_ref[2 * k + 1] = kk[:, HEAD_DIM:]
    for hh in range(N_HEADS):
        vc_ref[hh] = kvc[:, 256 + hh * 64:256 + (hh + 1) * 64].astype(BF16)

    zs = _dot(hb, w_ref[:, N_ROPE_COLS + N_PLAIN_COLS:N_PROJ_COLS])
    sig = 1.0 / (1.0 + jnp.exp(-zs))
    xf = zs + fbias_ref[...]
    logf = jnp.minimum(xf, 0.0) - jnp.log(1.0 + jnp.exp(-jnp.abs(xf)))
    is_f = (lane >= SM_F0) & (lane < SM_W0)
    logf = jnp.where(is_f, logf, 0.0)

    @pl.when(j == 0)
    def _():
        carry_ref[...] = jnp.zeros_like(carry_ref)

    tri = (_iota((tm, tm), 0) >= _iota((tm, tm), 1)).astype(F32)
    cum = jnp.dot(tri, logf, preferred_element_type=F32, precision=lax.Precision.HIGHEST)
    cum = cum + carry_ref[0:1, :]
    carry_ref[0:1, :] = cum[tm - 1:tm, :]
    sm_ref[...] = jnp.where(lane < SM_F0, sig,
                            jnp.where(is_f, -LOG2E * cum, jnp.where(lane < SM_W0 + 4, 0.5 * zs, 0.0)))


def _proj_call(h, pos, invf, gain, w, fbias, kvn, wukv, B, S):
    T = B * S
    tm = TM_PROJ
    nj = S // tm
    row = lambda b, j: (b * nj + j, 0)
    hrow = lambda b, j: (0, b * nj + j, 0)
    const2 = lambda b, j: (0, 0)
    tok = lambda w_, dt: jax.ShapeDtypeStruct((T, w_), dt)
    hm = lambda n: jax.ShapeDtypeStruct((n, T, HEAD_DIM), BF16)
    out_shape = [tok(256, BF16)] * 4 + [hm(2)] + [tok(64, BF16)] * 4 + [tok(256, BF16), hm(4), hm(4), hm(2)] \
        + [tok(64, BF16)] * 3 + [hm(4), hm(4), tok(LANES, F32)]
    spec_tok = lambda w_: pl.BlockSpec((tm, w_), row)
    spec_hm = lambda n: pl.BlockSpec((n, tm, HEAD_DIM), hrow)
    out_specs = [spec_tok(256)] * 4 + [spec_hm(2)] + [spec_tok(64)] * 4 + [spec_tok(256), spec_hm(4), spec_hm(4), spec_hm(2)] \
        + [spec_tok(64)] * 3 + [spec_hm(4), spec_hm(4), spec_tok(LANES)]
    in_specs = [
        pl.BlockSpec((tm, h.shape[1]), row),
        pl.BlockSpec((tm, 1), row),
        pl.BlockSpec((1, LANES), const2),
        pl.BlockSpec((1, h.shape[1]), const2),
        pl.BlockSpec(w.shape, const2),
        pl.BlockSpec((1, LANES), const2),
        pl.BlockSpec((1, DSA_KV_RANK), const2),
        pl.BlockSpec(wukv.shape, const2),
    ]
    return pl.pallas_call(
        _proj_kernel, grid=(B, nj), in_specs=in_specs, out_specs=out_specs, out_shape=out_shape,
        scratch_shapes=[pltpu.VMEM((8, LANES), F32)],
        compiler_params=_cparams(("arbitrary", "arbitrary")),
    )(h, pos, invf, gain, w, fbias, kvn, wukv)


def _gelu_tanh(x):
    return 0.5 * x * (1.0 + jnp.tanh(np.sqrt(2.0 / np.pi).astype(np.float32) * (x + 0.044715 * (x * x * x))))


def _cmp_kernel(ck_ref, cv_ref, pek_ref, w1k_ref, w2k_ref, pev_ref, w1v_ref, w2v_ref, kc_ref, vc_ref):
    half = NSA_CMP_STRIDE * HEAD_DIM
    for c_ref, pe_ref, w1_ref, w2_ref, o_ref in ((ck_ref, pek_ref, w1k_ref, w2k_ref, kc_ref),
                                                 (cv_ref, pev_ref, w1v_ref, w2v_ref, vc_ref)):
        c = c_ref[...]
        a = _dot(c, w1_ref[0:half, :])
        b = _dot(c, w1_ref[half:2 * half, :])
        n = b.shape[0]
        pe_h = _dot(pe_ref[...], w1_ref[...])
        hid = a + pltpu.roll(b, n - 1, 0) + pe_h[0:1, :]
        o_ref[...] = _dot(_gelu_tanh(hid).astype(BF16), w2_ref[...]).astype(BF16)


def _cmp_call(ck, cv, pek, w1k, w2k, pev, w1v, w2v, B, S):
    nch = S // NSA_CMP_STRIDE
    blk = lambda b: (b, 0)
    const = lambda b: (0, 0)
    cw = NSA_CMP_STRIDE * HEAD_DIM
    in_specs = [pl.BlockSpec((nch, cw), blk), pl.BlockSpec((nch, cw), blk)]
    for _ in range(2):
        in_specs += [pl.BlockSpec((8, 2 * cw), const), pl.BlockSpec((2 * cw, NSA_CMP_HIDDEN), const),
                     pl.BlockSpec((NSA_CMP_HIDDEN, HEAD_DIM), const)]
    out = jax.ShapeDtypeStruct((B * nch, HEAD_DIM), BF16)
    return pl.pallas_call(
        _cmp_kernel, grid=(B,), in_specs=in_specs,
        out_specs=[pl.BlockSpec((nch, HEAD_DIM), blk)] * 2, out_shape=[out, out],
        compiler_params=_cparams(("arbitrary",)),
    )(ck, cv, pek, w1k, w2k, pev, w1v, w2v)


def _attend(q, segments, extra_logit=None):
    scores = []
    for k, _, bias, allowed in segments:
        s = _dot_t(q, k)
        if bias is not None:
            s = s + bias
        if allowed is not None:
            s = jnp.where(allowed, s, NEG_INF)
        scores.append(s)
    m = functools.reduce(jnp.maximum, [jnp.max(s, axis=-1, keepdims=True) for s in scores])
    if extra_logit is not None:
        m = jnp.maximum(m, extra_logit)
    den = jnp.exp2(extra_logit - m) if extra_logit is not None else 0.0
    o = 0.0
    for s, (_, v, _, _) in zip(scores, segments):
        e = jnp.exp2(s - m)
        den = den + jnp.sum(e, axis=-1, keepdims=True)
        o = o + _dot(e.astype(BF16), v)
    return o / den


def _causal_split(i, tq, nk, S):
    n_full = nk - S // N_CAUSAL_CLASSES
    t_col = i * tq + _iota((tq, 1), 0)
    tail_causal = (n_full + _iota((1, nk - n_full), 1)) <= t_col
    return n_full, tail_causal


N_CAUSAL_CLASSES = 4


def _for_causal_class(i, tq, S, body):
    step = S // N_CAUSAL_CLASSES
    cls = (i * tq + tq - 1) // step
    for c in range(N_CAUSAL_CLASSES):
        pl.when(cls == c)(functools.partial(body, (c + 1) * step))


def _nsa_kernel(q_ref, sm_ref, kc_ref, vc_ref, ov_ref, ex_ref, ksel_ref, vsel_ref, kwin_ref, vwin_ref, o_ref):
    i = pl.program_id(1)
    tq = q_ref.shape[0]
    S = ksel_ref.shape[0]
    ncmp = kc_ref.shape[0]
    q = q_ref[...]
    sm = sm_ref[...]
    t_col = i * tq + _iota((tq, 1), 0)

    c_end = _iota((1, ncmp), 1) * NSA_CMP_STRIDE + (NSA_CMP_LEN - 1)
    allowed_c = c_end <= t_col
    kc = kc_ref[...]
    vc = vc_ref[...]
    o_cmp = []
    imp = jnp.zeros((tq, LANES), F32)
    for hh in range(N_HEADS):
        qh = q[:, hh * 64:(hh + 1) * 64]
        s = jnp.where(allowed_c, _dot_t(qh, kc), NEG_INF)
        m = jnp.max(s, axis=-1, keepdims=True)
        e = jnp.where(allowed_c, jnp.exp2(s - m), 0.0)
        den = jnp.sum(e, axis=-1, keepdims=True)
        p = e * jnp.where(den > 0.0, 1.0 / den, 0.0)
        pb = p.astype(BF16)
        o_cmp.append(_dot(pb, vc))
        imp = imp + _dot(pb, ov_ref[...])

    n_blk = S // NSA_SEL_BLOCK
    jl = _iota((tq, LANES), 1)
    cur = t_col // NSA_SEL_BLOCK
    forced = (jl == 0) | (jl == cur) | (jl == cur - 1)
    imp = jnp.where(forced, NSA_FORCE_SCORE, imp)
    imp = jnp.where(jl > cur, -NSA_FORCE_SCORE, imp)
    rank = jnp.zeros((tq, LANES), F32)
    for jp in range(n_blk):
        col = imp[:, jp:jp + 1]
        ahead = (col > imp) | ((col == imp) & (jp < jl))
        rank = rank + jnp.where(ahead, 1.0, 0.0)
    unselected = jnp.where((rank < float(min(NSA_N_SEL, n_blk))) & (jl < n_blk), 0.0, NEG_INF).astype(BF16)

    nband = NSA_WINDOW + tq
    start = pl.multiple_of(jnp.maximum(i * tq - NSA_WINDOW, 0), tq)
    kwin = kwin_ref[pl.ds(start, nband), :]
    vwin = vwin_ref[pl.ds(start, nband), :]
    s_band = start + _iota((1, nband), 1)
    allowed_win = (s_band <= t_col) & (s_band > t_col - NSA_WINDOW)
    gate = lambda hh, k: sm[:, SM_GATE0 + 3 * hh + k:SM_GATE0 + 3 * hh + k + 1]
    partial_out = []
    for hh in range(N_HEADS):
        o_win = _attend(q[:, hh * 64:(hh + 1) * 64], [(kwin, vwin, None, allowed_win)])
        partial_out.append(gate(hh, 0) * o_cmp[hh] + gate(hh, 2) * o_win)

    def selected(nk):
        bias = _dot(unselected, ex_ref[:, 0:nk])
        n_full, tail_causal = _causal_split(i, tq, nk, S)
        segments = [(ksel_ref[n_full:nk, :], vsel_ref[n_full:nk, :], bias[:, n_full:nk], tail_causal)]
        if n_full:
            segments.append((ksel_ref[0:n_full, :], vsel_ref[0:n_full, :], bias[:, 0:n_full], None))
        outs = []
        for hh in range(N_HEADS):
            o_sel = _attend(q[:, hh * 64:(hh + 1) * 64], segments)
            outs.append(partial_out[hh] + gate(hh, 1) * o_sel)
        o_ref[...] = jnp.concatenate(outs, axis=1).astype(BF16)

    _for_causal_class(i, tq, S, selected)


def _nsa_call(qa, sm, kcc, vcc, overlap, expand, ksel, vsel, kwin, vwin, B, S):
    T = B * S
    nq = S // TQ
    ncmp = S // NSA_CMP_STRIDE
    tile = lambda b, i: (b * nq + i, 0)
    seq = lambda b, i: (b, 0)
    const = lambda b, i: (0, 0)
    in_specs = [pl.BlockSpec((TQ, 256), tile), pl.BlockSpec((TQ, LANES), tile),
                pl.BlockSpec((ncmp, 64), seq), pl.BlockSpec((ncmp, 64), seq),
                pl.BlockSpec(overlap.shape, const), pl.BlockSpec(expand.shape, const)] \
        + [pl.BlockSpec((S, 64), seq)] * 4
    return pl.pallas_call(
        _nsa_kernel, grid=(B, nq), in_specs=in_specs, out_specs=pl.BlockSpec((TQ, 256), tile),
        out_shape=jax.ShapeDtypeStruct((T, 256), BF16),
        compiler_params=_cparams(("arbitrary", "arbitrary")),
    )(qa, sm, kcc, vcc, overlap, expand, ksel, vsel, kwin, vwin)


def _fox_kernel(q_ref, nbias_ref, k_ref, v_ref, o_ref):
    i = pl.program_id(1)
    tq = q_ref.shape[0]
    S = k_ref.shape[1]
    q = q_ref[...]

    def attend(nk):
        n_full, tail_causal = _causal_split(i, tq, nk, S)
        outs = []
        for hh in range(N_HEADS):
            segments = [(k_ref[hh, n_full:nk, :], v_ref[hh, n_full:nk, :], nbias_ref[0, hh:hh + 1, n_full:nk],
                         tail_causal)]
            if n_full:
                segments.append((k_ref[hh, 0:n_full, :], v_ref[hh, 0:n_full, :], nbias_ref[0, hh:hh + 1, 0:n_full],
                                 None))
            outs.append(_attend(q[:, hh * 64:(hh + 1) * 64], segments))
        o_ref[...] = jnp.concatenate(outs, axis=1).astype(BF16)

    _for_causal_class(i, tq, S, attend)


def _fox_call(qb, nbias, kb, vb, B, S):
    T = B * S
    nq = S // TQ
    tile = lambda b, i: (b * nq + i, 0)
    in_specs = [pl.BlockSpec((TQ, 256), tile),
                pl.BlockSpec((1, N_HEADS, S), lambda b, i: (b, 0, 0)),
                pl.BlockSpec((N_HEADS, S, 64), lambda b, i: (0, b, 0)),
                pl.BlockSpec((N_HEADS, S, 64), lambda b, i: (0, b, 0))]
    return pl.pallas_call(
        _fox_kernel, grid=(B, nq), in_specs=in_specs, out_specs=pl.BlockSpec((TQ, 256), tile),
        out_shape=jax.ShapeDtypeStruct((T, 256), BF16),
        compiler_params=_cparams(("arbitrary", "arbitrary")),
    )(qb, nbias, kb, vb)


def _dsa_kernel(q_ref, qi_ref, sm_ref, ki_ref, cnt_ref, k_ref, v_ref, o_ref, score_ref):
    i = pl.program_id(1)
    tq = q_ref.shape[0]
    S = ki_ref.shape[0]
    sm = sm_ref[...]
    qi = qi_ref[...]
    t_col = i * tq + _iota((tq, 1), 0)
    top_k = min(DSA_TOPK_MAX, S // 4)
    k_eff = jnp.minimum(t_col + 1, top_k).astype(F32)
    q = q_ref[...]

    def attend(nk, bias, allowed):
        outs = [_attend(q[:, hh * 64:(hh + 1) * 64], [(k_ref[hh, 0:nk, :], v_ref[hh, 0:nk, :], bias, allowed)])
                for hh in range(N_HEADS)]
        o_ref[...] = jnp.concatenate(outs, axis=1).astype(BF16)

    def float_of_ordered_bits(u):
        k = u ^ INT_MIN
        return lax.bitcast_convert_type(jnp.where(k >= 0, k, k ^ 0x7FFFFFFF), F32)

    def select_and_attend(nk):
        causal = _iota((1, nk), 1) <= t_col
        score = jnp.zeros((tq, nk), F32)
        for hh in range(N_HEADS):
            lg = _dot_t(qi[:, hh * 64:(hh + 1) * 64], ki_ref[0:nk, :])
            score = score + sm[:, SM_W0 + hh:SM_W0 + hh + 1] * jnp.maximum(lg, 0.0)
        score_ref[:, 0:nk] = jnp.where(causal, score, -jnp.inf)

        def search(it, u):
            cand = u | lax.shift_left(jnp.int32(1), 31 - it)
            thr = float_of_ordered_bits(cand)
            cnt = jnp.sum(jnp.where(score_ref[:, 0:nk] >= thr, 1.0, 0.0), axis=-1, keepdims=True)
            return jnp.where(cnt >= k_eff, cand, u)

        u = lax.fori_loop(0, 32, search, jnp.zeros((tq, 1), I32))
        thr = float_of_ordered_bits(u)
        score = score_ref[:, 0:nk]
        gt = score > thr
        eq = score == thr
        need = k_eff - jnp.sum(jnp.where(gt, 1.0, 0.0), axis=-1, keepdims=True)
        eqb = jnp.where(eq, 1.0, 0.0).astype(BF16)
        run = jnp.zeros((tq, LANES), F32)
        bias_chunks = []
        for c in range(nk // LANES):
            sl = slice(c * LANES, (c + 1) * LANES)
            r = _dot(eqb[:, sl], cnt_ref[...])
            prefix = r[:, :LANES] + run
            run = run + r[:, LANES:]
            keep = gt[:, sl] | (eq[:, sl] & (prefix < need))
            bias_chunks.append(jnp.where(keep, 0.0, NEG_INF))
        attend(nk, jnp.concatenate(bias_chunks, axis=1), None)

    def body(nk):
        if nk * N_CAUSAL_CLASSES == S:
            all_kept = (i + 1) * tq <= top_k
            pl.when(all_kept)(lambda: attend(nk, None, _iota((1, nk), 1) <= t_col))
            pl.when(jnp.logical_not(all_kept))(lambda: select_and_attend(nk))
        else:
            select_and_attend(nk)

    _for_causal_class(i, tq, S, body)


def _dsa_call(qc, qi, sm, ki, cntmat, kc, vc, B, S):
    T = B * S
    nq = S // TQ
    tile = lambda b, i: (b * nq + i, 0)
    in_specs = [pl.BlockSpec((TQ, 256), tile), pl.BlockSpec((TQ, 256), tile), pl.BlockSpec((TQ, LANES), tile),
                pl.BlockSpec((S, 64), lambda b, i: (b, 0)),
                pl.BlockSpec(cntmat.shape, lambda b, i: (0, 0)),
                pl.BlockSpec((N_HEADS, S, 64), lambda b, i: (0, b, 0)),
                pl.BlockSpec((N_HEADS, S, 64), lambda b, i: (0, b, 0))]
    return pl.pallas_call(
        _dsa_kernel, grid=(B, nq), in_specs=in_specs, out_specs=pl.BlockSpec((TQ, 256), tile),
        out_shape=jax.ShapeDtypeStruct((T, 256), BF16),
        scratch_shapes=[pltpu.VMEM((TQ, S), F32)],
        compiler_params=_cparams(("arbitrary", "arbitrary")),
    )(qc, qi, sm, ki, cntmat, kc, vc)


def _swa_kernel(q_ref, sink_ref, k_ref, v_ref, o_ref):
    i = pl.program_id(1)
    tq = q_ref.shape[0]
    q = q_ref[...]
    t_col = i * tq + _iota((tq, 1), 0)
    nband = SWA_WINDOW + tq
    start = pl.multiple_of(jnp.maximum(i * tq - SWA_WINDOW, 0), tq)
    s_band = start + _iota((1, nband), 1)
    allowed = (s_band <= t_col) & (s_band > t_col - SWA_WINDOW)
    group = N_HEADS // SWA_KV_HEADS
    outs = []
    for hh in range(N_HEADS):
        k = k_ref[hh // group, pl.ds(start, nband), :]
        v = v_ref[hh // group, pl.ds(start, nband), :]
        sink = jnp.zeros((tq, 1), F32) + LOG2E * sink_ref[0:1, hh:hh + 1]
        outs.append(_attend(q[:, hh * 64:(hh + 1) * 64], [(k, v, None, allowed)], extra_logit=sink))
    o_ref[...] = jnp.concatenate(outs, axis=1).astype(BF16)


def _swa_call(qd, sinks, kd, vd, B, S):
    T = B * S
    nq = S // TQ
    tile = lambda b, i: (b * nq + i, 0)
    in_specs = [pl.BlockSpec((TQ, 256), tile), pl.BlockSpec((1, LANES), lambda b, i: (0, 0)),
                pl.BlockSpec((SWA_KV_HEADS, S, 64), lambda b, i: (0, b, 0)),
                pl.BlockSpec((SWA_KV_HEADS, S, 64), lambda b, i: (0, b, 0))]
    return pl.pallas_call(
        _swa_kernel, grid=(B, nq), in_specs=in_specs, out_specs=pl.BlockSpec((TQ, 256), tile),
        out_shape=jax.ShapeDtypeStruct((T, 256), BF16),
        compiler_params=_cparams(("arbitrary", "arbitrary")),
    )(qd, sinks, kd, vd)


RT_E1, RT_E2, RT_W1, RT_W2, RT_R1, RT_R2 = 0, 1, 2, 3, 4, 5
ROUTER_E0 = MOE_GROUPS


def _merge_kernel(h_ref, oa_ref, ob_ref, oc_ref, od_ref, gain_ref, wg_ref, wb_ref, wo_ref,
                  fgain_ref, wr_ref, br_ref, h2_ref, xn_ref, rt_ref, cnt_ref, base_ref):
    step = pl.program_id(0)
    tm = h_ref.shape[0]
    h = h_ref[...]
    hn = h * lax.rsqrt(jnp.mean(h * h, axis=-1, keepdims=True) + NORM_EPS) * gain_ref[...]
    hb = hn.astype(BF16)
    merged = jnp.zeros_like(h)
    for i, o_ref in enumerate((oa_ref, ob_ref, oc_ref, od_ref)):
        gate = 1.0 / (1.0 + jnp.exp(-_dot(hb, wg_ref[i])))
        merged = merged + gate * _dot(o_ref[...], wb_ref[i])
    h2 = h + _dot(merged.astype(BF16), wo_ref[...])
    h2_ref[...] = h2

    xn = h2 * lax.rsqrt(jnp.mean(h2 * h2, axis=-1, keepdims=True) + NORM_EPS) * fgain_ref[...]
    xn_ref[...] = xn
    logits = jnp.dot(xn, wr_ref[...], preferred_element_type=F32, precision=lax.Precision.HIGHEST) + br_ref[...]

    lane = _iota((tm, LANES), 1)
    lanef = lane.astype(F32)
    big = float(LANES)
    is_g = lane < MOE_GROUPS
    gl = jnp.where(is_g, logits, NEG_INF)
    gmax = jnp.max(gl, axis=-1, keepdims=True)
    grp = jnp.min(jnp.where(is_g & (gl == gmax), lanef, big), axis=-1, keepdims=True)
    p_grp = 1.0 / jnp.sum(jnp.where(is_g, jnp.exp(gl - gmax), 0.0), axis=-1, keepdims=True)
    lo = ROUTER_E0 + grp * MOE_EXPERTS_PER_GROUP
    in_grp = (lanef >= lo) & (lanef < lo + MOE_EXPERTS_PER_GROUP)
    el = jnp.where(in_grp, logits, NEG_INF)
    v1 = jnp.max(el, axis=-1, keepdims=True)
    l1 = jnp.min(jnp.where(in_grp & (el == v1), lanef, big), axis=-1, keepdims=True)
    el2 = jnp.where(lanef == l1, NEG_INF, el)
    v2 = jnp.max(el2, axis=-1, keepdims=True)
    l2 = jnp.min(jnp.where(in_grp & (lanef != l1) & (el2 == v2), lanef, big), axis=-1, keepdims=True)
    e21 = jnp.exp(v2 - v1)
    w1 = p_grp / (1.0 + e21)
    w2 = p_grp * e21 / (1.0 + e21)

    oh1 = jnp.where(lanef == l1, 1.0, 0.0)
    oh2 = jnp.where(lanef == l2, 1.0, 0.0)

    @pl.when(step == 0)
    def _():
        base_ref[...] = jnp.zeros_like(base_ref)

    both = oh1 + oh2
    strict = (_iota((tm, tm), 0) > _iota((tm, tm), 1)).astype(BF16)
    before = _dot(strict, both.astype(BF16)) + base_ref[0:1, :]
    r1 = jnp.sum(oh1 * before, axis=-1, keepdims=True)
    r2 = jnp.sum(oh2 * before, axis=-1, keepdims=True)
    total = base_ref[0:1, :] + jnp.sum(both, axis=0, keepdims=True)
    base_ref[0:1, :] = total
    cnt_ref[...] = jnp.broadcast_to(total, cnt_ref.shape)

    rt = jnp.zeros((tm, LANES), F32)
    for ln, val in ((RT_E1, l1 - ROUTER_E0), (RT_E2, l2 - ROUTER_E0), (RT_W1, w1), (RT_W2, w2), (RT_R1, r1), (RT_R2, r2)):
        rt = jnp.where(lane == ln, val, rt)
    rt_ref[...] = rt


def _merge_call(h, oa, ob, oc, od, gain, wg, wb, wo, fgain, wr, br):
    T, D = h.shape
    tm = TM_MERGE
    tile = lambda i: (i, 0)
    c2 = lambda i: (0, 0)
    c3 = lambda i: (0, 0, 0)
    once = pl.Buffered(1)
    in_specs = [pl.BlockSpec((tm, D), tile)] + [pl.BlockSpec((tm, 256), tile)] * 4 + [
        pl.BlockSpec((1, D), c2),
        pl.BlockSpec(wg.shape, c3, pipeline_mode=once),
        pl.BlockSpec(wb.shape, c3, pipeline_mode=once),
        pl.BlockSpec(wo.shape, c2, pipeline_mode=once),
        pl.BlockSpec((1, D), c2),
        pl.BlockSpec(wr.shape, c2),
        pl.BlockSpec((1, LANES), c2),
    ]
    out_shape = [jax.ShapeDtypeStruct((T, D), F32), jax.ShapeDtypeStruct((T, D), F32),
                 jax.ShapeDtypeStruct((T, LANES), F32), jax.ShapeDtypeStruct((8, LANES), F32)]
    out_specs = [pl.BlockSpec((tm, D), tile), pl.BlockSpec((tm, D), tile), pl.BlockSpec((tm, LANES), tile),
                 pl.BlockSpec((8, LANES), c2)]
    return pl.pallas_call(
        _merge_kernel, grid=(T // tm,), in_specs=in_specs, out_specs=out_specs, out_shape=out_shape,
        scratch_shapes=[pltpu.VMEM((8, LANES), F32)],
        compiler_params=_cparams(("arbitrary",)),
    )(h, oa, ob, oc, od, gain, wg, wb, wo, fgain, wr, br)


def _row_copy(src, src_row, dst, dst_row, sem):
    return pltpu.make_async_copy(src.at[pl.ds(src_row, 1)], dst.at[pl.ds(dst_row, 1)], sem)


def _dest_kernel(rt_ref, ps_ref, d_ref):
    rt = rt_ref[...]
    lane = _iota(rt.shape, 1)
    lanef = lane.astype(F32)
    ps = ps_ref[...]

    def dest(e_lane, r_lane):
        start = jnp.sum(jnp.where(lanef == rt[:, e_lane:e_lane + 1], ps, 0.0), axis=-1, keepdims=True)
        return start + rt[:, r_lane:r_lane + 1]

    d = jnp.where(lane == 0, dest(RT_E1, RT_R1), jnp.where(lane == 1, dest(RT_E2, RT_R2), 0.0))
    d_ref[...] = d.astype(I32)


def _dest_call(rt, pstart_row):
    T = rt.shape[0]
    tm = TM_MERGE
    return pl.pallas_call(
        _dest_kernel, grid=(T // tm,),
        in_specs=[pl.BlockSpec((tm, LANES), lambda i: (i, 0)), pl.BlockSpec((1, LANES), lambda i: (0, 0))],
        out_specs=pl.BlockSpec((tm, LANES), lambda i: (i, 0)), out_shape=jax.ShapeDtypeStruct((T, LANES), I32),
        compiler_params=_cparams(("arbitrary",)),
    )(rt, pstart_row)


DMA_ISSUE_UNROLL = 8


def _dispatch_kernel(d1_ref, d2_ref, xn_ref, xs_in_ref, xs_ref, sem):
    del xs_in_ref
    base = pl.program_id(0) * TD_DISPATCH

    def copies(r):
        t = base + r
        return (_row_copy(xn_ref, r, xs_ref, d1_ref[t], sem), _row_copy(xn_ref, r, xs_ref, d2_ref[t], sem))

    def start(r, c):
        for cp in copies(r):
            cp.start()
        return c

    def wait(r, c):
        for cp in copies(r):
            cp.wait()
        return c

    lax.fori_loop(0, TD_DISPATCH, start, 0, unroll=DMA_ISSUE_UNROLL)
    lax.fori_loop(0, TD_DISPATCH, wait, 0)


def _dispatch_call(d1, d2, xn, xs_zero):
    T, D = xn.shape
    any_spec = pl.BlockSpec(memory_space=pl.ANY)
    grid_spec = pltpu.PrefetchScalarGridSpec(
        num_scalar_prefetch=2, grid=(T // TD_DISPATCH,),
        in_specs=[pl.BlockSpec((TD_DISPATCH, D), lambda i, *_: (i, 0)), any_spec], out_specs=any_spec,
        scratch_shapes=[pltpu.SemaphoreType.DMA(())])
    return pl.pallas_call(
        _dispatch_kernel, grid_spec=grid_spec, out_shape=jax.ShapeDtypeStruct(xs_zero.shape, xs_zero.dtype),
        input_output_aliases={3: 0},
        compiler_params=pltpu.CompilerParams(dimension_semantics=("arbitrary",), has_side_effects=True,
                                             vmem_limit_bytes=VMEM_LIMIT),
    )(d1, d2, xn, xs_zero)


def _expert_kernel(be_ref, used_ref, x_ref, wg_ref, wu_ref, wd_ref, y_ref):
    i = pl.program_id(0)

    @pl.when(i < used_ref[0])
    def _():
        x = x_ref[...].astype(BF16)
        g = _dot(x, wg_ref[0, 0].astype(BF16))
        u = _dot(x, wu_ref[0, 0].astype(BF16))
        act = g / (1.0 + jnp.exp(-g)) * u
        y_ref[...] = _dot(act.astype(BF16), wd_ref[0, 0].astype(BF16))

    @pl.when(i >= used_ref[0])
    def _():
        y_ref[...] = jnp.zeros_like(y_ref)


def _expert_call(blk_expert, used, xs, wg, wu, wd, layer):
    rows, D = xs.shape
    rb = MOE_ROW_BLOCK
    DE = wg.shape[3]
    xmap = lambda i, be, used: (jnp.minimum(i, used[0] - 1), 0)
    wmap = lambda i, be, used: (layer, be[i], 0, 0)
    grid_spec = pltpu.PrefetchScalarGridSpec(
        num_scalar_prefetch=2, grid=(rows // rb,),
        in_specs=[pl.BlockSpec((rb, D), xmap), pl.BlockSpec((1, 1, D, DE), wmap), pl.BlockSpec((1, 1, D, DE), wmap),
                  pl.BlockSpec((1, 1, DE, D), wmap)],
        out_specs=pl.BlockSpec((rb, D), lambda i, be, used: (i, 0)))
    return pl.pallas_call(
        _expert_kernel, grid_spec=grid_spec, out_shape=jax.ShapeDtypeStruct((rows, D), F32),
        compiler_params=_cparams(("arbitrary",)),
    )(blk_expert, used, xs, wg, wu, wd)


def _combine_kernel(final_norm, d1_ref, d2_ref, h_ref, rt_ref, gain_ref, ys_ref, o_ref, buf_ref, sem):
    base = pl.program_id(0) * TC_COMBINE

    def copies(r):
        t = base + r
        return (_row_copy(ys_ref, d1_ref[t], buf_ref.at[0], r, sem),
                _row_copy(ys_ref, d2_ref[t], buf_ref.at[1], r, sem))

    def start(r, c):
        for cp in copies(r):
            cp.start()
        return c

    def wait(r, c):
        for cp in copies(r):
            cp.wait()
        return c

    lax.fori_loop(0, TC_COMBINE, start, 0, unroll=DMA_ISSUE_UNROLL)
    lax.fori_loop(0, TC_COMBINE, wait, 0)
    rt = rt_ref[...]
    out = h_ref[...] + rt[:, RT_W1:RT_W1 + 1] * buf_ref[0] + rt[:, RT_W2:RT_W2 + 1] * buf_ref[1]
    if final_norm:
        out = out * lax.rsqrt(jnp.mean(out * out, axis=-1, keepdims=True) + NORM_EPS) * gain_ref[...]
    o_ref[...] = out


def _combine_call(d1, d2, h, rt, gain, ys, final_norm):
    T, D = h.shape
    tc = TC_COMBINE
    tile = lambda i, *_: (i, 0)
    grid_spec = pltpu.PrefetchScalarGridSpec(
        num_scalar_prefetch=2, grid=(T // tc,),
        in_specs=[pl.BlockSpec((tc, D), tile), pl.BlockSpec((tc, LANES), tile),
                  pl.BlockSpec((1, D), lambda i, *_: (0, 0)), pl.BlockSpec(memory_space=pl.ANY)],
        out_specs=pl.BlockSpec((tc, D), tile),
        scratch_shapes=[pltpu.VMEM((2, tc, D), F32), pltpu.SemaphoreType.DMA(())])
    return pl.pallas_call(
        functools.partial(_combine_kernel, final_norm), grid_spec=grid_spec,
        out_shape=jax.ShapeDtypeStruct((T, D), F32),
        compiler_params=_cparams(("arbitrary",)),
    )(d1, d2, h, rt, gain, ys)


def _in_proj_columns():
    widths = (('q_a', 256), ('k_cmp', 64), ('v_cmp', 64), ('k_sel', 64), ('v_sel', 64), ('k_win', 64), ('v_win', 64),
              ('g_a', 12), ('q_b', 256), ('k_b', 256), ('v_b', 256), ('f_b', 4), ('q_c', 256), ('ckv_c', 128),
              ('qi_c', 256), ('ki_c', 64), ('wi_c', 4), ('q_d', 256), ('k_d', 128), ('v_d', 128))
    off, cols = 0, {}
    for name, w in widths:
        cols[name] = np.arange(off, off + w)
        off += w
    cat = lambda names: np.concatenate([cols[n] for n in names])
    rope = cat(('q_a', 'q_c', 'qi_c', 'q_d', 'k_d', 'k_cmp', 'k_sel', 'k_win', 'ki_c'))
    plain = cat(('q_b', 'k_b', 'v_b', 'v_d', 'ckv_c', 'v_cmp', 'v_sel', 'v_win'))
    small = cat(('g_a', 'f_b', 'wi_c'))
    assert rope.size == N_ROPE_COLS and plain.size == N_PLAIN_COLS
    return np.concatenate([rope, plain, small]), N_SMALL_COLS - small.size


def _static_tables(S):
    n_cmp_pad = S // NSA_CMP_STRIDE
    n_blk = S // NSA_SEL_BLOCK
    c0 = np.arange(n_cmp_pad) * NSA_CMP_STRIDE
    b0 = np.arange(n_blk) * NSA_SEL_BLOCK
    ov = ((c0[:, None] < b0[None, :] + NSA_SEL_BLOCK) & (c0[:, None] + NSA_CMP_LEN > b0[None, :])).astype(np.float32)
    overlap = np.zeros((n_cmp_pad, LANES), np.float32)
    overlap[:, :n_blk] = ov
    expand = np.zeros((LANES, S), np.float32)
    expand[np.arange(S) // NSA_SEL_BLOCK, np.arange(S)] = 1.0
    strict_upper = (np.arange(LANES)[:, None] < np.arange(LANES)[None, :]).astype(np.float32)
    cntmat = np.concatenate([strict_upper, np.ones((LANES, LANES), np.float32)], axis=1)
    return jnp.asarray(overlap, BF16), jnp.asarray(expand, BF16), jnp.asarray(cntmat, BF16)


def kernel(x, positions, attn_norm, w_in, nsa_pe_k, nsa_w1_k, nsa_w2_k, nsa_pe_v, nsa_w1_v, nsa_w2_v, fox_forget_bias, dsa_kv_norm, dsa_w_ukv, swa_sinks, w_branch, w_gate, w_out, ffn_norm, moe_w_group, moe_b_group, moe_w_expert, moe_b_expert, moe_w_gate, moe_w_up, moe_w_down, final_norm):
    B, S, D = x.shape
    T = B * S
    depth = w_in.shape[0]
    perm, n_pad = _in_proj_columns()
    overlap, expand, cntmat = _static_tables(S)
    half = HEAD_DIM // 2
    inv_freq = ROPE_THETA ** (-jnp.arange(half, dtype=F32) / half)
    invf = jnp.tile(inv_freq, LANES // half).reshape(1, LANES)
    pos = positions.reshape(T, 1).astype(I32)
    small_pad = lambda v, off: jnp.zeros((1, LANES), F32).at[0, off:off + v.shape[0]].set(v.astype(F32))

    n_rows = -(-(T * 2 + MOE_N_EXPERTS * (MOE_ROW_BLOCK - 1)) // MOE_ROW_BLOCK) * MOE_ROW_BLOCK
    h = x.reshape(T, D)
    for l in range(depth):
        w = jnp.pad(w_in[l][:, perm], ((0, 0), (0, n_pad))).astype(BF16)
        (qa, qc, qi, qd, kd, kcmp, ksel, kwin, ki, qb, kb, vb, vd, vcmp, vsel, vwin, kc, vc, sm) = _proj_call(
            h, pos, invf, attn_norm[l].reshape(1, D), w, small_pad(fox_forget_bias[l], SM_F0),
            dsa_kv_norm[l].reshape(1, DSA_KV_RANK), dsa_w_ukv[l].astype(BF16), B, S)

        cw = NSA_CMP_STRIDE * HEAD_DIM
        pe_rows = lambda pe: jnp.broadcast_to(pe.reshape(1, 2 * cw), (8, 2 * cw)).astype(BF16)
        kcc, vcc = _cmp_call(kcmp.reshape(T // NSA_CMP_STRIDE, cw), vcmp.reshape(T // NSA_CMP_STRIDE, cw),
                             pe_rows(nsa_pe_k[l]), nsa_w1_k[l].astype(BF16), nsa_w2_k[l].astype(BF16),
                             pe_rows(nsa_pe_v[l]), nsa_w1_v[l].astype(BF16), nsa_w2_v[l].astype(BF16), B, S)
        o_a = _nsa_call(qa, sm, kcc, vcc, overlap, expand, ksel, vsel, kwin, vwin, B, S)

        key_bias = sm[:, SM_F0:SM_F0 + N_HEADS].reshape(B, S, N_HEADS).transpose(0, 2, 1)
        o_b = _fox_call(qb, key_bias, kb, vb, B, S)
        o_c = _dsa_call(qc, qi, sm, ki, cntmat, kc, vc, B, S)
        o_d = _swa_call(qd, small_pad(swa_sinks[l], 0), kd, vd, B, S)

        wr = jnp.zeros((D, LANES), F32).at[:, :MOE_GROUPS].set(moe_w_group[l]) \
            .at[:, ROUTER_E0:ROUTER_E0 + MOE_N_EXPERTS].set(moe_w_expert[l])
        br = jnp.zeros((1, LANES), F32).at[0, :MOE_GROUPS].set(moe_b_group[l]) \
            .at[0, ROUTER_E0:ROUTER_E0 + MOE_N_EXPERTS].set(moe_b_expert[l])
        h2, xn, rt, cnt = _merge_call(h, o_a, o_b, o_c, o_d, attn_norm[l].reshape(1, D), w_gate[l].astype(BF16),
                                      w_branch[l].astype(BF16), w_out[l].astype(BF16), ffn_norm[l].reshape(1, D),
                                      wr, br)

        counts = cnt[0, ROUTER_E0:ROUTER_E0 + MOE_N_EXPERTS].astype(I32)
        padded = (counts + MOE_ROW_BLOCK - 1) // MOE_ROW_BLOCK * MOE_ROW_BLOCK
        p_end = jnp.cumsum(padded)
        p_start = p_end - padded
        n_blocks = n_rows // MOE_ROW_BLOCK
        blk_first_row = jnp.arange(n_blocks, dtype=I32) * MOE_ROW_BLOCK
        blk_expert = jnp.minimum(jnp.sum((p_end[None, :] <= blk_first_row[:, None]).astype(I32), axis=1),
                                 MOE_N_EXPERTS - 1).astype(I32)
        used = (p_end[-1:] // MOE_ROW_BLOCK).astype(I32)
        dest = _dest_call(rt, small_pad(p_start, 0))
        d1, d2 = dest[:, 0], dest[:, 1]

        xs = _dispatch_call(d1, d2, xn, jnp.zeros((n_rows, D), F32))
        ys = _expert_call(blk_expert, used, xs, moe_w_gate, moe_w_up, moe_w_down, l)
        last = l == depth - 1
        h = _combine_call(d1, d2, h2, rt, final_norm.reshape(1, D), ys, last)
    return h.reshape(B, S, D)
```

```python
import functools
import math

import numpy as np
import jax
import jax.numpy as jnp
from jax import lax
from jax.experimental import pallas as pl
from jax.experimental.pallas import tpu as pltpu

F32 = jnp.float32
BF16 = jnp.bfloat16
I32 = jnp.int32

HEAD_DIM = 64
N_HEADS = 4
MIX_WIDTH = N_HEADS * HEAD_DIM
ROPE_THETA = 10000.0
NORM_EPS = 1e-6
NEG_INF = -1e30
INT_MIN = -2 ** 31
LOG2E = float(np.log2(np.e))

NSA_CMP_LEN = 32
NSA_CMP_STRIDE = 16
NSA_CMP_HIDDEN = 256
NSA_SEL_BLOCK = 64
NSA_N_SEL = 16
NSA_WINDOW = 512
NSA_FORCE_SCORE = 1e4
DSA_KV_RANK = 128
DSA_TOPK_MAX = 256
SWA_WINDOW = 128
SWA_KV_HEADS = 2
MOE_GROUPS = 8
MOE_EXPERTS_PER_GROUP = 8
MOE_N_EXPERTS = 64
MOE_D_EXPERT = 256

LANES = 128
VMEM_LIMIT = 48 * 1024 * 1024

N_ROPE_COLS = 1408
N_PLAIN_COLS = 1216
N_SMALL_COLS = 128
N_PROJ_COLS = N_ROPE_COLS + N_PLAIN_COLS + N_SMALL_COLS
SM_GATE0, SM_F0, SM_W0 = 0, 12, 16

TM_PROJ = 512
TQ = 256
TM_MERGE = 512
MOE_ROW_BLOCK = 256
TD_DISPATCH = 256
TC_COMBINE = 256


def _cparams(sem):
    return pltpu.CompilerParams(dimension_semantics=sem, vmem_limit_bytes=VMEM_LIMIT)


def _dot(a, b):
    return jnp.dot(a, b, preferred_element_type=F32)


def _dot_t(a, b):
    return lax.dot_general(a, b, (((1,), (1,)), ((), ())), preferred_element_type=F32)


def _iota(shape, dim):
    return lax.broadcasted_iota(I32, shape, dim)


def _rope_slab(x, cos, sin_signed, first_half):
    rot = jnp.where(first_half, pltpu.roll(x, 96, 1), pltpu.roll(x, 32, 1))
    return x * cos + rot * sin_signed


def _proj_kernel(h_ref, pos_ref, invf_ref, gain_ref, w_ref, fbias_ref, kvn_ref, wukv_ref,
                 qa_ref, qc_ref, qi_ref, qd_ref, kd_ref, kcmp_ref, ksel_ref, kwin_ref, ki_ref,
                 qb_ref, kb_ref, vb_ref, vd_ref, vcmp_ref, vsel_ref, vwin_ref, kc_ref, vc_ref,
                 sm_ref, carry_ref):
    j = pl.program_id(1)
    tm = h_ref.shape[0]
    h = h_ref[...]
    hn = h * lax.rsqrt(jnp.mean(h * h, axis=-1, keepdims=True) + NORM_EPS) * gain_ref[...]
    hb = hn.astype(BF16)

    ang = pos_ref[...].astype(F32) * invf_ref[...]
    cos = jnp.cos(ang)
    sin = jnp.sin(ang)
    lane = _iota((tm, LANES), 1)
    first_half = (lane % HEAD_DIM) < (HEAD_DIM // 2)
    sin_signed = jnp.where(first_half, -sin, sin)
    rope = lambda x: _rope_slab(x, cos, sin_signed, first_half)

    zr = _dot(hb, w_ref[:, 0:N_ROPE_COLS])
    slab = lambda k: rope(zr[:, k * LANES:(k + 1) * LANES])
    scale = HEAD_DIM ** -0.5
    softmax_scale = scale * LOG2E
    for q_ref, k0, sc in ((qa_ref, 0, softmax_scale), (qc_ref, 2, softmax_scale), (qi_ref, 4, scale),
                          (qd_ref, 6, softmax_scale)):
        q_ref[...] = (jnp.concatenate([slab(k0), slab(k0 + 1)], axis=1) * sc).astype(BF16)
    s8 = slab(8).astype(BF16)
    kd_ref[0] = s8[:, :HEAD_DIM]
    kd_ref[1] = s8[:, HEAD_DIM:]
    s9 = slab(9).astype(BF16)
    kcmp_ref[...] = s9[:, :HEAD_DIM]
    ksel_ref[...] = s9[:, HEAD_DIM:]
    s10 = slab(10).astype(BF16)
    kwin_ref[...] = s10[:, :HEAD_DIM]
    ki_ref[...] = s10[:, HEAD_DIM:]

    zp = _dot(hb, w_ref[:, N_ROPE_COLS:N_ROPE_COLS + N_PLAIN_COLS])
    qb_ref[...] = (zp[:, 0:256] * softmax_scale).astype(BF16)
    for hh in range(N_HEADS):
        kb_ref[hh] = zp[:, 256 + hh * 64:256 + (hh + 1) * 64].astype(BF16)
        vb_ref[hh] = zp[:, 512 + hh * 64:512 + (hh + 1) * 64].astype(BF16)
    vd_ref[0] = zp[:, 768:832].astype(BF16)
    vd_ref[1] = zp[:, 832:896].astype(BF16)
    ckv = zp[:, 896:1024]
    vcmp_ref[...] = zp[:, 1024:1088].astype(BF16)
    vsel_ref[...] = zp[:, 1088:1152].astype(BF16)
    vwin_ref[...] = zp[:, 1152:1216].astype(BF16)

    ckvn = ckv * lax.rsqrt(jnp.mean(ckv * ckv, axis=-1, keepdims=True) + NORM_EPS) * kvn_ref[...]
    kvc = _dot(ckvn.astype(BF16), wukv_ref[...])
    for k in range(2):
        kk = rope(kvc[:, k * LANES:(k + 1) * LANES]).astype(BF16)
        kc_ref[2 * k] = kk[:, :HEAD_DIM]
        kc_ref[2 * k + 1] = kk[:, HEAD_DIM:]
    for hh in range(N_HEADS):
        vc_ref[hh] = kvc[:, 256 + hh * 64:256 + (hh + 1) * 64].astype(BF16)

    zs = _dot(hb, w_ref[:, N_ROPE_COLS + N_PLAIN_COLS:N_PROJ_COLS])
    sig = 1.0 / (1.0 + jnp.exp(-zs))
    xf = zs + fbias_ref[...]
    logf = jnp.minimum(xf, 0.0) - jnp.log(1.0 + jnp.exp(-jnp.abs(xf)))
    is_f = (lane >= SM_F0) & (lane < SM_W0)
    logf = jnp.where(is_f, logf, 0.0)

    @pl.when(j == 0)
    def _():
        carry_ref[...] = jnp.zeros_like(carry_ref)

    tri = (_iota((tm, tm), 0) >= _iota((tm, tm), 1)).astype(F32)
    cum = jnp.dot(tri, logf, preferred_element_type=F32, precision=lax.Precision.HIGHEST)
    cum = cum + carry_ref[0:1, :]
    carry_ref[0:1, :] = cum[tm - 1:tm, :]
    sm_ref[...] = jnp.where(lane < SM_F0, sig,
                            jnp.where(is_f, -LOG2E * cum, jnp.where(lane < SM_W0 + 4, 0.5 * zs, 0.0)))


def _proj_call(h, pos, invf, gain, w, fbias, kvn, wukv, B, S):
    T = B * S
    tm = TM_PROJ
    nj = S // tm
    row = lambda b, j: (b * nj + j, 0)
    hrow = lambda b, j: (0, b * nj + j, 0)
    const2 = lambda b, j: (0, 0)
    tok = lambda w_, dt: jax.ShapeDtypeStruct((T, w_), dt)
    hm = lambda n: jax.ShapeDtypeStruct((n, T, HEAD_DIM), BF16)
    out_shape = [tok(256, BF16)] * 4 + [hm(2)] + [tok(64, BF16)] * 4 + [tok(256, BF16), hm(4), hm(4), hm(2)] \
        + [tok(64, BF16)] * 3 + [hm(4), hm(4), tok(LANES, F32)]
    spec_tok = lambda w_: pl.BlockSpec((tm, w_), row)
    spec_hm = lambda n: pl.BlockSpec((n, tm, HEAD_DIM), hrow)
    out_specs = [spec_tok(256)] * 4 + [spec_hm(2)] + [spec_tok(64)] * 4 + [spec_tok(256), spec_hm(4), spec_hm(4), spec_hm(2)] \
        + [spec_tok(64)] * 3 + [spec_hm(4), spec_hm(4), spec_tok(LANES)]
    in_specs = [
        pl.BlockSpec((tm, h.shape[1]), row),
        pl.BlockSpec((tm, 1), row),
        pl.BlockSpec((1, LANES), const2),
        pl.BlockSpec((1, h.shape[1]), const2),
        pl.BlockSpec(w.shape, const2),
        pl.BlockSpec((1, LANES), const2),
        pl.BlockSpec((1, DSA_KV_RANK), const2),
        pl.BlockSpec(wukv.shape, const2),
    ]
    return pl.pallas_call(
        _proj_kernel, grid=(B, nj), in_specs=in_specs, out_specs=out_specs, out_shape=out_shape,
        scratch_shapes=[pltpu.VMEM((8, LANES), F32)],
        compiler_params=_cparams(("arbitrary", "arbitrary")),
    )(h, pos, invf, gain, w, fbias, kvn, wukv)


def _gelu_tanh(x):
    return 0.5 * x * (1.0 + jnp.tanh(np.sqrt(2.0 / np.pi).astype(np.float32) * (x + 0.044715 * (x * x * x))))


def _cmp_kernel(ck_ref, cv_ref, pek_ref, w1k_ref, w2k_ref, pev_ref, w1v_ref, w2v_ref, kc_ref, vc_ref):
    half = NSA_CMP_STRIDE * HEAD_DIM
    for c_ref, pe_ref, w1_ref, w2_ref, o_ref in ((ck_ref, pek_ref, w1k_ref, w2k_ref, kc_ref),
                                                 (cv_ref, pev_ref, w1v_ref, w2v_ref, vc_ref)):
        c = c_ref[...]
        a = _dot(c, w1_ref[0:half, :])
        b = _dot(c, w1_ref[half:2 * half, :])
        n = b.shape[0]
        pe_h = _dot(pe_ref[...], w1_ref[...])
        hid = a + pltpu.roll(b, n - 1, 0) + pe_h[0:1, :]
        o_ref[...] = _dot(_gelu_tanh(hid).astype(BF16), w2_ref[...]).astype(BF16)


def _cmp_call(ck, cv, pek, w1k, w2k, pev, w1v, w2v, B, S):
    nch = S // NSA_CMP_STRIDE
    blk = lambda b: (b, 0)
    const = lambda b: (0, 0)
    cw = NSA_CMP_STRIDE * HEAD_DIM
    in_specs = [pl.BlockSpec((nch, cw), blk), pl.BlockSpec((nch, cw), blk)]
    for _ in range(2):
        in_specs += [pl.BlockSpec((8, 2 * cw), const), pl.BlockSpec((2 * cw, NSA_CMP_HIDDEN), const),
                     pl.BlockSpec((NSA_CMP_HIDDEN, HEAD_DIM), const)]
    out = jax.ShapeDtypeStruct((B * nch, HEAD_DIM), BF16)
    return pl.pallas_call(
        _cmp_kernel, grid=(B,), in_specs=in_specs,
        out_specs=[pl.BlockSpec((nch, HEAD_DIM), blk)] * 2, out_shape=[out, out],
        compiler_params=_cparams(("arbitrary",)),
    )(ck, cv, pek, w1k, w2k, pev, w1v, w2v)


def _attend(q, segments, extra_logit=None):
    scores = []
    for k, _, bias, allowed in segments:
        s = _dot_t(q, k)
        if bias is not None:
            s = s + bias
        if allowed is not None:
            s = jnp.where(allowed, s, NEG_INF)
        scores.append(s)
    m = functools.reduce(jnp.maximum, [jnp.max(s, axis=-1, keepdims=True) for s in scores])
    if extra_logit is not None:
        m = jnp.maximum(m, extra_logit)
    den = jnp.exp2(extra_logit - m) if extra_logit is not None else 0.0
    o = 0.0
    for s, (_, v, _, _) in zip(scores, segments):
        e = jnp.exp2(s - m)
        den = den + jnp.sum(e, axis=-1, keepdims=True)
        o = o + _dot(e.astype(BF16), v)
    return o / den


def _causal_split(i, tq, nk, S):
    n_full = nk - S // N_CAUSAL_CLASSES
    t_col = i * tq + _iota((tq, 1), 0)
    tail_causal = (n_full + _iota((1, nk - n_full), 1)) <= t_col
    return n_full, tail_causal


N_CAUSAL_CLASSES = 4


def _for_causal_class(i, tq, S, body):
    step = S // N_CAUSAL_CLASSES
    cls = (i * tq + tq - 1) // step
    for c in range(N_CAUSAL_CLASSES):
        pl.when(cls == c)(functools.partial(body, (c + 1) * step))


def _nsa_kernel(q_ref, sm_ref, kc_ref, vc_ref, ov_ref, ex_ref, ksel_ref, vsel_ref, kwin_ref, vwin_ref, o_ref):
    i = pl.program_id(1)
    tq = q_ref.shape[0]
    S = ksel_ref.shape[0]
    ncmp = kc_ref.shape[0]
    q = q_ref[...]
    sm = sm_ref[...]
    t_col = i * tq + _iota((tq, 1), 0)

    c_end = _iota((1, ncmp), 1) * NSA_CMP_STRIDE + (NSA_CMP_LEN - 1)
    allowed_c = c_end <= t_col
    kc = kc_ref[...]
    vc = vc_ref[...]
    o_cmp = []
    imp = jnp.zeros((tq, LANES), F32)
    for hh in range(N_HEADS):
        qh = q[:, hh * 64:(hh + 1) * 64]
        s = jnp.where(allowed_c, _dot_t(qh, kc), NEG_INF)
        m = jnp.max(s, axis=-1, keepdims=True)
        e = jnp.where(allowed_c, jnp.exp2(s - m), 0.0)
        den = jnp.sum(e, axis=-1, keepdims=True)
        p = e * jnp.where(den > 0.0, 1.0 / den, 0.0)
        pb = p.astype(BF16)
        o_cmp.append(_dot(pb, vc))
        imp = imp + _dot(pb, ov_ref[...])

    n_blk = S // NSA_SEL_BLOCK
    jl = _iota((tq, LANES), 1)
    cur = t_col // NSA_SEL_BLOCK
    forced = (jl == 0) | (jl == cur) | (jl == cur - 1)
    imp = jnp.where(forced, NSA_FORCE_SCORE, imp)
    imp = jnp.where(jl > cur, -NSA_FORCE_SCORE, imp)
    rank = jnp.zeros((tq, LANES), F32)
    for jp in range(n_blk):
        col = imp[:, jp:jp + 1]
        ahead = (col > imp) | ((col == imp) & (jp < jl))
        rank = rank + jnp.where(ahead, 1.0, 0.0)
    unselected = jnp.where((rank < float(min(NSA_N_SEL, n_blk))) & (jl < n_blk), 0.0, NEG_INF).astype(BF16)

    nband = NSA_WINDOW + tq
    start = pl.multiple_of(jnp.maximum(i * tq - NSA_WINDOW, 0), math.gcd(tq, NSA_WINDOW))
    kwin = kwin_ref[pl.ds(start, nband), :]
    vwin = vwin_ref[pl.ds(start, nband), :]
    s_band = start + _iota((1, nband), 1)
    allowed_win = (s_band <= t_col) & (s_band > t_col - NSA_WINDOW)
    gate = lambda hh, k: sm[:, SM_GATE0 + 3 * hh + k:SM_GATE0 + 3 * hh + k + 1]
    partial_out = []
    for hh in range(N_HEADS):
        o_win = _attend(q[:, hh * 64:(hh + 1) * 64], [(kwin, vwin, None, allowed_win)])
        partial_out.append(gate(hh, 0) * o_cmp[hh] + gate(hh, 2) * o_win)

    def selected(nk):
        bias = _dot(unselected, ex_ref[:, 0:nk])
        n_full, tail_causal = _causal_split(i, tq, nk, S)
        segments = [(ksel_ref[n_full:nk, :], vsel_ref[n_full:nk, :], bias[:, n_full:nk], tail_causal)]
        if n_full:
            segments.append((ksel_ref[0:n_full, :], vsel_ref[0:n_full, :], bias[:, 0:n_full], None))
        outs = []
        for hh in range(N_HEADS):
            o_sel = _attend(q[:, hh * 64:(hh + 1) * 64], segments)
            outs.append(partial_out[hh] + gate(hh, 1) * o_sel)
        o_ref[...] = jnp.concatenate(outs, axis=1).astype(BF16)

    _for_causal_class(i, tq, S, selected)


def _nsa_call(qa, sm, kcc, vcc, overlap, expand, ksel, vsel, kwin, vwin, B, S):
    T = B * S
    nq = S // TQ
    ncmp = S // NSA_CMP_STRIDE
    tile = lambda b, i: (b * nq + i, 0)
    seq = lambda b, i: (b, 0)
    const = lambda b, i: (0, 0)
    in_specs = [pl.BlockSpec((TQ, 256), tile), pl.BlockSpec((TQ, LANES), tile),
                pl.BlockSpec((ncmp, 64), seq), pl.BlockSpec((ncmp, 64), seq),
                pl.BlockSpec(overlap.shape, const), pl.BlockSpec(expand.shape, const)] \
        + [pl.BlockSpec((S, 64), seq)] * 4
    return pl.pallas_call(
        _nsa_kernel, grid=(B, nq), in_specs=in_specs, out_specs=pl.BlockSpec((TQ, 256), tile),
        out_shape=jax.ShapeDtypeStruct((T, 256), BF16),
        compiler_params=_cparams(("arbitrary", "arbitrary")),
    )(qa, sm, kcc, vcc, overlap, expand, ksel, vsel, kwin, vwin)


def _fox_kernel(q_ref, nbias_ref, k_ref, v_ref, o_ref):
    i = pl.program_id(1)
    tq = q_ref.shape[0]
    S = k_ref.shape[1]
    q = q_ref[...]

    def attend(nk):
        n_full, tail_causal = _causal_split(i, tq, nk, S)
        outs = []
        for hh in range(N_HEADS):
            segments = [(k_ref[hh, n_full:nk, :], v_ref[hh, n_full:nk, :], nbias_ref[0, hh:hh + 1, n_full:nk],
                         tail_causal)]
            if n_full:
                segments.append((k_ref[hh, 0:n_full, :], v_ref[hh, 0:n_full, :], nbias_ref[0, hh:hh + 1, 0:n_full],
                                 None))
            outs.append(_attend(q[:, hh * 64:(hh + 1) * 64], segments))
        o_ref[...] = jnp.concatenate(outs, axis=1).astype(BF16)

    _for_causal_class(i, tq, S, attend)


def _fox_call(qb, nbias, kb, vb, B, S):
    T = B * S
    nq = S // TQ
    tile = lambda b, i: (b * nq + i, 0)
    in_specs = [pl.BlockSpec((TQ, 256), tile),
                pl.BlockSpec((1, N_HEADS, S), lambda b, i: (b, 0, 0)),
                pl.BlockSpec((N_HEADS, S, 64), lambda b, i: (0, b, 0)),
                pl.BlockSpec((N_HEADS, S, 64), lambda b, i: (0, b, 0))]
    return pl.pallas_call(
        _fox_kernel, grid=(B, nq), in_specs=in_specs, out_specs=pl.BlockSpec((TQ, 256), tile),
        out_shape=jax.ShapeDtypeStruct((T, 256), BF16),
        compiler_params=_cparams(("arbitrary", "arbitrary")),
    )(qb, nbias, kb, vb)


def _dsa_kernel(q_ref, qi_ref, sm_ref, ki_ref, cnt_ref, k_ref, v_ref, o_ref, score_ref):
    i = pl.program_id(1)
    tq = q_ref.shape[0]
    S = ki_ref.shape[0]
    sm = sm_ref[...]
    qi = qi_ref[...]
    t_col = i * tq + _iota((tq, 1), 0)
    top_k = min(DSA_TOPK_MAX, S // 4)
    k_eff = jnp.minimum(t_col + 1, top_k).astype(F32)
    q = q_ref[...]

    def attend(nk, bias, allowed):
        outs = [_attend(q[:, hh * 64:(hh + 1) * 64], [(k_ref[hh, 0:nk, :], v_ref[hh, 0:nk, :], bias, allowed)])
                for hh in range(N_HEADS)]
        o_ref[...] = jnp.concatenate(outs, axis=1).astype(BF16)

    def float_of_ordered_bits(u):
        k = u ^ INT_MIN
        return lax.bitcast_convert_type(jnp.where(k >= 0, k, k ^ 0x7FFFFFFF), F32)

    def select_and_attend(nk):
        causal = _iota((1, nk), 1) <= t_col
        score = jnp.zeros((tq, nk), F32)
        for hh in range(N_HEADS):
            lg = _dot_t(qi[:, hh * 64:(hh + 1) * 64], ki_ref[0:nk, :])
            score = score + sm[:, SM_W0 + hh:SM_W0 + hh + 1] * jnp.maximum(lg, 0.0)
        score_ref[:, 0:nk] = jnp.where(causal, score, -jnp.inf)

        def search(it, u):
            cand = u | lax.shift_left(jnp.int32(1), 31 - it)
            thr = float_of_ordered_bits(cand)
            cnt = jnp.sum(jnp.where(score_ref[:, 0:nk] >= thr, 1.0, 0.0), axis=-1, keepdims=True)
            return jnp.where(cnt >= k_eff, cand, u)

        u = lax.fori_loop(0, 32, search, jnp.zeros((tq, 1), I32))
        thr = float_of_ordered_bits(u)
        score = score_ref[:, 0:nk]
        gt = score > thr
        eq = score == thr
        need = k_eff - jnp.sum(jnp.where(gt, 1.0, 0.0), axis=-1, keepdims=True)
        eqb = jnp.where(eq, 1.0, 0.0).astype(BF16)
        run = jnp.zeros((tq, LANES), F32)
        bias_chunks = []
        for c in range(nk // LANES):
            sl = slice(c * LANES, (c + 1) * LANES)
            r = _dot(eqb[:, sl], cnt_ref[...])
            prefix = r[:, :LANES] + run
            run = run + r[:, LANES:]
            keep = gt[:, sl] | (eq[:, sl] & (prefix < need))
            bias_chunks.append(jnp.where(keep, 0.0, NEG_INF))
        attend(nk, jnp.concatenate(bias_chunks, axis=1), None)

    def body(nk):
        if nk * N_CAUSAL_CLASSES == S:
            all_kept = (i + 1) * tq <= top_k
            pl.when(all_kept)(lambda: attend(nk, None, _iota((1, nk), 1) <= t_col))
            pl.when(jnp.logical_not(all_kept))(lambda: select_and_attend(nk))
        else:
            select_and_attend(nk)

    _for_causal_class(i, tq, S, body)


def _dsa_call(qc, qi, sm, ki, cntmat, kc, vc, B, S):
    T = B * S
    nq = S // TQ
    tile = lambda b, i: (b * nq + i, 0)
    in_specs = [pl.BlockSpec((TQ, 256), tile), pl.BlockSpec((TQ, 256), tile), pl.BlockSpec((TQ, LANES), tile),
                pl.BlockSpec((S, 64), lambda b, i: (b, 0)),
                pl.BlockSpec(cntmat.shape, lambda b, i: (0, 0)),
                pl.BlockSpec((N_HEADS, S, 64), lambda b, i: (0, b, 0)),
                pl.BlockSpec((N_HEADS, S, 64), lambda b, i: (0, b, 0))]
    return pl.pallas_call(
        _dsa_kernel, grid=(B, nq), in_specs=in_specs, out_specs=pl.BlockSpec((TQ, 256), tile),
        out_shape=jax.ShapeDtypeStruct((T, 256), BF16),
        scratch_shapes=[pltpu.VMEM((TQ, S), F32)],
        compiler_params=_cparams(("arbitrary", "arbitrary")),
    )(qc, qi, sm, ki, cntmat, kc, vc)


def _swa_kernel(q_ref, sink_ref, k_ref, v_ref, o_ref):
    i = pl.program_id(1)
    tq = q_ref.shape[0]
    q = q_ref[...]
    t_col = i * tq + _iota((tq, 1), 0)
    nband = SWA_WINDOW + tq
    start = pl.multiple_of(jnp.maximum(i * tq - SWA_WINDOW, 0), math.gcd(tq, SWA_WINDOW))
    s_band = start + _iota((1, nband), 1)
    allowed = (s_band <= t_col) & (s_band > t_col - SWA_WINDOW)
    group = N_HEADS // SWA_KV_HEADS
    outs = []
    for hh in range(N_HEADS):
        k = k_ref[hh // group, pl.ds(start, nband), :]
        v = v_ref[hh // group, pl.ds(start, nband), :]
        sink = jnp.zeros((tq, 1), F32) + LOG2E * sink_ref[0:1, hh:hh + 1]
        outs.append(_attend(q[:, hh * 64:(hh + 1) * 64], [(k, v, None, allowed)], extra_logit=sink))
    o_ref[...] = jnp.concatenate(outs, axis=1).astype(BF16)


def _swa_call(qd, sinks, kd, vd, B, S):
    T = B * S
    nq = S // TQ
    tile = lambda b, i: (b * nq + i, 0)
    in_specs = [pl.BlockSpec((TQ, 256), tile), pl.BlockSpec((1, LANES), lambda b, i: (0, 0)),
                pl.BlockSpec((SWA_KV_HEADS, S, 64), lambda b, i: (0, b, 0)),
                pl.BlockSpec((SWA_KV_HEADS, S, 64), lambda b, i: (0, b, 0))]
    return pl.pallas_call(
        _swa_kernel, grid=(B, nq), in_specs=in_specs, out_specs=pl.BlockSpec((TQ, 256), tile),
        out_shape=jax.ShapeDtypeStruct((T, 256), BF16),
        compiler_params=_cparams(("arbitrary", "arbitrary")),
    )(qd, sinks, kd, vd)


RT_E1, RT_E2, RT_W1, RT_W2, RT_R1, RT_R2 = 0, 1, 2, 3, 4, 5
ROUTER_E0 = MOE_GROUPS


def _merge_kernel(h_ref, oa_ref, ob_ref, oc_ref, od_ref, gain_ref, wg_ref, wb_ref, wo_ref,
                  fgain_ref, wr_ref, br_ref, h2_ref, xn_ref, rt_ref, cnt_ref, base_ref):
    step = pl.program_id(0)
    tm = h_ref.shape[0]
    h = h_ref[...]
    hn = h * lax.rsqrt(jnp.mean(h * h, axis=-1, keepdims=True) + NORM_EPS) * gain_ref[...]
    hb = hn.astype(BF16)
    merged = jnp.zeros_like(h)
    for i, o_ref in enumerate((oa_ref, ob_ref, oc_ref, od_ref)):
        gate = 1.0 / (1.0 + jnp.exp(-_dot(hb, wg_ref[i])))
        merged = merged + gate * _dot(o_ref[...], wb_ref[i])
    h2 = h + _dot(merged.astype(BF16), wo_ref[...])
    h2_ref[...] = h2

    xn = h2 * lax.rsqrt(jnp.mean(h2 * h2, axis=-1, keepdims=True) + NORM_EPS) * fgain_ref[...]
    xn_ref[...] = xn
    logits = jnp.dot(xn, wr_ref[...], preferred_element_type=F32, precision=lax.Precision.HIGHEST) + br_ref[...]

    lane = _iota((tm, LANES), 1)
    lanef = lane.astype(F32)
    big = float(LANES)
    is_g = lane < MOE_GROUPS
    gl = jnp.where(is_g, logits, NEG_INF)
    gmax = jnp.max(gl, axis=-1, keepdims=True)
    grp = jnp.min(jnp.where(is_g & (gl == gmax), lanef, big), axis=-1, keepdims=True)
    p_grp = 1.0 / jnp.sum(jnp.where(is_g, jnp.exp(gl - gmax), 0.0), axis=-1, keepdims=True)
    lo = ROUTER_E0 + grp * MOE_EXPERTS_PER_GROUP
    in_grp = (lanef >= lo) & (lanef < lo + MOE_EXPERTS_PER_GROUP)
    el = jnp.where(in_grp, logits, NEG_INF)
    v1 = jnp.max(el, axis=-1, keepdims=True)
    l1 = jnp.min(jnp.where(in_grp & (el == v1), lanef, big), axis=-1, keepdims=True)
    el2 = jnp.where(lanef == l1, NEG_INF, el)
    v2 = jnp.max(el2, axis=-1, keepdims=True)
    l2 = jnp.min(jnp.where(in_grp & (lanef != l1) & (el2 == v2), lanef, big), axis=-1, keepdims=True)
    e21 = jnp.exp(v2 - v1)
    w1 = p_grp / (1.0 + e21)
    w2 = p_grp * e21 / (1.0 + e21)

    oh1 = jnp.where(lanef == l1, 1.0, 0.0)
    oh2 = jnp.where(lanef == l2, 1.0, 0.0)

    @pl.when(step == 0)
    def _():
        base_ref[...] = jnp.zeros_like(base_ref)

    both = oh1 + oh2
    strict = (_iota((tm, tm), 0) > _iota((tm, tm), 1)).astype(BF16)
    before = _dot(strict, both.astype(BF16)) + base_ref[0:1, :]
    r1 = jnp.sum(oh1 * before, axis=-1, keepdims=True)
    r2 = jnp.sum(oh2 * before, axis=-1, keepdims=True)
    total = base_ref[0:1, :] + jnp.sum(both, axis=0, keepdims=True)
    base_ref[0:1, :] = total
    cnt_ref[...] = jnp.broadcast_to(total, cnt_ref.shape)

    rt = jnp.zeros((tm, LANES), F32)
    for ln, val in ((RT_E1, l1 - ROUTER_E0), (RT_E2, l2 - ROUTER_E0), (RT_W1, w1), (RT_W2, w2), (RT_R1, r1), (RT_R2, r2)):
        rt = jnp.where(lane == ln, val, rt)
    rt_ref[...] = rt


def _merge_call(h, oa, ob, oc, od, gain, wg, wb, wo, fgain, wr, br):
    T, D = h.shape
    tm = TM_MERGE
    tile = lambda i: (i, 0)
    c2 = lambda i: (0, 0)
    c3 = lambda i: (0, 0, 0)
    once = pl.Buffered(1)
    in_specs = [pl.BlockSpec((tm, D), tile)] + [pl.BlockSpec((tm, 256), tile)] * 4 + [
        pl.BlockSpec((1, D), c2),
        pl.BlockSpec(wg.shape, c3, pipeline_mode=once),
        pl.BlockSpec(wb.shape, c3, pipeline_mode=once),
        pl.BlockSpec(wo.shape, c2, pipeline_mode=once),
        pl.BlockSpec((1, D), c2),
        pl.BlockSpec(wr.shape, c2),
        pl.BlockSpec((1, LANES), c2),
    ]
    out_shape = [jax.ShapeDtypeStruct((T, D), F32), jax.ShapeDtypeStruct((T, D), F32),
                 jax.ShapeDtypeStruct((T, LANES), F32), jax.ShapeDtypeStruct((8, LANES), F32)]
    out_specs = [pl.BlockSpec((tm, D), tile), pl.BlockSpec((tm, D), tile), pl.BlockSpec((tm, LANES), tile),
                 pl.BlockSpec((8, LANES), c2)]
    return pl.pallas_call(
        _merge_kernel, grid=(T // tm,), in_specs=in_specs, out_specs=out_specs, out_shape=out_shape,
        scratch_shapes=[pltpu.VMEM((8, LANES), F32)],
        compiler_params=_cparams(("arbitrary",)),
    )(h, oa, ob, oc, od, gain, wg, wb, wo, fgain, wr, br)


def _row_copy(src, src_row, dst, dst_row, sem):
    return pltpu.make_async_copy(src.at[pl.ds(src_row, 1)], dst.at[pl.ds(dst_row, 1)], sem)


def _dest_kernel(rt_ref, ps_ref, d_ref):
    rt = rt_ref[...]
    lane = _iota(rt.shape, 1)
    lanef = lane.astype(F32)
    ps = ps_ref[...]

    def dest(e_lane, r_lane):
        start = jnp.sum(jnp.where(lanef == rt[:, e_lane:e_lane + 1], ps, 0.0), axis=-1, keepdims=True)
        return start + rt[:, r_lane:r_lane + 1]

    d = jnp.where(lane == 0, dest(RT_E1, RT_R1), jnp.where(lane == 1, dest(RT_E2, RT_R2), 0.0))
    d_ref[...] = d.astype(I32)


def _dest_call(rt, pstart_row):
    T = rt.shape[0]
    tm = TM_MERGE
    return pl.pallas_call(
        _dest_kernel, grid=(T // tm,),
        in_specs=[pl.BlockSpec((tm, LANES), lambda i: (i, 0)), pl.BlockSpec((1, LANES), lambda i: (0, 0))],
        out_specs=pl.BlockSpec((tm, LANES), lambda i: (i, 0)), out_shape=jax.ShapeDtypeStruct((T, LANES), I32),
        compiler_params=_cparams(("arbitrary",)),
    )(rt, pstart_row)


DMA_ISSUE_UNROLL = 8


def _dispatch_kernel(d1_ref, d2_ref, xn_ref, xs_in_ref, xs_ref, sem):
    del xs_in_ref
    base = pl.program_id(0) * TD_DISPATCH

    def copies(r):
        t = base + r
        return (_row_copy(xn_ref, r, xs_ref, d1_ref[t], sem), _row_copy(xn_ref, r, xs_ref, d2_ref[t], sem))

    def start(r, c):
        for cp in copies(r):
            cp.start()
        return c

    def wait(r, c):
        for cp in copies(r):
            cp.wait()
        return c

    lax.fori_loop(0, TD_DISPATCH, start, 0, unroll=DMA_ISSUE_UNROLL)
    lax.fori_loop(0, TD_DISPATCH, wait, 0)


def _dispatch_call(d1, d2, xn, xs_zero):
    T, D = xn.shape
    any_spec = pl.BlockSpec(memory_space=pl.ANY)
    grid_spec = pltpu.PrefetchScalarGridSpec(
        num_scalar_prefetch=2, grid=(T // TD_DISPATCH,),
        in_specs=[pl.BlockSpec((TD_DISPATCH, D), lambda i, *_: (i, 0)), any_spec], out_specs=any_spec,
        scratch_shapes=[pltpu.SemaphoreType.DMA(())])
    return pl.pallas_call(
        _dispatch_kernel, grid_spec=grid_spec, out_shape=jax.ShapeDtypeStruct(xs_zero.shape, xs_zero.dtype),
        input_output_aliases={3: 0},
        compiler_params=pltpu.CompilerParams(dimension_semantics=("arbitrary",), has_side_effects=True,
                                             vmem_limit_bytes=VMEM_LIMIT),
    )(d1, d2, xn, xs_zero)


def _expert_kernel(be_ref, used_ref, x_ref, wg_ref, wu_ref, wd_ref, y_ref):
    i = pl.program_id(0)

    @pl.when(i < used_ref[0])
    def _():
        x = x_ref[...].astype(BF16)
        g = _dot(x, wg_ref[0, 0].astype(BF16))
        u = _dot(x, wu_ref[0, 0].astype(BF16))
        act = g / (1.0 + jnp.exp(-g)) * u
        y_ref[...] = _dot(act.astype(BF16), wd_ref[0, 0].astype(BF16))

    @pl.when(i >= used_ref[0])
    def _():
        y_ref[...] = jnp.zeros_like(y_ref)


def _expert_call(blk_expert, used, xs, wg, wu, wd, layer):
    rows, D = xs.shape
    rb = MOE_ROW_BLOCK
    DE = wg.shape[3]
    xmap = lambda i, be, used: (jnp.minimum(i, used[0] - 1), 0)
    wmap = lambda i, be, used: (layer, be[i], 0, 0)
    grid_spec = pltpu.PrefetchScalarGridSpec(
        num_scalar_prefetch=2, grid=(rows // rb,),
        in_specs=[pl.BlockSpec((rb, D), xmap), pl.BlockSpec((1, 1, D, DE), wmap), pl.BlockSpec((1, 1, D, DE), wmap),
                  pl.BlockSpec((1, 1, DE, D), wmap)],
        out_specs=pl.BlockSpec((rb, D), lambda i, be, used: (i, 0)))
    return pl.pallas_call(
        _expert_kernel, grid_spec=grid_spec, out_shape=jax.ShapeDtypeStruct((rows, D), F32),
        compiler_params=_cparams(("arbitrary",)),
    )(blk_expert, used, xs, wg, wu, wd)


def _combine_kernel(final_norm, d1_ref, d2_ref, h_ref, rt_ref, gain_ref, ys_ref, o_ref, buf_ref, sem):
    base = pl.program_id(0) * TC_COMBINE

    def copies(r):
        t = base + r
        return (_row_copy(ys_ref, d1_ref[t], buf_ref.at[0], r, sem),
                _row_copy(ys_ref, d2_ref[t], buf_ref.at[1], r, sem))

    def start(r, c):
        for cp in copies(r):
            cp.start()
        return c

    def wait(r, c):
        for cp in copies(r):
            cp.wait()
        return c

    lax.fori_loop(0, TC_COMBINE, start, 0, unroll=DMA_ISSUE_UNROLL)
    lax.fori_loop(0, TC_COMBINE, wait, 0)
    rt = rt_ref[...]
    out = h_ref[...] + rt[:, RT_W1:RT_W1 + 1] * buf_ref[0] + rt[:, RT_W2:RT_W2 + 1] * buf_ref[1]
    if final_norm:
        out = out * lax.rsqrt(jnp.mean(out * out, axis=-1, keepdims=True) + NORM_EPS) * gain_ref[...]
    o_ref[...] = out


def _combine_call(d1, d2, h, rt, gain, ys, final_norm):
    T, D = h.shape
    tc = TC_COMBINE
    tile = lambda i, *_: (i, 0)
    grid_spec = pltpu.PrefetchScalarGridSpec(
        num_scalar_prefetch=2, grid=(T // tc,),
        in_specs=[pl.BlockSpec((tc, D), tile), pl.BlockSpec((tc, LANES), tile),
                  pl.BlockSpec((1, D), lambda i, *_: (0, 0)), pl.BlockSpec(memory_space=pl.ANY)],
        out_specs=pl.BlockSpec((tc, D), tile),
        scratch_shapes=[pltpu.VMEM((2, tc, D), F32), pltpu.SemaphoreType.DMA(())])
    return pl.pallas_call(
        functools.partial(_combine_kernel, final_norm), grid_spec=grid_spec,
        out_shape=jax.ShapeDtypeStruct((T, D), F32),
        compiler_params=_cparams(("arbitrary",)),
    )(d1, d2, h, rt, gain, ys)


def _in_proj_columns():
    widths = (('q_a', 256), ('k_cmp', 64), ('v_cmp', 64), ('k_sel', 64), ('v_sel', 64), ('k_win', 64), ('v_win', 64),
              ('g_a', 12), ('q_b', 256), ('k_b', 256), ('v_b', 256), ('f_b', 4), ('q_c', 256), ('ckv_c', 128),
              ('qi_c', 256), ('ki_c', 64), ('wi_c', 4), ('q_d', 256), ('k_d', 128), ('v_d', 128))
    off, cols = 0, {}
    for name, w in widths:
        cols[name] = np.arange(off, off + w)
        off += w
    cat = lambda names: np.concatenate([cols[n] for n in names])
    rope = cat(('q_a', 'q_c', 'qi_c', 'q_d', 'k_d', 'k_cmp', 'k_sel', 'k_win', 'ki_c'))
    plain = cat(('q_b', 'k_b', 'v_b', 'v_d', 'ckv_c', 'v_cmp', 'v_sel', 'v_win'))
    small = cat(('g_a', 'f_b', 'wi_c'))
    assert rope.size == N_ROPE_COLS and plain.size == N_PLAIN_COLS
    return np.concatenate([rope, plain, small]), N_SMALL_COLS - small.size


def _static_tables(S):
    n_cmp_pad = S // NSA_CMP_STRIDE
    n_blk = S // NSA_SEL_BLOCK
    c0 = np.arange(n_cmp_pad) * NSA_CMP_STRIDE
    b0 = np.arange(n_blk) * NSA_SEL_BLOCK
    ov = ((c0[:, None] < b0[None, :] + NSA_SEL_BLOCK) & (c0[:, None] + NSA_CMP_LEN > b0[None, :])).astype(np.float32)
    overlap = np.zeros((n_cmp_pad, LANES), np.float32)
    overlap[:, :n_blk] = ov
    expand = np.zeros((LANES, S), np.float32)
    expand[np.arange(S) // NSA_SEL_BLOCK, np.arange(S)] = 1.0
    strict_upper = (np.arange(LANES)[:, None] < np.arange(LANES)[None, :]).astype(np.float32)
    cntmat = np.concatenate([strict_upper, np.ones((LANES, LANES), np.float32)], axis=1)
    return jnp.asarray(overlap, BF16), jnp.asarray(expand, BF16), jnp.asarray(cntmat, BF16)


def kernel(x, positions, attn_norm, w_in, nsa_pe_k, nsa_w1_k, nsa_w2_k, nsa_pe_v, nsa_w1_v, nsa_w2_v, fox_forget_bias, dsa_kv_norm, dsa_w_ukv, swa_sinks, w_branch, w_gate, w_out, ffn_norm, moe_w_group, moe_b_group, moe_w_expert, moe_b_expert, moe_w_gate, moe_w_up, moe_w_down, final_norm):
    B, S, D = x.shape
    T = B * S
    depth = w_in.shape[0]
    perm, n_pad = _in_proj_columns()
    overlap, expand, cntmat = _static_tables(S)
    half = HEAD_DIM // 2
    inv_freq = ROPE_THETA ** (-jnp.arange(half, dtype=F32) / half)
    invf = jnp.tile(inv_freq, LANES // half).reshape(1, LANES)
    pos = positions.reshape(T, 1).astype(I32)
    small_pad = lambda v, off: jnp.zeros((1, LANES), F32).at[0, off:off + v.shape[0]].set(v.astype(F32))

    n_rows = -(-(T * 2 + MOE_N_EXPERTS * (MOE_ROW_BLOCK - 1)) // MOE_ROW_BLOCK) * MOE_ROW_BLOCK
    h = x.reshape(T, D)
    for l in range(depth):
        w = jnp.pad(w_in[l][:, perm], ((0, 0), (0, n_pad))).astype(BF16)
        (qa, qc, qi, qd, kd, kcmp, ksel, kwin, ki, qb, kb, vb, vd, vcmp, vsel, vwin, kc, vc, sm) = _proj_call(
            h, pos, invf, attn_norm[l].reshape(1, D), w, small_pad(fox_forget_bias[l], SM_F0),
            dsa_kv_norm[l].reshape(1, DSA_KV_RANK), dsa_w_ukv[l].astype(BF16), B, S)

        cw = NSA_CMP_STRIDE * HEAD_DIM
        pe_rows = lambda pe: jnp.broadcast_to(pe.reshape(1, 2 * cw), (8, 2 * cw)).astype(BF16)
        kcc, vcc = _cmp_call(kcmp.reshape(T // NSA_CMP_STRIDE, cw), vcmp.reshape(T // NSA_CMP_STRIDE, cw),
                             pe_rows(nsa_pe_k[l]), nsa_w1_k[l].astype(BF16), nsa_w2_k[l].astype(BF16),
                             pe_rows(nsa_pe_v[l]), nsa_w1_v[l].astype(BF16), nsa_w2_v[l].astype(BF16), B, S)
        o_a = _nsa_call(qa, sm, kcc, vcc, overlap, expand, ksel, vsel, kwin, vwin, B, S)

        key_bias = sm[:, SM_F0:SM_F0 + N_HEADS].reshape(B, S, N_HEADS).transpose(0, 2, 1)
        o_b = _fox_call(qb, key_bias, kb, vb, B, S)
        o_c = _dsa_call(qc, qi, sm, ki, cntmat, kc, vc, B, S)
        o_d = _swa_call(qd, small_pad(swa_sinks[l], 0), kd, vd, B, S)

        wr = jnp.zeros((D, LANES), F32).at[:, :MOE_GROUPS].set(moe_w_group[l]) \
            .at[:, ROUTER_E0:ROUTER_E0 + MOE_N_EXPERTS].set(moe_w_expert[l])
        br = jnp.zeros((1, LANES), F32).at[0, :MOE_GROUPS].set(moe_b_group[l]) \
            .at[0, ROUTER_E0:ROUTER_E0 + MOE_N_EXPERTS].set(moe_b_expert[l])
        h2, xn, rt, cnt = _merge_call(h, o_a, o_b, o_c, o_d, attn_norm[l].reshape(1, D), w_gate[l].astype(BF16),
                                      w_branch[l].astype(BF16), w_out[l].astype(BF16), ffn_norm[l].reshape(1, D),
                                      wr, br)

        counts = cnt[0, ROUTER_E0:ROUTER_E0 + MOE_N_EXPERTS].astype(I32)
        padded = (counts + MOE_ROW_BLOCK - 1) // MOE_ROW_BLOCK * MOE_ROW_BLOCK
        p_end = jnp.cumsum(padded)
        p_start = p_end - padded
        n_blocks = n_rows // MOE_ROW_BLOCK
        blk_first_row = jnp.arange(n_blocks, dtype=I32) * MOE_ROW_BLOCK
        blk_expert = jnp.minimum(jnp.sum((p_end[None, :] <= blk_first_row[:, None]).astype(I32), axis=1),
                                 MOE_N_EXPERTS - 1).astype(I32)
        used = (p_end[-1:] // MOE_ROW_BLOCK).astype(I32)
        dest = _dest_call(rt, small_pad(p_start, 0))
        d1, d2 = dest[:, 0], dest[:, 1]

        xs = _dispatch_call(d1, d2, xn, jnp.zeros((n_rows, D), F32))
        ys = _expert_call(blk_expert, used, xs, moe_w_gate, moe_w_up, moe_w_down, l)
        last = l == depth - 1
        h = _combine_call(d1, d2, h2, rt, final_norm.reshape(1, D), ys, last)
    return h.reshape(B, S, D)
```

```python
import functools
import math

import numpy as np
import jax
import jax.numpy as jnp
from jax import lax
from jax.experimental import pallas as pl
from jax.experimental.pallas import tpu as pltpu

F32 = jnp.float32
BF16 = jnp.bfloat16
I32 = jnp.int32

HEAD_DIM = 64
N_HEADS = 4
MIX_WIDTH = N_HEADS * HEAD_DIM
ROPE_THETA = 10000.0
NORM_EPS = 1e-6
NEG_INF = -1e30
INT_MIN = -2 ** 31
LOG2E = float(np.log2(np.e))

NSA_CMP_LEN = 32
NSA_CMP_STRIDE = 16
NSA_CMP_HIDDEN = 256
NSA_SEL_BLOCK = 64
NSA_N_SEL = 16
NSA_WINDOW = 512
NSA_FORCE_SCORE = 1e4
DSA_KV_RANK = 128
DSA_TOPK_MAX = 256
SWA_WINDOW = 128
SWA_KV_HEADS = 2
MOE_GROUPS = 8
MOE_EXPERTS_PER_GROUP = 8
MOE_N_EXPERTS = 64
MOE_D_EXPERT = 256

LANES = 128
VMEM_LIMIT = 48 * 1024 * 1024

N_ROPE_COLS = 1408
N_PLAIN_COLS = 1216
N_SMALL_COLS = 128
N_PROJ_COLS = N_ROPE_COLS + N_PLAIN_COLS + N_SMALL_COLS
SM_GATE0, SM_F0, SM_W0 = 0, 12, 16

TM_PROJ = 512
TQ = 256
TM_MERGE = 512
MOE_ROW_BLOCK = 256
TD_DISPATCH = 256
TC_COMBINE = 256


def _cparams(sem):
    return pltpu.CompilerParams(dimension_semantics=sem, vmem_limit_bytes=VMEM_LIMIT)


def _dot(a, b):
    return jnp.dot(a, b, preferred_element_type=F32)


def _dot_t(a, b):
    return lax.dot_general(a, b, (((1,), (1,)), ((), ())), preferred_element_type=F32)


def _iota(shape, dim):
    return lax.broadcasted_iota(I32, shape, dim)


def _rope_slab(x, cos, sin_signed, first_half):
    rot = jnp.where(first_half, pltpu.roll(x, 96, 1), pltpu.roll(x, 32, 1))
    return x * cos + rot * sin_signed


def _proj_kernel(h_ref, pos_ref, invf_ref, gain_ref, w_ref, fbias_ref, kvn_ref, wukv_ref,
                 qa_ref, qc_ref, qi_ref, qd_ref, kd_ref, kcmp_ref, ksel_ref, kwin_ref, ki_ref,
                 qb_ref, kb_ref, vb_ref, vd_ref, vcmp_ref, vsel_ref, vwin_ref, kc_ref, vc_ref,
                 sm_ref, carry_ref):
    j = pl.program_id(1)
    tm = h_ref.shape[0]
    h = h_ref[...]
    hn = h * lax.rsqrt(jnp.mean(h * h, axis=-1, keepdims=True) + NORM_EPS) * gain_ref[...]
    hb = hn.astype(BF16)

    ang = pos_ref[...].astype(F32) * invf_ref[...]
    cos = jnp.cos(ang)
    sin = jnp.sin(ang)
    lane = _iota((tm, LANES), 1)
    first_half = (lane % HEAD_DIM) < (HEAD_DIM // 2)
    sin_signed = jnp.where(first_half, -sin, sin)
    rope = lambda x: _rope_slab(x, cos, sin_signed, first_half)

    zr = _dot(hb, w_ref[:, 0:N_ROPE_COLS])
    slab = lambda k: rope(zr[:, k * LANES:(k + 1) * LANES])
    scale = HEAD_DIM ** -0.5
    softmax_scale = scale * LOG2E
    for q_ref, k0, sc in ((qa_ref, 0, softmax_scale), (qc_ref, 2, softmax_scale), (qi_ref, 4, scale),
                          (qd_ref, 6, softmax_scale)):
        q_ref[...] = (jnp.concatenate([slab(k0), slab(k0 + 1)], axis=1) * sc).astype(BF16)
    s8 = slab(8).astype(BF16)
    kd_ref[0] = s8[:, :HEAD_DIM]
    kd_ref[1] = s8[:, HEAD_DIM:]
    s9 = slab(9).astype(BF16)
    kcmp_ref[...] = s9[:, :HEAD_DIM]
    ksel_ref[...] = s9[:, HEAD_DIM:]
    s10 = slab(10).astype(BF16)
    kwin_ref[...] = s10[:, :HEAD_DIM]
    ki_ref[...] = s10[:, HEAD_DIM:]

    zp = _dot(hb, w_ref[:, N_ROPE_COLS:N_ROPE_COLS + N_PLAIN_COLS])
    qb_ref[...] = (zp[:, 0:256] * softmax_scale).astype(BF16)
    for hh in range(N_HEADS):
        kb_ref[hh] = zp[:, 256 + hh * 64:256 + (hh + 1) * 64].astype(BF16)
        vb_ref[hh] = zp[:, 512 + hh * 64:512 + (hh + 1) * 64].astype(BF16)
    vd_ref[0] = zp[:, 768:832].astype(BF16)
    vd_ref[1] = zp[:, 832:896].astype(BF16)
    ckv = zp[:, 896:1024]
    vcmp_ref[...] = zp[:, 1024:1088].astype(BF16)
    vsel_ref[...] = zp[:, 1088:1152].astype(BF16)
    vwin_ref[...] = zp[:, 1152:1216].astype(BF16)

    ckvn = ckv * lax.rsqrt(jnp.mean(ckv * ckv, axis=-1, keepdims=True) + NORM_EPS) * kvn_ref[...]
    kvc = _dot(ckvn.astype(BF16), wukv_ref[...])
    for k in range(2):
        kk = rope(kvc[:, k * LANES:(k + 1) * LANES]).astype(BF16)
        kc_ref[2 * k] = kk[:, :HEAD_DIM]
        kc_ref[2 * k + 1] = kk[:, HEAD_DIM:]
    for hh in range(N_HEADS):
        vc_ref[hh] = kvc[:, 256 + hh * 64:256 + (hh + 1) * 64].astype(BF16)

    zs = _dot(hb, w_ref[:, N_ROPE_COLS + N_PLAIN_COLS:N_PROJ_COLS])
    sig = 1.0 / (1.0 + jnp.exp(-zs))
    xf = zs + fbias_ref[...]
    logf = jnp.minimum(xf, 0.0) - jnp.log(1.0 + jnp.exp(-jnp.abs(xf)))
    is_f = (lane >= SM_F0) & (lane < SM_W0)
    logf = jnp.where(is_f, logf, 0.0)

    @pl.when(j == 0)
    def _():
        carry_ref[...] = jnp.zeros_like(carry_ref)

    tri = (_iota((tm, tm), 0) >= _iota((tm, tm), 1)).astype(F32)
    cum = jnp.dot(tri, logf, preferred_element_type=F32, precision=lax.Precision.HIGHEST)
    cum = cum + carry_ref[0:1, :]
    carry_ref[0:1, :] = cum[tm - 1:tm, :]
    sm_ref[...] = jnp.where(lane < SM_F0, sig,
                            jnp.where(is_f, -LOG2E * cum, jnp.where(lane < SM_W0 + 4, 0.5 * zs, 0.0)))


def _proj_call(h, pos, invf, gain, w, fbias, kvn, wukv, B, S):
    T = B * S
    tm = TM_PROJ
    nj = S // tm
    row = lambda b, j: (b * nj + j, 0)
    hrow = lambda b, j: (0, b * nj + j, 0)
    const2 = lambda b, j: (0, 0)
    tok = lambda w_, dt: jax.ShapeDtypeStruct((T, w_), dt)
    hm = lambda n: jax.ShapeDtypeStruct((n, T, HEAD_DIM), BF16)
    out_shape = [tok(256, BF16)] * 4 + [hm(2)] + [tok(64, BF16)] * 4 + [tok(256, BF16), hm(4), hm(4), hm(2)] \
        + [tok(64, BF16)] * 3 + [hm(4), hm(4), tok(LANES, F32)]
    spec_tok = lambda w_: pl.BlockSpec((tm, w_), row)
    spec_hm = lambda n: pl.BlockSpec((n, tm, HEAD_DIM), hrow)
    out_specs = [spec_tok(256)] * 4 + [spec_hm(2)] + [spec_tok(64)] * 4 + [spec_tok(256), spec_hm(4), spec_hm(4), spec_hm(2)] \
        + [spec_tok(64)] * 3 + [spec_hm(4), spec_hm(4), spec_tok(LANES)]
    in_specs = [
        pl.BlockSpec((tm, h.shape[1]), row),
        pl.BlockSpec((tm, 1), row),
        pl.BlockSpec((1, LANES), const2),
        pl.BlockSpec((1, h.shape[1]), const2),
        pl.BlockSpec(w.shape, const2),
        pl.BlockSpec((1, LANES), const2),
        pl.BlockSpec((1, DSA_KV_RANK), const2),
        pl.BlockSpec(wukv.shape, const2),
    ]
    return pl.pallas_call(
        _proj_kernel, grid=(B, nj), in_specs=in_specs, out_specs=out_specs, out_shape=out_shape,
        scratch_shapes=[pltpu.VMEM((8, LANES), F32)],
        compiler_params=_cparams(("arbitrary", "arbitrary")),
    )(h, pos, invf, gain, w, fbias, kvn, wukv)


def _gelu_tanh(x):
    return 0.5 * x * (1.0 + jnp.tanh(np.sqrt(2.0 / np.pi).astype(np.float32) * (x + 0.044715 * (x * x * x))))


def _cmp_kernel(ck_ref, cv_ref, pek_ref, w1k_ref, w2k_ref, pev_ref, w1v_ref, w2v_ref, kc_ref, vc_ref):
    half = NSA_CMP_STRIDE * HEAD_DIM
    for c_ref, pe_ref, w1_ref, w2_ref, o_ref in ((ck_ref, pek_ref, w1k_ref, w2k_ref, kc_ref),
                                                 (cv_ref, pev_ref, w1v_ref, w2v_ref, vc_ref)):
        c = c_ref[...]
        a = _dot(c, w1_ref[0:half, :])
        b = _dot(c, w1_ref[half:2 * half, :])
        n = b.shape[0]
        pe_h = _dot(pe_ref[...], w1_ref[...])
        hid = a + pltpu.roll(b, n - 1, 0) + pe_h[0:1, :]
        o_ref[...] = _dot(_gelu_tanh(hid).astype(BF16), w2_ref[...]).astype(BF16)


def _cmp_call(ck, cv, pek, w1k, w2k, pev, w1v, w2v, B, S):
    nch = S // NSA_CMP_STRIDE
    blk = lambda b: (b, 0)
    const = lambda b: (0, 0)
    cw = NSA_CMP_STRIDE * HEAD_DIM
    in_specs = [pl.BlockSpec((nch, cw), blk), pl.BlockSpec((nch, cw), blk)]
    for _ in range(2):
        in_specs += [pl.BlockSpec((8, 2 * cw), const), pl.BlockSpec((2 * cw, NSA_CMP_HIDDEN), const),
                     pl.BlockSpec((NSA_CMP_HIDDEN, HEAD_DIM), const)]
    out = jax.ShapeDtypeStruct((B * nch, HEAD_DIM), BF16)
    return pl.pallas_call(
        _cmp_kernel, grid=(B,), in_specs=in_specs,
        out_specs=[pl.BlockSpec((nch, HEAD_DIM), blk)] * 2, out_shape=[out, out],
        compiler_params=_cparams(("arbitrary",)),
    )(ck, cv, pek, w1k, w2k, pev, w1v, w2v)


def _attend(q, segments, extra_logit=None):
    scores = []
    for k, _, bias, allowed in segments:
        s = _dot_t(q, k)
        if bias is not None:
            s = s + bias
        if allowed is not None:
            s = jnp.where(allowed, s, NEG_INF)
        scores.append(s)
    m = functools.reduce(jnp.maximum, [jnp.max(s, axis=-1, keepdims=True) for s in scores])
    if extra_logit is not None:
        m = jnp.maximum(m, extra_logit)
    den = jnp.exp2(extra_logit - m) if extra_logit is not None else 0.0
    o = 0.0
    for s, (_, v, _, _) in zip(scores, segments):
        e = jnp.exp2(s - m)
        den = den + jnp.sum(e, axis=-1, keepdims=True)
        o = o + _dot(e.astype(BF16), v)
    return o / den


def _causal_split(i, tq, nk, S):
    n_full = nk - S // N_CAUSAL_CLASSES
    t_col = i * tq + _iota((tq, 1), 0)
    tail_causal = (n_full + _iota((1, nk - n_full), 1)) <= t_col
    return n_full, tail_causal


N_CAUSAL_CLASSES = 8


def _for_causal_class(i, tq, S, body):
    step = S // N_CAUSAL_CLASSES
    cls = (i * tq + tq - 1) // step
    for c in range(N_CAUSAL_CLASSES):
        pl.when(cls == c)(functools.partial(body, (c + 1) * step))


def _nsa_kernel(q_ref, sm_ref, kc_ref, vc_ref, ov_ref, ex_ref, ksel_ref, vsel_ref, kwin_ref, vwin_ref, o_ref):
    i = pl.program_id(1)
    tq = q_ref.shape[0]
    S = ksel_ref.shape[0]
    ncmp = kc_ref.shape[0]
    q = q_ref[...]
    sm = sm_ref[...]
    t_col = i * tq + _iota((tq, 1), 0)

    c_end = _iota((1, ncmp), 1) * NSA_CMP_STRIDE + (NSA_CMP_LEN - 1)
    allowed_c = c_end <= t_col
    kc = kc_ref[...]
    vc = vc_ref[...]
    o_cmp = []
    p_cmp = []
    for hh in range(N_HEADS):
        qh = q[:, hh * 64:(hh + 1) * 64]
        s = jnp.where(allowed_c, _dot_t(qh, kc), NEG_INF)
        m = jnp.max(s, axis=-1, keepdims=True)
        e = jnp.where(allowed_c, jnp.exp2(s - m), 0.0)
        den = jnp.sum(e, axis=-1, keepdims=True)
        p = e * jnp.where(den > 0.0, 1.0 / den, 0.0)
        p_cmp.append(p.astype(BF16))
        o_cmp.append(_dot(p_cmp[-1], vc))

    n_blk = S // NSA_SEL_BLOCK
    n_sel = min(NSA_N_SEL, n_blk)

    def unselected_block_bias():
        imp = sum(_dot_t(ov_ref[...], p) for p in p_cmp)
        jb = _iota((n_blk, tq), 0)
        cur = (i * tq + _iota((1, tq), 1)) // NSA_SEL_BLOCK
        forced = (jb == 0) | (jb == cur) | (jb == cur - 1)
        imp = jnp.where(forced, NSA_FORCE_SCORE, imp)
        imp = jnp.where(jb > cur, -NSA_FORCE_SCORE, imp)
        rank = jnp.zeros((n_blk, tq), F32)
        for jp in range(n_blk):
            row = imp[jp:jp + 1, :]
            ahead = (row > imp) | ((row == imp) & (jp < jb))
            rank = rank + jnp.where(ahead, 1.0, 0.0)
        unsel = jnp.transpose(jnp.where(rank < float(n_sel), 0.0, NEG_INF))
        return jnp.concatenate([unsel, jnp.zeros((tq, LANES - n_blk), F32)], axis=1).astype(BF16)

    nband = NSA_WINDOW + tq
    start = pl.multiple_of(jnp.maximum(i * tq - NSA_WINDOW, 0), math.gcd(tq, NSA_WINDOW))
    kwin = kwin_ref[pl.ds(start, nband), :]
    vwin = vwin_ref[pl.ds(start, nband), :]
    s_band = start + _iota((1, nband), 1)
    allowed_win = (s_band <= t_col) & (s_band > t_col - NSA_WINDOW)
    gate = lambda hh, k: sm[:, SM_GATE0 + 3 * hh + k:SM_GATE0 + 3 * hh + k + 1]
    partial_out = []
    for hh in range(N_HEADS):
        o_win = _attend(q[:, hh * 64:(hh + 1) * 64], [(kwin, vwin, None, allowed_win)])
        partial_out.append(gate(hh, 0) * o_cmp[hh] + gate(hh, 2) * o_win)

    def selected(nk):
        n_full, tail_causal = _causal_split(i, tq, nk, S)
        if nk <= n_sel * NSA_SEL_BLOCK:
            part_bias = lambda lo, hi: None
        else:
            bias = _dot(unselected_block_bias(), ex_ref[:, 0:nk])
            part_bias = lambda lo, hi: bias[:, lo:hi]
        segments = [(ksel_ref[n_full:nk, :], vsel_ref[n_full:nk, :], part_bias(n_full, nk), tail_causal)]
        if n_full:
            segments.append((ksel_ref[0:n_full, :], vsel_ref[0:n_full, :], part_bias(0, n_full), None))
        outs = []
        for hh in range(N_HEADS):
            o_sel = _attend(q[:, hh * 64:(hh + 1) * 64], segments)
            outs.append(partial_out[hh] + gate(hh, 1) * o_sel)
        o_ref[...] = jnp.concatenate(outs, axis=1).astype(BF16)

    _for_causal_class(i, tq, S, selected)


def _nsa_call(qa, sm, kcc, vcc, overlap, expand, ksel, vsel, kwin, vwin, B, S):
    T = B * S
    nq = S // TQ
    ncmp = S // NSA_CMP_STRIDE
    tile = lambda b, i: (b * nq + i, 0)
    seq = lambda b, i: (b, 0)
    const = lambda b, i: (0, 0)
    in_specs = [pl.BlockSpec((TQ, 256), tile), pl.BlockSpec((TQ, LANES), tile),
                pl.BlockSpec((ncmp, 64), seq), pl.BlockSpec((ncmp, 64), seq),
                pl.BlockSpec(overlap.shape, const), pl.BlockSpec(expand.shape, const)] \
        + [pl.BlockSpec((S, 64), seq)] * 4
    return pl.pallas_call(
        _nsa_kernel, grid=(B, nq), in_specs=in_specs, out_specs=pl.BlockSpec((TQ, 256), tile),
        out_shape=jax.ShapeDtypeStruct((T, 256), BF16),
        compiler_params=_cparams(("arbitrary", "arbitrary")),
    )(qa, sm, kcc, vcc, overlap, expand, ksel, vsel, kwin, vwin)


def _fox_kernel(q_ref, nbias_ref, k_ref, v_ref, o_ref):
    i = pl.program_id(1)
    tq = q_ref.shape[0]
    S = k_ref.shape[1]
    q = q_ref[...]

    def attend(nk):
        n_full, tail_causal = _causal_split(i, tq, nk, S)
        outs = []
        for hh in range(N_HEADS):
            segments = [(k_ref[hh, n_full:nk, :], v_ref[hh, n_full:nk, :], nbias_ref[0, hh:hh + 1, n_full:nk],
                         tail_causal)]
            if n_full:
                segments.append((k_ref[hh, 0:n_full, :], v_ref[hh, 0:n_full, :], nbias_ref[0, hh:hh + 1, 0:n_full],
                                 None))
            outs.append(_attend(q[:, hh * 64:(hh + 1) * 64], segments))
        o_ref[...] = jnp.concatenate(outs, axis=1).astype(BF16)

    _for_causal_class(i, tq, S, attend)


def _fox_call(qb, nbias, kb, vb, B, S):
    T = B * S
    nq = S // TQ
    tile = lambda b, i: (b * nq + i, 0)
    in_specs = [pl.BlockSpec((TQ, 256), tile),
                pl.BlockSpec((1, N_HEADS, S), lambda b, i: (b, 0, 0)),
                pl.BlockSpec((N_HEADS, S, 64), lambda b, i: (0, b, 0)),
                pl.BlockSpec((N_HEADS, S, 64), lambda b, i: (0, b, 0))]
    return pl.pallas_call(
        _fox_kernel, grid=(B, nq), in_specs=in_specs, out_specs=pl.BlockSpec((TQ, 256), tile),
        out_shape=jax.ShapeDtypeStruct((T, 256), BF16),
        compiler_params=_cparams(("arbitrary", "arbitrary")),
    )(qb, nbias, kb, vb)


def _dsa_kernel(q_ref, qi_ref, sm_ref, ki_ref, cnt_ref, k_ref, v_ref, o_ref, score_ref):
    i = pl.program_id(1)
    tq = q_ref.shape[0]
    S = ki_ref.shape[0]
    sm = sm_ref[...]
    qi = qi_ref[...]
    t_col = i * tq + _iota((tq, 1), 0)
    top_k = min(DSA_TOPK_MAX, S // 4)
    k_eff = jnp.minimum(t_col + 1, top_k).astype(F32)
    q = q_ref[...]

    def attend(nk, bias, allowed):
        outs = [_attend(q[:, hh * 64:(hh + 1) * 64], [(k_ref[hh, 0:nk, :], v_ref[hh, 0:nk, :], bias, allowed)])
                for hh in range(N_HEADS)]
        o_ref[...] = jnp.concatenate(outs, axis=1).astype(BF16)

    def float_of_ordered_bits(u):
        k = u ^ INT_MIN
        return lax.bitcast_convert_type(jnp.where(k >= 0, k, k ^ 0x7FFFFFFF), F32)

    def select_and_attend(nk):
        causal = _iota((1, nk), 1) <= t_col
        score = jnp.zeros((tq, nk), F32)
        for hh in range(N_HEADS):
            lg = _dot_t(qi[:, hh * 64:(hh + 1) * 64], ki_ref[0:nk, :])
            score = score + sm[:, SM_W0 + hh:SM_W0 + hh + 1] * jnp.maximum(lg, 0.0)
        score_ref[:, 0:nk] = jnp.where(causal, score, -jnp.inf)

        def search(it, u):
            cand = u | lax.shift_left(jnp.int32(1), 31 - it)
            thr = float_of_ordered_bits(cand)
            cnt = jnp.sum(jnp.where(score_ref[:, 0:nk] >= thr, 1.0, 0.0), axis=-1, keepdims=True)
            return jnp.where(cnt >= k_eff, cand, u)

        u = lax.fori_loop(0, 32, search, jnp.zeros((tq, 1), I32))
        thr = float_of_ordered_bits(u)
        score = score_ref[:, 0:nk]
        gt = score > thr
        eq = score == thr
        need = k_eff - jnp.sum(jnp.where(gt, 1.0, 0.0), axis=-1, keepdims=True)
        eqb = jnp.where(eq, 1.0, 0.0).astype(BF16)
        run = jnp.zeros((tq, LANES), F32)
        bias_chunks = []
        for c in range(nk // LANES):
            sl = slice(c * LANES, (c + 1) * LANES)
            r = _dot(eqb[:, sl], cnt_ref[...])
            prefix = r[:, :LANES] + run
            run = run + r[:, LANES:]
            keep = gt[:, sl] | (eq[:, sl] & (prefix < need))
            bias_chunks.append(jnp.where(keep, 0.0, NEG_INF))
        attend(nk, jnp.concatenate(bias_chunks, axis=1), None)

    def body(nk):
        if nk * N_CAUSAL_CLASSES == S:
            all_kept = (i + 1) * tq <= top_k
            pl.when(all_kept)(lambda: attend(nk, None, _iota((1, nk), 1) <= t_col))
            pl.when(jnp.logical_not(all_kept))(lambda: select_and_attend(nk))
        else:
            select_and_attend(nk)

    _for_causal_class(i, tq, S, body)


def _dsa_call(qc, qi, sm, ki, cntmat, kc, vc, B, S):
    T = B * S
    nq = S // TQ
    tile = lambda b, i: (b * nq + i, 0)
    in_specs = [pl.BlockSpec((TQ, 256), tile), pl.BlockSpec((TQ, 256), tile), pl.BlockSpec((TQ, LANES), tile),
                pl.BlockSpec((S, 64), lambda b, i: (b, 0)),
                pl.BlockSpec(cntmat.shape, lambda b, i: (0, 0)),
                pl.BlockSpec((N_HEADS, S, 64), lambda b, i: (0, b, 0)),
                pl.BlockSpec((N_HEADS, S, 64), lambda b, i: (0, b, 0))]
    return pl.pallas_call(
        _dsa_kernel, grid=(B, nq), in_specs=in_specs, out_specs=pl.BlockSpec((TQ, 256), tile),
        out_shape=jax.ShapeDtypeStruct((T, 256), BF16),
        scratch_shapes=[pltpu.VMEM((TQ, S), F32)],
        compiler_params=_cparams(("arbitrary", "arbitrary")),
    )(qc, qi, sm, ki, cntmat, kc, vc)


def _swa_kernel(q_ref, sink_ref, k_ref, v_ref, o_ref):
    i = pl.program_id(1)
    tq = q_ref.shape[0]
    q = q_ref[...]
    t_col = i * tq + _iota((tq, 1), 0)
    nband = SWA_WINDOW + tq
    start = pl.multiple_of(jnp.maximum(i * tq - SWA_WINDOW, 0), math.gcd(tq, SWA_WINDOW))
    s_band = start + _iota((1, nband), 1)
    allowed = (s_band <= t_col) & (s_band > t_col - SWA_WINDOW)
    group = N_HEADS // SWA_KV_HEADS
    outs = []
    for hh in range(N_HEADS):
        k = k_ref[hh // group, pl.ds(start, nband), :]
        v = v_ref[hh // group, pl.ds(start, nband), :]
        sink = jnp.zeros((tq, 1), F32) + LOG2E * sink_ref[0:1, hh:hh + 1]
        outs.append(_attend(q[:, hh * 64:(hh + 1) * 64], [(k, v, None, allowed)], extra_logit=sink))
    o_ref[...] = jnp.concatenate(outs, axis=1).astype(BF16)


def _swa_call(qd, sinks, kd, vd, B, S):
    T = B * S
    nq = S // TQ
    tile = lambda b, i: (b * nq + i, 0)
    in_specs = [pl.BlockSpec((TQ, 256), tile), pl.BlockSpec((1, LANES), lambda b, i: (0, 0)),
                pl.BlockSpec((SWA_KV_HEADS, S, 64), lambda b, i: (0, b, 0)),
                pl.BlockSpec((SWA_KV_HEADS, S, 64), lambda b, i: (0, b, 0))]
    return pl.pallas_call(
        _swa_kernel, grid=(B, nq), in_specs=in_specs, out_specs=pl.BlockSpec((TQ, 256), tile),
        out_shape=jax.ShapeDtypeStruct((T, 256), BF16),
        compiler_params=_cparams(("arbitrary", "arbitrary")),
    )(qd, sinks, kd, vd)


RT_E1, RT_E2, RT_W1, RT_W2, RT_R1, RT_R2 = 0, 1, 2, 3, 4, 5
ROUTER_E0 = MOE_GROUPS


def _merge_kernel(h_ref, oa_ref, ob_ref, oc_ref, od_ref, gain_ref, wg_ref, wb_ref, wo_ref,
                  fgain_ref, wr_ref, br_ref, h2_ref, xn_ref, rt_ref, cnt_ref, base_ref):
    step = pl.program_id(0)
    tm = h_ref.shape[0]
    h = h_ref[...]
    hn = h * lax.rsqrt(jnp.mean(h * h, axis=-1, keepdims=True) + NORM_EPS) * gain_ref[...]
    hb = hn.astype(BF16)
    merged = jnp.zeros_like(h)
    for i, o_ref in enumerate((oa_ref, ob_ref, oc_ref, od_ref)):
        gate = 1.0 / (1.0 + jnp.exp(-_dot(hb, wg_ref[i])))
        merged = merged + gate * _dot(o_ref[...], wb_ref[i])
    h2 = h + _dot(merged.astype(BF16), wo_ref[...])
    h2_ref[...] = h2

    xn = h2 * lax.rsqrt(jnp.mean(h2 * h2, axis=-1, keepdims=True) + NORM_EPS) * fgain_ref[...]
    xn_ref[...] = xn
    logits = jnp.dot(xn, wr_ref[...], preferred_element_type=F32, precision=lax.Precision.HIGHEST) + br_ref[...]

    lane = _iota((tm, LANES), 1)
    lanef = lane.astype(F32)
    big = float(LANES)
    is_g = lane < MOE_GROUPS
    gl = jnp.where(is_g, logits, NEG_INF)
    gmax = jnp.max(gl, axis=-1, keepdims=True)
    grp = jnp.min(jnp.where(is_g & (gl == gmax), lanef, big), axis=-1, keepdims=True)
    p_grp = 1.0 / jnp.sum(jnp.where(is_g, jnp.exp(gl - gmax), 0.0), axis=-1, keepdims=True)
    lo = ROUTER_E0 + grp * MOE_EXPERTS_PER_GROUP
    in_grp = (lanef >= lo) & (lanef < lo + MOE_EXPERTS_PER_GROUP)
    el = jnp.where(in_grp, logits, NEG_INF)
    v1 = jnp.max(el, axis=-1, keepdims=True)
    l1 = jnp.min(jnp.where(in_grp & (el == v1), lanef, big), axis=-1, keepdims=True)
    el2 = jnp.where(lanef == l1, NEG_INF, el)
    v2 = jnp.max(el2, axis=-1, keepdims=True)
    l2 = jnp.min(jnp.where(in_grp & (lanef != l1) & (el2 == v2), lanef, big), axis=-1, keepdims=True)
    e21 = jnp.exp(v2 - v1)
    w1 = p_grp / (1.0 + e21)
    w2 = p_grp * e21 / (1.0 + e21)

    oh1 = jnp.where(lanef == l1, 1.0, 0.0)
    oh2 = jnp.where(lanef == l2, 1.0, 0.0)

    @pl.when(step == 0)
    def _():
        base_ref[...] = jnp.zeros_like(base_ref)

    both = oh1 + oh2
    strict = (_iota((tm, tm), 0) > _iota((tm, tm), 1)).astype(BF16)
    before = _dot(strict, both.astype(BF16)) + base_ref[0:1, :]
    r1 = jnp.sum(oh1 * before, axis=-1, keepdims=True)
    r2 = jnp.sum(oh2 * before, axis=-1, keepdims=True)
    total = base_ref[0:1, :] + jnp.sum(both, axis=0, keepdims=True)
    base_ref[0:1, :] = total
    cnt_ref[...] = jnp.broadcast_to(total, cnt_ref.shape)

    rt = jnp.zeros((tm, LANES), F32)
    for ln, val in ((RT_E1, l1 - ROUTER_E0), (RT_E2, l2 - ROUTER_E0), (RT_W1, w1), (RT_W2, w2), (RT_R1, r1), (RT_R2, r2)):
        rt = jnp.where(lane == ln, val, rt)
    rt_ref[...] = rt


def _merge_call(h, oa, ob, oc, od, gain, wg, wb, wo, fgain, wr, br):
    T, D = h.shape
    tm = TM_MERGE
    tile = lambda i: (i, 0)
    c2 = lambda i: (0, 0)
    c3 = lambda i: (0, 0, 0)
    once = pl.Buffered(1)
    in_specs = [pl.BlockSpec((tm, D), tile)] + [pl.BlockSpec((tm, 256), tile)] * 4 + [
        pl.BlockSpec((1, D), c2),
        pl.BlockSpec(wg.shape, c3, pipeline_mode=once),
        pl.BlockSpec(wb.shape, c3, pipeline_mode=once),
        pl.BlockSpec(wo.shape, c2, pipeline_mode=once),
        pl.BlockSpec((1, D), c2),
        pl.BlockSpec(wr.shape, c2),
        pl.BlockSpec((1, LANES), c2),
    ]
    out_shape = [jax.ShapeDtypeStruct((T, D), F32), jax.ShapeDtypeStruct((T, D), F32),
                 jax.ShapeDtypeStruct((T, LANES), F32), jax.ShapeDtypeStruct((8, LANES), F32)]
    out_specs = [pl.BlockSpec((tm, D), tile), pl.BlockSpec((tm, D), tile), pl.BlockSpec((tm, LANES), tile),
                 pl.BlockSpec((8, LANES), c2)]
    return pl.pallas_call(
        _merge_kernel, grid=(T // tm,), in_specs=in_specs, out_specs=out_specs, out_shape=out_shape,
        scratch_shapes=[pltpu.VMEM((8, LANES), F32)],
        compiler_params=_cparams(("arbitrary",)),
    )(h, oa, ob, oc, od, gain, wg, wb, wo, fgain, wr, br)


def _row_copy(src, src_row, dst, dst_row, sem):
    return pltpu.make_async_copy(src.at[pl.ds(src_row, 1)], dst.at[pl.ds(dst_row, 1)], sem)


def _dest_kernel(rt_ref, ps_ref, d_ref):
    rt = rt_ref[...]
    lane = _iota(rt.shape, 1)
    lanef = lane.astype(F32)
    ps = ps_ref[...]

    def dest(e_lane, r_lane):
        start = jnp.sum(jnp.where(lanef == rt[:, e_lane:e_lane + 1], ps, 0.0), axis=-1, keepdims=True)
        return start + rt[:, r_lane:r_lane + 1]

    d = jnp.where(lane == 0, dest(RT_E1, RT_R1), jnp.where(lane == 1, dest(RT_E2, RT_R2), 0.0))
    d_ref[...] = d.astype(I32)


def _dest_call(rt, pstart_row):
    T = rt.shape[0]
    tm = TM_MERGE
    return pl.pallas_call(
        _dest_kernel, grid=(T // tm,),
        in_specs=[pl.BlockSpec((tm, LANES), lambda i: (i, 0)), pl.BlockSpec((1, LANES), lambda i: (0, 0))],
        out_specs=pl.BlockSpec((tm, LANES), lambda i: (i, 0)), out_shape=jax.ShapeDtypeStruct((T, LANES), I32),
        compiler_params=_cparams(("arbitrary",)),
    )(rt, pstart_row)


DMA_ISSUE_UNROLL = 8


def _dispatch_kernel(d1_ref, d2_ref, xn_ref, xs_in_ref, xs_ref, sem):
    del xs_in_ref
    base = pl.program_id(0) * TD_DISPATCH

    def copies(r):
        t = base + r
        return (_row_copy(xn_ref, r, xs_ref, d1_ref[t], sem), _row_copy(xn_ref, r, xs_ref, d2_ref[t], sem))

    def start(r, c):
        for cp in copies(r):
            cp.start()
        return c

    def wait(r, c):
        for cp in copies(r):
            cp.wait()
        return c

    lax.fori_loop(0, TD_DISPATCH, start, 0, unroll=DMA_ISSUE_UNROLL)
    lax.fori_loop(0, TD_DISPATCH, wait, 0)


def _dispatch_call(d1, d2, xn, xs_zero):
    T, D = xn.shape
    any_spec = pl.BlockSpec(memory_space=pl.ANY)
    grid_spec = pltpu.PrefetchScalarGridSpec(
        num_scalar_prefetch=2, grid=(T // TD_DISPATCH,),
        in_specs=[pl.BlockSpec((TD_DISPATCH, D), lambda i, *_: (i, 0)), any_spec], out_specs=any_spec,
        scratch_shapes=[pltpu.SemaphoreType.DMA(())])
    return pl.pallas_call(
        _dispatch_kernel, grid_spec=grid_spec, out_shape=jax.ShapeDtypeStruct(xs_zero.shape, xs_zero.dtype),
        input_output_aliases={3: 0},
        compiler_params=pltpu.CompilerParams(dimension_semantics=("arbitrary",), has_side_effects=True,
                                             vmem_limit_bytes=VMEM_LIMIT),
    )(d1, d2, xn, xs_zero)


def _expert_kernel(be_ref, used_ref, x_ref, wg_ref, wu_ref, wd_ref, y_ref):
    i = pl.program_id(0)

    @pl.when(i < used_ref[0])
    def _():
        x = x_ref[...].astype(BF16)
        g = _dot(x, wg_ref[0, 0].astype(BF16))
        u = _dot(x, wu_ref[0, 0].astype(BF16))
        act = g / (1.0 + jnp.exp(-g)) * u
        y_ref[...] = _dot(act.astype(BF16), wd_ref[0, 0].astype(BF16))

    @pl.when(i >= used_ref[0])
    def _():
        y_ref[...] = jnp.zeros_like(y_ref)


def _expert_call(blk_expert, used, xs, wg, wu, wd, layer):
    rows, D = xs.shape
    rb = MOE_ROW_BLOCK
    DE = wg.shape[3]
    xmap = lambda i, be, used: (jnp.maximum(jnp.minimum(i, used[0] - 1), 0), 0)
    wmap = lambda i, be, used: (layer, be[i], 0, 0)
    grid_spec = pltpu.PrefetchScalarGridSpec(
        num_scalar_prefetch=2, grid=(rows // rb,),
        in_specs=[pl.BlockSpec((rb, D), xmap), pl.BlockSpec((1, 1, D, DE), wmap), pl.BlockSpec((1, 1, D, DE), wmap),
                  pl.BlockSpec((1, 1, DE, D), wmap)],
        out_specs=pl.BlockSpec((rb, D), lambda i, be, used: (i, 0)))
    return pl.pallas_call(
        _expert_kernel, grid_spec=grid_spec, out_shape=jax.ShapeDtypeStruct((rows, D), F32),
        compiler_params=_cparams(("arbitrary",)),
    )(blk_expert, used, xs, wg, wu, wd)


def _combine_kernel(final_norm, d1_ref, d2_ref, h_ref, rt_ref, gain_ref, ys_ref, o_ref, buf_ref, sem):
    base = pl.program_id(0) * TC_COMBINE

    def copies(r):
        t = base + r
        return (_row_copy(ys_ref, d1_ref[t], buf_ref.at[0], r, sem),
                _row_copy(ys_ref, d2_ref[t], buf_ref.at[1], r, sem))

    def start(r, c):
        for cp in copies(r):
            cp.start()
        return c

    def wait(r, c):
        for cp in copies(r):
            cp.wait()
        return c

    lax.fori_loop(0, TC_COMBINE, start, 0, unroll=DMA_ISSUE_UNROLL)
    lax.fori_loop(0, TC_COMBINE, wait, 0)
    rt = rt_ref[...]
    out = h_ref[...] + rt[:, RT_W1:RT_W1 + 1] * buf_ref[0] + rt[:, RT_W2:RT_W2 + 1] * buf_ref[1]
    if final_norm:
        out = out * lax.rsqrt(jnp.mean(out * out, axis=-1, keepdims=True) + NORM_EPS) * gain_ref[...]
    o_ref[...] = out


def _combine_call(d1, d2, h, rt, gain, ys, final_norm):
    T, D = h.shape
    tc = TC_COMBINE
    tile = lambda i, *_: (i, 0)
    grid_spec = pltpu.PrefetchScalarGridSpec(
        num_scalar_prefetch=2, grid=(T // tc,),
        in_specs=[pl.BlockSpec((tc, D), tile), pl.BlockSpec((tc, LANES), tile),
                  pl.BlockSpec((1, D), lambda i, *_: (0, 0)), pl.BlockSpec(memory_space=pl.ANY)],
        out_specs=pl.BlockSpec((tc, D), tile),
        scratch_shapes=[pltpu.VMEM((2, tc, D), F32), pltpu.SemaphoreType.DMA(())])
    return pl.pallas_call(
        functools.partial(_combine_kernel, final_norm), grid_spec=grid_spec,
        out_shape=jax.ShapeDtypeStruct((T, D), F32),
        compiler_params=_cparams(("arbitrary",)),
    )(d1, d2, h, rt, gain, ys)


def _in_proj_columns():
    widths = (('q_a', 256), ('k_cmp', 64), ('v_cmp', 64), ('k_sel', 64), ('v_sel', 64), ('k_win', 64), ('v_win', 64),
              ('g_a', 12), ('q_b', 256), ('k_b', 256), ('v_b', 256), ('f_b', 4), ('q_c', 256), ('ckv_c', 128),
              ('qi_c', 256), ('ki_c', 64), ('wi_c', 4), ('q_d', 256), ('k_d', 128), ('v_d', 128))
    off, cols = 0, {}
    for name, w in widths:
        cols[name] = np.arange(off, off + w)
        off += w
    cat = lambda names: np.concatenate([cols[n] for n in names])
    rope = cat(('q_a', 'q_c', 'qi_c', 'q_d', 'k_d', 'k_cmp', 'k_sel', 'k_win', 'ki_c'))
    plain = cat(('q_b', 'k_b', 'v_b', 'v_d', 'ckv_c', 'v_cmp', 'v_sel', 'v_win'))
    small = cat(('g_a', 'f_b', 'wi_c'))
    assert rope.size == N_ROPE_COLS and plain.size == N_PLAIN_COLS
    return np.concatenate([rope, plain, small]), N_SMALL_COLS - small.size


def _static_tables(S):
    n_cmp_pad = S // NSA_CMP_STRIDE
    n_blk = S // NSA_SEL_BLOCK
    c0 = np.arange(n_cmp_pad) * NSA_CMP_STRIDE
    b0 = np.arange(n_blk) * NSA_SEL_BLOCK
    ov = ((c0[:, None] < b0[None, :] + NSA_SEL_BLOCK) & (c0[:, None] + NSA_CMP_LEN > b0[None, :])).astype(np.float32)
    overlap = np.ascontiguousarray(ov.T)
    expand = np.zeros((LANES, S), np.float32)
    expand[np.arange(S) // NSA_SEL_BLOCK, np.arange(S)] = 1.0
    strict_upper = (np.arange(LANES)[:, None] < np.arange(LANES)[None, :]).astype(np.float32)
    cntmat = np.concatenate([strict_upper, np.ones((LANES, LANES), np.float32)], axis=1)
    return jnp.asarray(overlap, BF16), jnp.asarray(expand, BF16), jnp.asarray(cntmat, BF16)


def kernel(x, positions, attn_norm, w_in, nsa_pe_k, nsa_w1_k, nsa_w2_k, nsa_pe_v, nsa_w1_v, nsa_w2_v, fox_forget_bias, dsa_kv_norm, dsa_w_ukv, swa_sinks, w_branch, w_gate, w_out, ffn_norm, moe_w_group, moe_b_group, moe_w_expert, moe_b_expert, moe_w_gate, moe_w_up, moe_w_down, final_norm):
    B, S, D = x.shape
    T = B * S
    depth = w_in.shape[0]
    perm, n_pad = _in_proj_columns()
    overlap, expand, cntmat = _static_tables(S)
    half = HEAD_DIM // 2
    inv_freq = ROPE_THETA ** (-jnp.arange(half, dtype=F32) / half)
    invf = jnp.tile(inv_freq, LANES // half).reshape(1, LANES)
    pos = positions.reshape(T, 1).astype(I32)
    small_pad = lambda v, off: jnp.zeros((1, LANES), F32).at[0, off:off + v.shape[0]].set(v.astype(F32))

    n_rows = -(-(T * 2 + MOE_N_EXPERTS * (MOE_ROW_BLOCK - 1)) // MOE_ROW_BLOCK) * MOE_ROW_BLOCK
    h = x.reshape(T, D)
    for l in range(depth):
        w = jnp.pad(w_in[l][:, perm], ((0, 0), (0, n_pad))).astype(BF16)
        (qa, qc, qi, qd, kd, kcmp, ksel, kwin, ki, qb, kb, vb, vd, vcmp, vsel, vwin, kc, vc, sm) = _proj_call(
            h, pos, invf, attn_norm[l].reshape(1, D), w, small_pad(fox_forget_bias[l], SM_F0),
            dsa_kv_norm[l].reshape(1, DSA_KV_RANK), dsa_w_ukv[l].astype(BF16), B, S)

        cw = NSA_CMP_STRIDE * HEAD_DIM
        pe_rows = lambda pe: jnp.broadcast_to(pe.reshape(1, 2 * cw), (8, 2 * cw)).astype(BF16)
        kcc, vcc = _cmp_call(kcmp.reshape(T // NSA_CMP_STRIDE, cw), vcmp.reshape(T // NSA_CMP_STRIDE, cw),
                             pe_rows(nsa_pe_k[l]), nsa_w1_k[l].astype(BF16), nsa_w2_k[l].astype(BF16),
                             pe_rows(nsa_pe_v[l]), nsa_w1_v[l].astype(BF16), nsa_w2_v[l].astype(BF16), B, S)
        o_a = _nsa_call(qa, sm, kcc, vcc, overlap, expand, ksel, vsel, kwin, vwin, B, S)

        key_bias = sm[:, SM_F0:SM_F0 + N_HEADS].reshape(B, S, N_HEADS).transpose(0, 2, 1)
        o_b = _fox_call(qb, key_bias, kb, vb, B, S)
        o_c = _dsa_call(qc, qi, sm, ki, cntmat, kc, vc, B, S)
        o_d = _swa_call(qd, small_pad(swa_sinks[l], 0), kd, vd, B, S)

        wr = jnp.zeros((D, LANES), F32).at[:, :MOE_GROUPS].set(moe_w_group[l]) \
            .at[:, ROUTER_E0:ROUTER_E0 + MOE_N_EXPERTS].set(moe_w_expert[l])
        br = jnp.zeros((1, LANES), F32).at[0, :MOE_GROUPS].set(moe_b_group[l]) \
            .at[0, ROUTER_E0:ROUTER_E0 + MOE_N_EXPERTS].set(moe_b_expert[l])
        h2, xn, rt, cnt = _merge_call(h, o_a, o_b, o_c, o_d, attn_norm[l].reshape(1, D), w_gate[l].astype(BF16),
                                      w_branch[l].astype(BF16), w_out[l].astype(BF16), ffn_norm[l].reshape(1, D),
                                      wr, br)

        counts = cnt[0, ROUTER_E0:ROUTER_E0 + MOE_N_EXPERTS].astype(I32)
        padded = (counts + MOE_ROW_BLOCK - 1) // MOE_ROW_BLOCK * MOE_ROW_BLOCK
        p_end = jnp.cumsum(padded)
        p_start = p_end - padded
        n_blocks = n_rows // MOE_ROW_BLOCK
        blk_first_row = jnp.arange(n_blocks, dtype=I32) * MOE_ROW_BLOCK
        blk_expert = jnp.minimum(jnp.sum((p_end[None, :] <= blk_first_row[:, None]).astype(I32), axis=1),
                                 MOE_N_EXPERTS - 1).astype(I32)
        used = (p_end[-1:] // MOE_ROW_BLOCK).astype(I32)
        dest = _dest_call(rt, small_pad(p_start, 0))
        d1, d2 = dest[:, 0], dest[:, 1]

        xs = _dispatch_call(d1, d2, xn, jnp.zeros((n_rows, D), F32))
        ys = _expert_call(blk_expert, used, xs, moe_w_gate, moe_w_up, moe_w_down, l)
        last = l == depth - 1
        h = _combine_call(d1, d2, h2, rt, final_norm.reshape(1, D), ys, last)
    return h.reshape(B, S, D)
```

```python
import functools
import math

import numpy as np
import jax
import jax.numpy as jnp
from jax import lax
from jax.experimental import pallas as pl
from jax.experimental.pallas import tpu as pltpu

F32 = jnp.float32
BF16 = jnp.bfloat16
I32 = jnp.int32

HEAD_DIM = 64
N_HEADS = 4
MIX_WIDTH = N_HEADS * HEAD_DIM
ROPE_THETA = 10000.0
NORM_EPS = 1e-6
NEG_INF = -1e30
INT_MIN = -2 ** 31
LOG2E = float(np.log2(np.e))

NSA_CMP_LEN = 32
NSA_CMP_STRIDE = 16
NSA_CMP_HIDDEN = 256
NSA_SEL_BLOCK = 64
NSA_N_SEL = 16
NSA_WINDOW = 512
NSA_FORCE_SCORE = 1e4
DSA_KV_RANK = 128
DSA_TOPK_MAX = 256
SWA_WINDOW = 128
SWA_KV_HEADS = 2
MOE_GROUPS = 8
MOE_EXPERTS_PER_GROUP = 8
MOE_N_EXPERTS = 64
MOE_D_EXPERT = 256

LANES = 128
VMEM_LIMIT = 48 * 1024 * 1024

N_ROPE_COLS = 1408
N_PLAIN_COLS = 1216
N_SMALL_COLS = 128
N_PROJ_COLS = N_ROPE_COLS + N_PLAIN_COLS + N_SMALL_COLS
SM_GATE0, SM_F0, SM_W0 = 0, 12, 16

TM_PROJ = 512
PROJ_CHUNK = 128
TQ = 256
TM_MERGE = 512
MERGE_CHUNK = 128
MOE_ROW_BLOCK = 256
TD_DISPATCH = 256
TC_COMBINE = 256


def _cparams(sem):
    return pltpu.CompilerParams(dimension_semantics=sem, vmem_limit_bytes=VMEM_LIMIT)


def _dot(a, b):
    return jnp.dot(a, b, preferred_element_type=F32)


def _dot_t(a, b):
    return lax.dot_general(a, b, (((1,), (1,)), ((), ())), preferred_element_type=F32)


def _iota(shape, dim):
    return lax.broadcasted_iota(I32, shape, dim)


def _rope_slab(x, cos, sin_signed, first_half):
    rot = jnp.where(first_half, pltpu.roll(x, 96, 1), pltpu.roll(x, 32, 1))
    return x * cos + rot * sin_signed


def _proj_kernel(h_ref, pos_ref, invf_ref, gain_ref, w_ref, fbias_ref, kvn_ref, wukv_ref,
                 qa_ref, qc_ref, qi_ref, qd_ref, kd_ref, kcmp_ref, ksel_ref, kwin_ref, ki_ref,
                 qb_ref, kb_ref, vb_ref, vd_ref, vcmp_ref, vsel_ref, vwin_ref, kc_ref, vc_ref,
                 sm_ref, carry_ref):
    j = pl.program_id(1)
    tm = h_ref.shape[0]
    tc = PROJ_CHUNK

    @pl.when(j == 0)
    def _():
        carry_ref[...] = jnp.zeros_like(carry_ref)

    lane = _iota((tc, LANES), 1)
    first_half = (lane % HEAD_DIM) < (HEAD_DIM // 2)
    is_f = (lane >= SM_F0) & (lane < SM_W0)
    tri = (_iota((tc, tc), 0) >= _iota((tc, tc), 1)).astype(F32)
    scale = HEAD_DIM ** -0.5
    softmax_scale = scale * LOG2E

    for rows in (slice(c * tc, (c + 1) * tc) for c in range(tm // tc)):
        h = h_ref[rows, :]
        hn = h * lax.rsqrt(jnp.mean(h * h, axis=-1, keepdims=True) + NORM_EPS) * gain_ref[...]
        hb = hn.astype(BF16)

        ang = pos_ref[rows, :].astype(F32) * invf_ref[...]
        cos = jnp.cos(ang)
        sin = jnp.sin(ang)
        sin_signed = jnp.where(first_half, -sin, sin)
        rope = lambda x: _rope_slab(x, cos, sin_signed, first_half)

        zr = _dot(hb, w_ref[:, 0:N_ROPE_COLS])
        slab = lambda k: rope(zr[:, k * LANES:(k + 1) * LANES])
        for q_ref, k0, sc in ((qa_ref, 0, softmax_scale), (qc_ref, 2, softmax_scale), (qi_ref, 4, scale),
                              (qd_ref, 6, softmax_scale)):
            q_ref[rows, :] = (jnp.concatenate([slab(k0), slab(k0 + 1)], axis=1) * sc).astype(BF16)
        s8 = slab(8).astype(BF16)
        kd_ref[0, rows, :] = s8[:, :HEAD_DIM]
        kd_ref[1, rows, :] = s8[:, HEAD_DIM:]
        s9 = slab(9).astype(BF16)
        kcmp_ref[rows, :] = s9[:, :HEAD_DIM]
        ksel_ref[rows, :] = s9[:, HEAD_DIM:]
        s10 = slab(10).astype(BF16)
        kwin_ref[rows, :] = s10[:, :HEAD_DIM]
        ki_ref[rows, :] = s10[:, HEAD_DIM:]

        zp = _dot(hb, w_ref[:, N_ROPE_COLS:N_ROPE_COLS + N_PLAIN_COLS])
        qb_ref[rows, :] = (zp[:, 0:256] * softmax_scale).astype(BF16)
        for hh in range(N_HEADS):
            kb_ref[hh, rows, :] = zp[:, 256 + hh * 64:256 + (hh + 1) * 64].astype(BF16)
            vb_ref[hh, rows, :] = zp[:, 512 + hh * 64:512 + (hh + 1) * 64].astype(BF16)
        vd_ref[0, rows, :] = zp[:, 768:832].astype(BF16)
        vd_ref[1, rows, :] = zp[:, 832:896].astype(BF16)
        ckv = zp[:, 896:1024]
        vcmp_ref[rows, :] = zp[:, 1024:1088].astype(BF16)
        vsel_ref[rows, :] = zp[:, 1088:1152].astype(BF16)
        vwin_ref[rows, :] = zp[:, 1152:1216].astype(BF16)

        ckvn = ckv * lax.rsqrt(jnp.mean(ckv * ckv, axis=-1, keepdims=True) + NORM_EPS) * kvn_ref[...]
        kvc = _dot(ckvn.astype(BF16), wukv_ref[...])
        for k in range(2):
            kk = rope(kvc[:, k * LANES:(k + 1) * LANES]).astype(BF16)
            kc_ref[2 * k, rows, :] = kk[:, :HEAD_DIM]
            kc_ref[2 * k + 1, rows, :] = kk[:, HEAD_DIM:]
        for hh in range(N_HEADS):
            vc_ref[hh, rows, :] = kvc[:, 256 + hh * 64:256 + (hh + 1) * 64].astype(BF16)

        zs = _dot(hb, w_ref[:, N_ROPE_COLS + N_PLAIN_COLS:N_PROJ_COLS])
        sig = 1.0 / (1.0 + jnp.exp(-zs))
        xf = zs + fbias_ref[...]
        logf = jnp.minimum(xf, 0.0) - jnp.log(1.0 + jnp.exp(-jnp.abs(xf)))
        logf = jnp.where(is_f, logf, 0.0)
        cum = jnp.dot(tri, logf, preferred_element_type=F32, precision=lax.Precision.HIGHEST)
        cum = cum + carry_ref[0:1, :]
        carry_ref[0:1, :] = cum[tc - 1:tc, :]
        sm_ref[rows, :] = jnp.where(lane < SM_F0, sig,
                                    jnp.where(is_f, -LOG2E * cum, jnp.where(lane < SM_W0 + 4, 0.5 * zs, 0.0)))


def _proj_call(h, pos, invf, gain, w, fbias, kvn, wukv, B, S):
    T = B * S
    tm = TM_PROJ
    nj = S // tm
    row = lambda b, j: (b * nj + j, 0)
    hrow = lambda b, j: (0, b * nj + j, 0)
    const2 = lambda b, j: (0, 0)
    tok = lambda w_, dt: jax.ShapeDtypeStruct((T, w_), dt)
    hm = lambda n: jax.ShapeDtypeStruct((n, T, HEAD_DIM), BF16)
    out_shape = [tok(256, BF16)] * 4 + [hm(2)] + [tok(64, BF16)] * 4 + [tok(256, BF16), hm(4), hm(4), hm(2)] \
        + [tok(64, BF16)] * 3 + [hm(4), hm(4), tok(LANES, F32)]
    spec_tok = lambda w_: pl.BlockSpec((tm, w_), row)
    spec_hm = lambda n: pl.BlockSpec((n, tm, HEAD_DIM), hrow)
    out_specs = [spec_tok(256)] * 4 + [spec_hm(2)] + [spec_tok(64)] * 4 + [spec_tok(256), spec_hm(4), spec_hm(4), spec_hm(2)] \
        + [spec_tok(64)] * 3 + [spec_hm(4), spec_hm(4), spec_tok(LANES)]
    in_specs = [
        pl.BlockSpec((tm, h.shape[1]), row),
        pl.BlockSpec((tm, 1), row),
        pl.BlockSpec((1, LANES), const2),
        pl.BlockSpec((1, h.shape[1]), const2),
        pl.BlockSpec(w.shape, const2),
        pl.BlockSpec((1, LANES), const2),
        pl.BlockSpec((1, DSA_KV_RANK), const2),
        pl.BlockSpec(wukv.shape, const2),
    ]
    return pl.pallas_call(
        _proj_kernel, grid=(B, nj), in_specs=in_specs, out_specs=out_specs, out_shape=out_shape,
        scratch_shapes=[pltpu.VMEM((8, LANES), F32)],
        compiler_params=_cparams(("arbitrary", "arbitrary")),
    )(h, pos, invf, gain, w, fbias, kvn, wukv)


def _gelu_tanh(x):
    return 0.5 * x * (1.0 + jnp.tanh(np.sqrt(2.0 / np.pi).astype(np.float32) * (x + 0.044715 * (x * x * x))))


def _cmp_kernel(ck_ref, cv_ref, pek_ref, w1k_ref, w2k_ref, pev_ref, w1v_ref, w2v_ref, kc_ref, vc_ref):
    half = NSA_CMP_STRIDE * HEAD_DIM
    for c_ref, pe_ref, w1_ref, w2_ref, o_ref in ((ck_ref, pek_ref, w1k_ref, w2k_ref, kc_ref),
                                                 (cv_ref, pev_ref, w1v_ref, w2v_ref, vc_ref)):
        c = c_ref[...]
        a = _dot(c, w1_ref[0:half, :])
        b = _dot(c, w1_ref[half:2 * half, :])
        n = b.shape[0]
        pe_h = _dot(pe_ref[...], w1_ref[...])
        hid = a + pltpu.roll(b, n - 1, 0) + pe_h[0:1, :]
        o_ref[...] = _dot(_gelu_tanh(hid).astype(BF16), w2_ref[...]).astype(BF16)


def _cmp_call(ck, cv, pek, w1k, w2k, pev, w1v, w2v, B, S):
    nch = S // NSA_CMP_STRIDE
    blk = lambda b: (b, 0)
    const = lambda b: (0, 0)
    cw = NSA_CMP_STRIDE * HEAD_DIM
    in_specs = [pl.BlockSpec((nch, cw), blk), pl.BlockSpec((nch, cw), blk)]
    for _ in range(2):
        in_specs += [pl.BlockSpec((8, 2 * cw), const), pl.BlockSpec((2 * cw, NSA_CMP_HIDDEN), const),
                     pl.BlockSpec((NSA_CMP_HIDDEN, HEAD_DIM), const)]
    out = jax.ShapeDtypeStruct((B * nch, HEAD_DIM), BF16)
    return pl.pallas_call(
        _cmp_kernel, grid=(B,), in_specs=in_specs,
        out_specs=[pl.BlockSpec((nch, HEAD_DIM), blk)] * 2, out_shape=[out, out],
        compiler_params=_cparams(("arbitrary",)),
    )(ck, cv, pek, w1k, w2k, pev, w1v, w2v)


def _attend(q, segments, extra_logit=None):
    scores = []
    for k, _, bias, allowed in segments:
        s = _dot_t(q, k)
        if bias is not None:
            s = s + bias
        if allowed is not None:
            s = jnp.where(allowed, s, NEG_INF)
        scores.append(s)
    m = functools.reduce(jnp.maximum, [jnp.max(s, axis=-1, keepdims=True) for s in scores])
    if extra_logit is not None:
        m = jnp.maximum(m, extra_logit)
    den = jnp.exp2(extra_logit - m) if extra_logit is not None else 0.0
    o = 0.0
    for s, (_, v, _, _) in zip(scores, segments):
        e = jnp.exp2(s - m)
        den = den + jnp.sum(e, axis=-1, keepdims=True)
        o = o + _dot(e.astype(BF16), v)
    return o / den


def _causal_split(i, tq, nk, S):
    n_full = nk - S // N_CAUSAL_CLASSES
    t_col = i * tq + _iota((tq, 1), 0)
    tail_causal = (n_full + _iota((1, nk - n_full), 1)) <= t_col
    return n_full, tail_causal


N_CAUSAL_CLASSES = 8


def _for_causal_class(i, tq, S, body):
    step = S // N_CAUSAL_CLASSES
    cls = (i * tq + tq - 1) // step
    for c in range(N_CAUSAL_CLASSES):
        pl.when(cls == c)(functools.partial(body, (c + 1) * step))


def _nsa_kernel(q_ref, sm_ref, kc_ref, vc_ref, ov_ref, ex_ref, ksel_ref, vsel_ref, kwin_ref, vwin_ref, o_ref):
    i = pl.program_id(1)
    tq = q_ref.shape[0]
    S = ksel_ref.shape[0]
    ncmp = kc_ref.shape[0]
    q = q_ref[...]
    sm = sm_ref[...]
    t_col = i * tq + _iota((tq, 1), 0)

    c_end = _iota((1, ncmp), 1) * NSA_CMP_STRIDE + (NSA_CMP_LEN - 1)
    allowed_c = c_end <= t_col
    kc = kc_ref[...]
    vc = vc_ref[...]
    o_cmp = []
    p_cmp = []
    for hh in range(N_HEADS):
        qh = q[:, hh * 64:(hh + 1) * 64]
        s = jnp.where(allowed_c, _dot_t(qh, kc), NEG_INF)
        m = jnp.max(s, axis=-1, keepdims=True)
        e = jnp.where(allowed_c, jnp.exp2(s - m), 0.0)
        den = jnp.sum(e, axis=-1, keepdims=True)
        p = e * jnp.where(den > 0.0, 1.0 / den, 0.0)
        p_cmp.append(p.astype(BF16))
        o_cmp.append(_dot(p_cmp[-1], vc))

    n_blk = S // NSA_SEL_BLOCK
    n_sel = min(NSA_N_SEL, n_blk)

    def unselected_block_bias():
        imp = sum(_dot_t(ov_ref[...], p) for p in p_cmp)
        jb = _iota((n_blk, tq), 0)
        cur = (i * tq + _iota((1, tq), 1)) // NSA_SEL_BLOCK
        forced = (jb == 0) | (jb == cur) | (jb == cur - 1)
        imp = jnp.where(forced, NSA_FORCE_SCORE, imp)
        imp = jnp.where(jb > cur, -NSA_FORCE_SCORE, imp)
        rank = jnp.zeros((n_blk, tq), F32)
        for jp in range(n_blk):
            row = imp[jp:jp + 1, :]
            ahead = (row > imp) | ((row == imp) & (jp < jb))
            rank = rank + jnp.where(ahead, 1.0, 0.0)
        unsel = jnp.transpose(jnp.where(rank < float(n_sel), 0.0, NEG_INF))
        return jnp.concatenate([unsel, jnp.zeros((tq, LANES - n_blk), F32)], axis=1).astype(BF16)

    nband = NSA_WINDOW + tq
    start = pl.multiple_of(jnp.maximum(i * tq - NSA_WINDOW, 0), math.gcd(tq, NSA_WINDOW))
    kwin = kwin_ref[pl.ds(start, nband), :]
    vwin = vwin_ref[pl.ds(start, nband), :]
    s_band = start + _iota((1, nband), 1)
    allowed_win = (s_band <= t_col) & (s_band > t_col - NSA_WINDOW)
    gate = lambda hh, k: sm[:, SM_GATE0 + 3 * hh + k:SM_GATE0 + 3 * hh + k + 1]
    partial_out = []
    for hh in range(N_HEADS):
        o_win = _attend(q[:, hh * 64:(hh + 1) * 64], [(kwin, vwin, None, allowed_win)])
        partial_out.append(gate(hh, 0) * o_cmp[hh] + gate(hh, 2) * o_win)

    def selected(nk):
        n_full, tail_causal = _causal_split(i, tq, nk, S)
        if nk <= n_sel * NSA_SEL_BLOCK:
            part_bias = lambda lo, hi: None
        else:
            bias = _dot(unselected_block_bias(), ex_ref[:, 0:nk])
            part_bias = lambda lo, hi: bias[:, lo:hi]
        segments = [(ksel_ref[n_full:nk, :], vsel_ref[n_full:nk, :], part_bias(n_full, nk), tail_causal)]
        if n_full:
            segments.append((ksel_ref[0:n_full, :], vsel_ref[0:n_full, :], part_bias(0, n_full), None))
        outs = []
        for hh in range(N_HEADS):
            o_sel = _attend(q[:, hh * 64:(hh + 1) * 64], segments)
            outs.append(partial_out[hh] + gate(hh, 1) * o_sel)
        o_ref[...] = jnp.concatenate(outs, axis=1).astype(BF16)

    _for_causal_class(i, tq, S, selected)


def _nsa_call(qa, sm, kcc, vcc, overlap, expand, ksel, vsel, kwin, vwin, B, S):
    T = B * S
    nq = S // TQ
    ncmp = S // NSA_CMP_STRIDE
    tile = lambda b, i: (b * nq + i, 0)
    seq = lambda b, i: (b, 0)
    const = lambda b, i: (0, 0)
    in_specs = [pl.BlockSpec((TQ, 256), tile), pl.BlockSpec((TQ, LANES), tile),
                pl.BlockSpec((ncmp, 64), seq), pl.BlockSpec((ncmp, 64), seq),
                pl.BlockSpec(overlap.shape, const), pl.BlockSpec(expand.shape, const)] \
        + [pl.BlockSpec((S, 64), seq)] * 4
    return pl.pallas_call(
        _nsa_kernel, grid=(B, nq), in_specs=in_specs, out_specs=pl.BlockSpec((TQ, 256), tile),
        out_shape=jax.ShapeDtypeStruct((T, 256), BF16),
        compiler_params=_cparams(("arbitrary", "arbitrary")),
    )(qa, sm, kcc, vcc, overlap, expand, ksel, vsel, kwin, vwin)


def _fox_kernel(q_ref, nbias_ref, k_ref, v_ref, o_ref):
    i = pl.program_id(1)
    tq = q_ref.shape[0]
    S = k_ref.shape[1]
    q = q_ref[...]

    def attend(nk):
        n_full, tail_causal = _causal_split(i, tq, nk, S)
        outs = []
        for hh in range(N_HEADS):
            segments = [(k_ref[hh, n_full:nk, :], v_ref[hh, n_full:nk, :], nbias_ref[0, hh:hh + 1, n_full:nk],
                         tail_causal)]
            if n_full:
                segments.append((k_ref[hh, 0:n_full, :], v_ref[hh, 0:n_full, :], nbias_ref[0, hh:hh + 1, 0:n_full],
                                 None))
            outs.append(_attend(q[:, hh * 64:(hh + 1) * 64], segments))
        o_ref[...] = jnp.concatenate(outs, axis=1).astype(BF16)

    _for_causal_class(i, tq, S, attend)


def _fox_call(qb, nbias, kb, vb, B, S):
    T = B * S
    nq = S // TQ
    tile = lambda b, i: (b * nq + i, 0)
    in_specs = [pl.BlockSpec((TQ, 256), tile),
                pl.BlockSpec((1, N_HEADS, S), lambda b, i: (b, 0, 0)),
                pl.BlockSpec((N_HEADS, S, 64), lambda b, i: (0, b, 0)),
                pl.BlockSpec((N_HEADS, S, 64), lambda b, i: (0, b, 0))]
    return pl.pallas_call(
        _fox_kernel, grid=(B, nq), in_specs=in_specs, out_specs=pl.BlockSpec((TQ, 256), tile),
        out_shape=jax.ShapeDtypeStruct((T, 256), BF16),
        compiler_params=_cparams(("arbitrary", "arbitrary")),
    )(qb, nbias, kb, vb)


def _dsa_kernel(q_ref, qi_ref, sm_ref, ki_ref, cnt_ref, k_ref, v_ref, o_ref, score_ref):
    i = pl.program_id(1)
    tq = q_ref.shape[0]
    S = ki_ref.shape[0]
    sm = sm_ref[...]
    qi = qi_ref[...]
    t_col = i * tq + _iota((tq, 1), 0)
    top_k = min(DSA_TOPK_MAX, S // 4)
    k_eff = jnp.minimum(t_col + 1, top_k).astype(F32)
    q = q_ref[...]

    def attend(nk, bias, allowed):
        outs = [_attend(q[:, hh * 64:(hh + 1) * 64], [(k_ref[hh, 0:nk, :], v_ref[hh, 0:nk, :], bias, allowed)])
                for hh in range(N_HEADS)]
        o_ref[...] = jnp.concatenate(outs, axis=1).astype(BF16)

    def float_of_ordered_bits(u):
        k = u ^ INT_MIN
        return lax.bitcast_convert_type(jnp.where(k >= 0, k, k ^ 0x7FFFFFFF), F32)

    def select_and_attend(nk):
        causal = _iota((1, nk), 1) <= t_col
        score = jnp.zeros((tq, nk), F32)
        for hh in range(N_HEADS):
            lg = _dot_t(qi[:, hh * 64:(hh + 1) * 64], ki_ref[0:nk, :])
            score = score + sm[:, SM_W0 + hh:SM_W0 + hh + 1] * jnp.maximum(lg, 0.0)
        score_ref[:, 0:nk] = jnp.where(causal, score, -jnp.inf)

        def search(it, u):
            cand = u | lax.shift_left(jnp.int32(1), 31 - it)
            thr = float_of_ordered_bits(cand)
            cnt = jnp.sum(jnp.where(score_ref[:, 0:nk] >= thr, 1.0, 0.0), axis=-1, keepdims=True)
            return jnp.where(cnt >= k_eff, cand, u)

        u = lax.fori_loop(0, 32, search, jnp.zeros((tq, 1), I32))
        thr = float_of_ordered_bits(u)
        score = score_ref[:, 0:nk]
        gt = score > thr
        eq = score == thr
        need = k_eff - jnp.sum(jnp.where(gt, 1.0, 0.0), axis=-1, keepdims=True)
        eqb = jnp.where(eq, 1.0, 0.0).astype(BF16)
        run = jnp.zeros((tq, LANES), F32)
        bias_chunks = []
        for c in range(nk // LANES):
            sl = slice(c * LANES, (c + 1) * LANES)
            r = _dot(eqb[:, sl], cnt_ref[...])
            prefix = r[:, :LANES] + run
            run = run + r[:, LANES:]
            keep = gt[:, sl] | (eq[:, sl] & (prefix < need))
            bias_chunks.append(jnp.where(keep, 0.0, NEG_INF))
        attend(nk, jnp.concatenate(bias_chunks, axis=1), None)

    def body(nk):
        if nk * N_CAUSAL_CLASSES == S:
            all_kept = (i + 1) * tq <= top_k
            pl.when(all_kept)(lambda: attend(nk, None, _iota((1, nk), 1) <= t_col))
            pl.when(jnp.logical_not(all_kept))(lambda: select_and_attend(nk))
        else:
            select_and_attend(nk)

    _for_causal_class(i, tq, S, body)


def _dsa_call(qc, qi, sm, ki, cntmat, kc, vc, B, S):
    T = B * S
    nq = S // TQ
    tile = lambda b, i: (b * nq + i, 0)
    in_specs = [pl.BlockSpec((TQ, 256), tile), pl.BlockSpec((TQ, 256), tile), pl.BlockSpec((TQ, LANES), tile),
                pl.BlockSpec((S, 64), lambda b, i: (b, 0)),
                pl.BlockSpec(cntmat.shape, lambda b, i: (0, 0)),
                pl.BlockSpec((N_HEADS, S, 64), lambda b, i: (0, b, 0)),
                pl.BlockSpec((N_HEADS, S, 64), lambda b, i: (0, b, 0))]
    return pl.pallas_call(
        _dsa_kernel, grid=(B, nq), in_specs=in_specs, out_specs=pl.BlockSpec((TQ, 256), tile),
        out_shape=jax.ShapeDtypeStruct((T, 256), BF16),
        scratch_shapes=[pltpu.VMEM((TQ, S), F32)],
        compiler_params=_cparams(("arbitrary", "arbitrary")),
    )(qc, qi, sm, ki, cntmat, kc, vc)


def _swa_kernel(q_ref, sink_ref, k_ref, v_ref, o_ref):
    i = pl.program_id(1)
    tq = q_ref.shape[0]
    q = q_ref[...]
    t_col = i * tq + _iota((tq, 1), 0)
    nband = SWA_WINDOW + tq
    start = pl.multiple_of(jnp.maximum(i * tq - SWA_WINDOW, 0), math.gcd(tq, SWA_WINDOW))
    s_band = start + _iota((1, nband), 1)
    allowed = (s_band <= t_col) & (s_band > t_col - SWA_WINDOW)
    group = N_HEADS // SWA_KV_HEADS
    outs = []
    for hh in range(N_HEADS):
        k = k_ref[hh // group, pl.ds(start, nband), :]
        v = v_ref[hh // group, pl.ds(start, nband), :]
        sink = jnp.zeros((tq, 1), F32) + LOG2E * sink_ref[0:1, hh:hh + 1]
        outs.append(_attend(q[:, hh * 64:(hh + 1) * 64], [(k, v, None, allowed)], extra_logit=sink))
    o_ref[...] = jnp.concatenate(outs, axis=1).astype(BF16)


def _swa_call(qd, sinks, kd, vd, B, S):
    T = B * S
    nq = S // TQ
    tile = lambda b, i: (b * nq + i, 0)
    in_specs = [pl.BlockSpec((TQ, 256), tile), pl.BlockSpec((1, LANES), lambda b, i: (0, 0)),
                pl.BlockSpec((SWA_KV_HEADS, S, 64), lambda b, i: (0, b, 0)),
                pl.BlockSpec((SWA_KV_HEADS, S, 64), lambda b, i: (0, b, 0))]
    return pl.pallas_call(
        _swa_kernel, grid=(B, nq), in_specs=in_specs, out_specs=pl.BlockSpec((TQ, 256), tile),
        out_shape=jax.ShapeDtypeStruct((T, 256), BF16),
        compiler_params=_cparams(("arbitrary", "arbitrary")),
    )(qd, sinks, kd, vd)


RT_E1, RT_E2, RT_W1, RT_W2, RT_R1, RT_R2 = 0, 1, 2, 3, 4, 5
ROUTER_E0 = MOE_GROUPS


def _merge_kernel(h_ref, oa_ref, ob_ref, oc_ref, od_ref, gain_ref, wg_ref, wb_ref, wo_ref,
                  fgain_ref, wr_ref, br_ref, h2_ref, xn_ref, rt_ref, cnt_ref, base_ref):
    step = pl.program_id(0)
    tm = h_ref.shape[0]
    logit_chunks = []
    for rows in (slice(c * MERGE_CHUNK, (c + 1) * MERGE_CHUNK) for c in range(tm // MERGE_CHUNK)):
        h = h_ref[rows, :]
        hn = h * lax.rsqrt(jnp.mean(h * h, axis=-1, keepdims=True) + NORM_EPS) * gain_ref[...]
        hb = hn.astype(BF16)
        merged = jnp.zeros_like(h)
        for i, o_ref in enumerate((oa_ref, ob_ref, oc_ref, od_ref)):
            gate = 1.0 / (1.0 + jnp.exp(-_dot(hb, wg_ref[i])))
            merged = merged + gate * _dot(o_ref[rows, :], wb_ref[i])
        h2 = h + _dot(merged.astype(BF16), wo_ref[...])
        h2_ref[rows, :] = h2
        xn = h2 * lax.rsqrt(jnp.mean(h2 * h2, axis=-1, keepdims=True) + NORM_EPS) * fgain_ref[...]
        xn_ref[rows, :] = xn
        logit_chunks.append(jnp.dot(xn, wr_ref[...], preferred_element_type=F32, precision=lax.Precision.HIGHEST))
    logits = jnp.concatenate(logit_chunks, axis=0) + br_ref[...]

    lane = _iota((tm, LANES), 1)
    lanef = lane.astype(F32)
    big = float(LANES)
    is_g = lane < MOE_GROUPS
    gl = jnp.where(is_g, logits, NEG_INF)
    gmax = jnp.max(gl, axis=-1, keepdims=True)
    grp = jnp.min(jnp.where(is_g & (gl == gmax), lanef, big), axis=-1, keepdims=True)
    p_grp = 1.0 / jnp.sum(jnp.where(is_g, jnp.exp(gl - gmax), 0.0), axis=-1, keepdims=True)
    lo = ROUTER_E0 + grp * MOE_EXPERTS_PER_GROUP
    in_grp = (lanef >= lo) & (lanef < lo + MOE_EXPERTS_PER_GROUP)
    el = jnp.where(in_grp, logits, NEG_INF)
    v1 = jnp.max(el, axis=-1, keepdims=True)
    l1 = jnp.min(jnp.where(in_grp & (el == v1), lanef, big), axis=-1, keepdims=True)
    el2 = jnp.where(lanef == l1, NEG_INF, el)
    v2 = jnp.max(el2, axis=-1, keepdims=True)
    l2 = jnp.min(jnp.where(in_grp & (lanef != l1) & (el2 == v2), lanef, big), axis=-1, keepdims=True)
    e21 = jnp.exp(v2 - v1)
    w1 = p_grp / (1.0 + e21)
    w2 = p_grp * e21 / (1.0 + e21)

    oh1 = jnp.where(lanef == l1, 1.0, 0.0)
    oh2 = jnp.where(lanef == l2, 1.0, 0.0)

    @pl.when(step == 0)
    def _():
        base_ref[...] = jnp.zeros_like(base_ref)

    both = oh1 + oh2
    strict = (_iota((tm, tm), 0) > _iota((tm, tm), 1)).astype(BF16)
    before = _dot(strict, both.astype(BF16)) + base_ref[0:1, :]
    r1 = jnp.sum(oh1 * before, axis=-1, keepdims=True)
    r2 = jnp.sum(oh2 * before, axis=-1, keepdims=True)
    total = base_ref[0:1, :] + jnp.sum(both, axis=0, keepdims=True)
    base_ref[0:1, :] = total
    cnt_ref[...] = jnp.broadcast_to(total, cnt_ref.shape)

    rt = jnp.zeros((tm, LANES), F32)
    for ln, val in ((RT_E1, l1 - ROUTER_E0), (RT_E2, l2 - ROUTER_E0), (RT_W1, w1), (RT_W2, w2), (RT_R1, r1), (RT_R2, r2)):
        rt = jnp.where(lane == ln, val, rt)
    rt_ref[...] = rt


def _merge_call(h, oa, ob, oc, od, gain, wg, wb, wo, fgain, wr, br):
    T, D = h.shape
    tm = TM_MERGE
    tile = lambda i: (i, 0)
    c2 = lambda i: (0, 0)
    c3 = lambda i: (0, 0, 0)
    once = pl.Buffered(1)
    in_specs = [pl.BlockSpec((tm, D), tile)] + [pl.BlockSpec((tm, 256), tile)] * 4 + [
        pl.BlockSpec((1, D), c2),
        pl.BlockSpec(wg.shape, c3, pipeline_mode=once),
        pl.BlockSpec(wb.shape, c3, pipeline_mode=once),
        pl.BlockSpec(wo.shape, c2, pipeline_mode=once),
        pl.BlockSpec((1, D), c2),
        pl.BlockSpec(wr.shape, c2),
        pl.BlockSpec((1, LANES), c2),
    ]
    out_shape = [jax.ShapeDtypeStruct((T, D), F32), jax.ShapeDtypeStruct((T, D), F32),
                 jax.ShapeDtypeStruct((T, LANES), F32), jax.ShapeDtypeStruct((8, LANES), F32)]
    out_specs = [pl.BlockSpec((tm, D), tile), pl.BlockSpec((tm, D), tile), pl.BlockSpec((tm, LANES), tile),
                 pl.BlockSpec((8, LANES), c2)]
    return pl.pallas_call(
        _merge_kernel, grid=(T // tm,), in_specs=in_specs, out_specs=out_specs, out_shape=out_shape,
        scratch_shapes=[pltpu.VMEM((8, LANES), F32)],
        compiler_params=_cparams(("arbitrary",)),
    )(h, oa, ob, oc, od, gain, wg, wb, wo, fgain, wr, br)


def _row_copy(src, src_row, dst, dst_row, sem):
    return pltpu.make_async_copy(src.at[pl.ds(src_row, 1)], dst.at[pl.ds(dst_row, 1)], sem)


def _dest_kernel(rt_ref, ps_ref, d_ref):
    rt = rt_ref[...]
    lane = _iota(rt.shape, 1)
    lanef = lane.astype(F32)
    ps = ps_ref[...]

    def dest(e_lane, r_lane):
        start = jnp.sum(jnp.where(lanef == rt[:, e_lane:e_lane + 1], ps, 0.0), axis=-1, keepdims=True)
        return start + rt[:, r_lane:r_lane + 1]

    d = jnp.where(lane == 0, dest(RT_E1, RT_R1), jnp.where(lane == 1, dest(RT_E2, RT_R2), 0.0))
    d_ref[...] = d.astype(I32)


def _dest_call(rt, pstart_row):
    T = rt.shape[0]
    tm = TM_MERGE
    return pl.pallas_call(
        _dest_kernel, grid=(T // tm,),
        in_specs=[pl.BlockSpec((tm, LANES), lambda i: (i, 0)), pl.BlockSpec((1, LANES), lambda i: (0, 0))],
        out_specs=pl.BlockSpec((tm, LANES), lambda i: (i, 0)), out_shape=jax.ShapeDtypeStruct((T, LANES), I32),
        compiler_params=_cparams(("arbitrary",)),
    )(rt, pstart_row)


DMA_ISSUE_UNROLL = 8


def _dispatch_kernel(d1_ref, d2_ref, xn_ref, xs_in_ref, xs_ref, sem):
    del xs_in_ref
    base = pl.program_id(0) * TD_DISPATCH

    def copies(r):
        t = base + r
        return (_row_copy(xn_ref, r, xs_ref, d1_ref[t], sem), _row_copy(xn_ref, r, xs_ref, d2_ref[t], sem))

    def start(r, c):
        for cp in copies(r):
            cp.start()
        return c

    def wait(r, c):
        for cp in copies(r):
            cp.wait()
        return c

    lax.fori_loop(0, TD_DISPATCH, start, 0, unroll=DMA_ISSUE_UNROLL)
    lax.fori_loop(0, TD_DISPATCH, wait, 0)


def _dispatch_call(d1, d2, xn, xs_zero):
    T, D = xn.shape
    any_spec = pl.BlockSpec(memory_space=pl.ANY)
    grid_spec = pltpu.PrefetchScalarGridSpec(
        num_scalar_prefetch=2, grid=(T // TD_DISPATCH,),
        in_specs=[pl.BlockSpec((TD_DISPATCH, D), lambda i, *_: (i, 0)), any_spec], out_specs=any_spec,
        scratch_shapes=[pltpu.SemaphoreType.DMA(())])
    return pl.pallas_call(
        _dispatch_kernel, grid_spec=grid_spec, out_shape=jax.ShapeDtypeStruct(xs_zero.shape, xs_zero.dtype),
        input_output_aliases={3: 0},
        compiler_params=pltpu.CompilerParams(dimension_semantics=("arbitrary",), has_side_effects=True,
                                             vmem_limit_bytes=VMEM_LIMIT),
    )(d1, d2, xn, xs_zero)


def _expert_kernel(be_ref, used_ref, x_ref, wg_ref, wu_ref, wd_ref, y_ref):
    i = pl.program_id(0)

    @pl.when(i < used_ref[0])
    def _():
        x = x_ref[...].astype(BF16)
        g = _dot(x, wg_ref[0, 0].astype(BF16))
        u = _dot(x, wu_ref[0, 0].astype(BF16))
        act = g / (1.0 + jnp.exp(-g)) * u
        y_ref[...] = _dot(act.astype(BF16), wd_ref[0, 0].astype(BF16))

    @pl.when(i >= used_ref[0])
    def _():
        y_ref[...] = jnp.zeros_like(y_ref)


def _expert_call(blk_expert, used, xs, wg, wu, wd, layer):
    rows, D = xs.shape
    rb = MOE_ROW_BLOCK
    DE = wg.shape[3]
    xmap = lambda i, be, used: (jnp.maximum(jnp.minimum(i, used[0] - 1), 0), 0)
    wmap = lambda i, be, used: (layer, be[i], 0, 0)
    grid_spec = pltpu.PrefetchScalarGridSpec(
        num_scalar_prefetch=2, grid=(rows // rb,),
        in_specs=[pl.BlockSpec((rb, D), xmap), pl.BlockSpec((1, 1, D, DE), wmap), pl.BlockSpec((1, 1, D, DE), wmap),
                  pl.BlockSpec((1, 1, DE, D), wmap)],
        out_specs=pl.BlockSpec((rb, D), lambda i, be, used: (i, 0)))
    return pl.pallas_call(
        _expert_kernel, grid_spec=grid_spec, out_shape=jax.ShapeDtypeStruct((rows, D), F32),
        compiler_params=_cparams(("arbitrary",)),
    )(blk_expert, used, xs, wg, wu, wd)


def _combine_kernel(final_norm, d1_ref, d2_ref, h_ref, rt_ref, gain_ref, ys_ref, o_ref, buf_ref, sems):
    step = pl.program_id(0)
    n_steps = pl.num_programs(0)

    def copies(tile, slot, r):
        t = tile * TC_COMBINE + r
        return (_row_copy(ys_ref, d1_ref[t], buf_ref.at[slot, 0], r, sems.at[slot]),
                _row_copy(ys_ref, d2_ref[t], buf_ref.at[slot, 1], r, sems.at[slot]))

    def start_tile(tile, slot):
        def body(r, c):
            for cp in copies(tile, slot, r):
                cp.start()
            return c
        lax.fori_loop(0, TC_COMBINE, body, 0, unroll=DMA_ISSUE_UNROLL)

    def wait_tile(tile, slot):
        def body(r, c):
            for cp in copies(tile, slot, r):
                cp.wait()
            return c
        lax.fori_loop(0, TC_COMBINE, body, 0)

    slot = step % 2

    @pl.when(step == 0)
    def _():
        start_tile(0, 0)

    @pl.when(step + 1 < n_steps)
    def _():
        start_tile(step + 1, 1 - slot)

    wait_tile(step, slot)
    rt = rt_ref[...]
    out = h_ref[...] + rt[:, RT_W1:RT_W1 + 1] * buf_ref[slot, 0] + rt[:, RT_W2:RT_W2 + 1] * buf_ref[slot, 1]
    if final_norm:
        out = out * lax.rsqrt(jnp.mean(out * out, axis=-1, keepdims=True) + NORM_EPS) * gain_ref[...]
    o_ref[...] = out


def _combine_call(d1, d2, h, rt, gain, ys, final_norm):
    T, D = h.shape
    tc = TC_COMBINE
    tile = lambda i, *_: (i, 0)
    grid_spec = pltpu.PrefetchScalarGridSpec(
        num_scalar_prefetch=2, grid=(T // tc,),
        in_specs=[pl.BlockSpec((tc, D), tile), pl.BlockSpec((tc, LANES), tile),
                  pl.BlockSpec((1, D), lambda i, *_: (0, 0)), pl.BlockSpec(memory_space=pl.ANY)],
        out_specs=pl.BlockSpec((tc, D), tile),
        scratch_shapes=[pltpu.VMEM((2, 2, tc, D), F32), pltpu.SemaphoreType.DMA((2,))])
    return pl.pallas_call(
        functools.partial(_combine_kernel, final_norm), grid_spec=grid_spec,
        out_shape=jax.ShapeDtypeStruct((T, D), F32),
        compiler_params=_cparams(("arbitrary",)),
    )(d1, d2, h, rt, gain, ys)


def _in_proj_columns():
    widths = (('q_a', 256), ('k_cmp', 64), ('v_cmp', 64), ('k_sel', 64), ('v_sel', 64), ('k_win', 64), ('v_win', 64),
              ('g_a', 12), ('q_b', 256), ('k_b', 256), ('v_b', 256), ('f_b', 4), ('q_c', 256), ('ckv_c', 128),
              ('qi_c', 256), ('ki_c', 64), ('wi_c', 4), ('q_d', 256), ('k_d', 128), ('v_d', 128))
    off, cols = 0, {}
    for name, w in widths:
        cols[name] = np.arange(off, off + w)
        off += w
    cat = lambda names: np.concatenate([cols[n] for n in names])
    rope = cat(('q_a', 'q_c', 'qi_c', 'q_d', 'k_d', 'k_cmp', 'k_sel', 'k_win', 'ki_c'))
    plain = cat(('q_b', 'k_b', 'v_b', 'v_d', 'ckv_c', 'v_cmp', 'v_sel', 'v_win'))
    small = cat(('g_a', 'f_b', 'wi_c'))
    assert rope.size == N_ROPE_COLS and plain.size == N_PLAIN_COLS
    return np.concatenate([rope, plain, small]), N_SMALL_COLS - small.size


def _static_tables(S):
    n_cmp_pad = S // NSA_CMP_STRIDE
    n_blk = S // NSA_SEL_BLOCK
    c0 = np.arange(n_cmp_pad) * NSA_CMP_STRIDE
    b0 = np.arange(n_blk) * NSA_SEL_BLOCK
    ov = ((c0[:, None] < b0[None, :] + NSA_SEL_BLOCK) & (c0[:, None] + NSA_CMP_LEN > b0[None, :])).astype(np.float32)
    overlap = np.ascontiguousarray(ov.T)
    expand = np.zeros((LANES, S), np.float32)
    expand[np.arange(S) // NSA_SEL_BLOCK, np.arange(S)] = 1.0
    strict_upper = (np.arange(LANES)[:, None] < np.arange(LANES)[None, :]).astype(np.float32)
    cntmat = np.concatenate([strict_upper, np.ones((LANES, LANES), np.float32)], axis=1)
    return jnp.asarray(overlap, BF16), jnp.asarray(expand, BF16), jnp.asarray(cntmat, BF16)


def kernel(x, positions, attn_norm, w_in, nsa_pe_k, nsa_w1_k, nsa_w2_k, nsa_pe_v, nsa_w1_v, nsa_w2_v, fox_forget_bias, dsa_kv_norm, dsa_w_ukv, swa_sinks, w_branch, w_gate, w_out, ffn_norm, moe_w_group, moe_b_group, moe_w_expert, moe_b_expert, moe_w_gate, moe_w_up, moe_w_down, final_norm):
    B, S, D = x.shape
    T = B * S
    depth = w_in.shape[0]
    perm, n_pad = _in_proj_columns()
    overlap, expand, cntmat = _static_tables(S)
    half = HEAD_DIM // 2
    inv_freq = ROPE_THETA ** (-jnp.arange(half, dtype=F32) / half)
    invf = jnp.tile(inv_freq, LANES // half).reshape(1, LANES)
    pos = positions.reshape(T, 1).astype(I32)
    small_pad = lambda v, off: jnp.zeros((1, LANES), F32).at[0, off:off + v.shape[0]].set(v.astype(F32))

    n_rows = -(-(T * 2 + MOE_N_EXPERTS * (MOE_ROW_BLOCK - 1)) // MOE_ROW_BLOCK) * MOE_ROW_BLOCK
    h = x.reshape(T, D)
    for l in range(depth):
        w = jnp.pad(w_in[l][:, perm], ((0, 0), (0, n_pad))).astype(BF16)
        (qa, qc, qi, qd, kd, kcmp, ksel, kwin, ki, qb, kb, vb, vd, vcmp, vsel, vwin, kc, vc, sm) = _proj_call(
            h, pos, invf, attn_norm[l].reshape(1, D), w, small_pad(fox_forget_bias[l], SM_F0),
            dsa_kv_norm[l].reshape(1, DSA_KV_RANK), dsa_w_ukv[l].astype(BF16), B, S)

        cw = NSA_CMP_STRIDE * HEAD_DIM
        pe_rows = lambda pe: jnp.broadcast_to(pe.reshape(1, 2 * cw), (8, 2 * cw)).astype(BF16)
        kcc, vcc = _cmp_call(kcmp.reshape(T // NSA_CMP_STRIDE, cw), vcmp.reshape(T // NSA_CMP_STRIDE, cw),
                             pe_rows(nsa_pe_k[l]), nsa_w1_k[l].astype(BF16), nsa_w2_k[l].astype(BF16),
                             pe_rows(nsa_pe_v[l]), nsa_w1_v[l].astype(BF16), nsa_w2_v[l].astype(BF16), B, S)
        o_a = _nsa_call(qa, sm, kcc, vcc, overlap, expand, ksel, vsel, kwin, vwin, B, S)

        key_bias = sm[:, SM_F0:SM_F0 + N_HEADS].reshape(B, S, N_HEADS).transpose(0, 2, 1)
        o_b = _fox_call(qb, key_bias, kb, vb, B, S)
        o_c = _dsa_call(qc, qi, sm, ki, cntmat, kc, vc, B, S)
        o_d = _swa_call(qd, small_pad(swa_sinks[l], 0), kd, vd, B, S)

        wr = jnp.zeros((D, LANES), F32).at[:, :MOE_GROUPS].set(moe_w_group[l]) \
            .at[:, ROUTER_E0:ROUTER_E0 + MOE_N_EXPERTS].set(moe_w_expert[l])
        br = jnp.zeros((1, LANES), F32).at[0, :MOE_GROUPS].set(moe_b_group[l]) \
            .at[0, ROUTER_E0:ROUTER_E0 + MOE_N_EXPERTS].set(moe_b_expert[l])
        h2, xn, rt, cnt = _merge_call(h, o_a, o_b, o_c, o_d, attn_norm[l].reshape(1, D), w_gate[l].astype(BF16),
                                      w_branch[l].astype(BF16), w_out[l].astype(BF16), ffn_norm[l].reshape(1, D),
                                      wr, br)

        counts = cnt[0, ROUTER_E0:ROUTER_E0 + MOE_N_EXPERTS].astype(I32)
        padded = (counts + MOE_ROW_BLOCK - 1) // MOE_ROW_BLOCK * MOE_ROW_BLOCK
        p_end = jnp.cumsum(padded)
        p_start = p_end - padded
        n_blocks = n_rows // MOE_ROW_BLOCK
        blk_first_row = jnp.arange(n_blocks, dtype=I32) * MOE_ROW_BLOCK
        blk_expert = jnp.minimum(jnp.sum((p_end[None, :] <= blk_first_row[:, None]).astype(I32), axis=1),
                                 MOE_N_EXPERTS - 1).astype(I32)
        used = (p_end[-1:] // MOE_ROW_BLOCK).astype(I32)
        dest = _dest_call(rt, small_pad(p_start, 0))
        d1, d2 = dest[:, 0], dest[:, 1]

        xs = _dispatch_call(d1, d2, xn, jnp.zeros((n_rows, D), F32))
        ys = _expert_call(blk_expert, used, xs, moe_w_gate, moe_w_up, moe_w_down, l)
        last = l == depth - 1
        h = _combine_call(d1, d2, h2, rt, final_norm.reshape(1, D), ys, last)
    return h.reshape(B, S, D)
```

```python
import functools
import math

import numpy as np
import jax
import jax.numpy as jnp
from jax import lax
from jax.experimental import pallas as pl
from jax.experimental.pallas import tpu as pltpu

F32 = jnp.float32
BF16 = jnp.bfloat16
I32 = jnp.int32

HEAD_DIM = 64
N_HEADS = 4
MIX_WIDTH = N_HEADS * HEAD_DIM
ROPE_THETA = 10000.0
NORM_EPS = 1e-6
NEG_INF = -1e30
INT_MIN = -2 ** 31
LOG2E = float(np.log2(np.e))

NSA_CMP_LEN = 32
NSA_CMP_STRIDE = 16
NSA_CMP_HIDDEN = 256
NSA_SEL_BLOCK = 64
NSA_N_SEL = 16
NSA_WINDOW = 512
NSA_FORCE_SCORE = 1e4
DSA_KV_RANK = 128
DSA_TOPK_MAX = 256
SWA_WINDOW = 128
SWA_KV_HEADS = 2
MOE_GROUPS = 8
MOE_EXPERTS_PER_GROUP = 8
MOE_N_EXPERTS = 64
MOE_D_EXPERT = 256

LANES = 128
VMEM_LIMIT = 48 * 1024 * 1024

N_ROPE_COLS = 1408
N_PLAIN_COLS = 1216
N_SMALL_COLS = 128
N_PROJ_COLS = N_ROPE_COLS + N_PLAIN_COLS + N_SMALL_COLS
SM_GATE0, SM_F0, SM_W0 = 0, 12, 16

TM_PROJ = 512
PROJ_CHUNK = 128
TQ = 256
TM_MERGE = 512
MERGE_CHUNK = 128
MOE_ROW_BLOCK = 256
TD_DISPATCH = 256
TC_COMBINE = 256


def _cparams(sem):
    return pltpu.CompilerParams(dimension_semantics=sem, vmem_limit_bytes=VMEM_LIMIT)


def _dot(a, b):
    return jnp.dot(a, b, preferred_element_type=F32)


def _dot_t(a, b):
    return lax.dot_general(a, b, (((1,), (1,)), ((), ())), preferred_element_type=F32)


def _iota(shape, dim):
    return lax.broadcasted_iota(I32, shape, dim)


def _rope_slab(x, cos, sin_signed, first_half):
    rot = jnp.where(first_half, pltpu.roll(x, 96, 1), pltpu.roll(x, 32, 1))
    return x * cos + rot * sin_signed


def _proj_kernel(h_ref, pos_ref, invf_ref, gain_ref, w_ref, fbias_ref, kvn_ref, wukv_ref,
                 qa_ref, qc_ref, qi_ref, qd_ref, kd_ref, kcmp_ref, ksel_ref, kwin_ref, ki_ref,
                 qb_ref, kb_ref, vb_ref, vd_ref, vcmp_ref, vsel_ref, vwin_ref, kc_ref, vc_ref,
                 sm_ref, carry_ref):
    j = pl.program_id(1)
    tm = h_ref.shape[0]
    tc = PROJ_CHUNK

    @pl.when(j == 0)
    def _():
        carry_ref[...] = jnp.zeros_like(carry_ref)

    lane = _iota((tc, LANES), 1)
    first_half = (lane % HEAD_DIM) < (HEAD_DIM // 2)
    is_f = (lane >= SM_F0) & (lane < SM_W0)
    tri = (_iota((tc, tc), 0) >= _iota((tc, tc), 1)).astype(F32)
    scale = HEAD_DIM ** -0.5
    softmax_scale = scale * LOG2E

    for rows in (slice(c * tc, (c + 1) * tc) for c in range(tm // tc)):
        h = h_ref[rows, :]
        hn = h * lax.rsqrt(jnp.mean(h * h, axis=-1, keepdims=True) + NORM_EPS) * gain_ref[...]
        hb = hn.astype(BF16)

        ang = pos_ref[rows, :].astype(F32) * invf_ref[...]
        cos = jnp.cos(ang)
        sin = jnp.sin(ang)
        sin_signed = jnp.where(first_half, -sin, sin)
        rope = lambda x: _rope_slab(x, cos, sin_signed, first_half)

        zr = _dot(hb, w_ref[:, 0:N_ROPE_COLS])
        slab = lambda k: rope(zr[:, k * LANES:(k + 1) * LANES])
        for q_ref, k0, sc in ((qa_ref, 0, softmax_scale), (qc_ref, 2, softmax_scale), (qi_ref, 4, scale),
                              (qd_ref, 6, softmax_scale)):
            q_ref[rows, :] = (jnp.concatenate([slab(k0), slab(k0 + 1)], axis=1) * sc).astype(BF16)
        s8 = slab(8).astype(BF16)
        kd_ref[0, rows, :] = s8[:, :HEAD_DIM]
        kd_ref[1, rows, :] = s8[:, HEAD_DIM:]
        s9 = slab(9).astype(BF16)
        kcmp_ref[rows, :] = s9[:, :HEAD_DIM]
        ksel_ref[rows, :] = s9[:, HEAD_DIM:]
        s10 = slab(10).astype(BF16)
        kwin_ref[rows, :] = s10[:, :HEAD_DIM]
        ki_ref[rows, :] = s10[:, HEAD_DIM:]

        zp = _dot(hb, w_ref[:, N_ROPE_COLS:N_ROPE_COLS + N_PLAIN_COLS])
        qb_ref[rows, :] = (zp[:, 0:256] * softmax_scale).astype(BF16)
        for hh in range(N_HEADS):
            kb_ref[hh, rows, :] = zp[:, 256 + hh * 64:256 + (hh + 1) * 64].astype(BF16)
            vb_ref[hh, rows, :] = zp[:, 512 + hh * 64:512 + (hh + 1) * 64].astype(BF16)
        vd_ref[0, rows, :] = zp[:, 768:832].astype(BF16)
        vd_ref[1, rows, :] = zp[:, 832:896].astype(BF16)
        ckv = zp[:, 896:1024]
        vcmp_ref[rows, :] = zp[:, 1024:1088].astype(BF16)
        vsel_ref[rows, :] = zp[:, 1088:1152].astype(BF16)
        vwin_ref[rows, :] = zp[:, 1152:1216].astype(BF16)

        ckvn = ckv * lax.rsqrt(jnp.mean(ckv * ckv, axis=-1, keepdims=True) + NORM_EPS) * kvn_ref[...]
        kvc = _dot(ckvn.astype(BF16), wukv_ref[...])
        for k in range(2):
            kk = rope(kvc[:, k * LANES:(k + 1) * LANES]).astype(BF16)
            kc_ref[2 * k, rows, :] = kk[:, :HEAD_DIM]
            kc_ref[2 * k + 1, rows, :] = kk[:, HEAD_DIM:]
        for hh in range(N_HEADS):
            vc_ref[hh, rows, :] = kvc[:, 256 + hh * 64:256 + (hh + 1) * 64].astype(BF16)

        zs = _dot(hb, w_ref[:, N_ROPE_COLS + N_PLAIN_COLS:N_PROJ_COLS])
        sig = 1.0 / (1.0 + jnp.exp(-zs))
        xf = zs + fbias_ref[...]
        logf = jnp.minimum(xf, 0.0) - jnp.log(1.0 + jnp.exp(-jnp.abs(xf)))
        logf = jnp.where(is_f, logf, 0.0)
        cum = jnp.dot(tri, logf, preferred_element_type=F32, precision=lax.Precision.HIGHEST)
        cum = cum + carry_ref[0:1, :]
        carry_ref[0:1, :] = cum[tc - 1:tc, :]
        sm_ref[rows, :] = jnp.where(lane < SM_F0, sig,
                                    jnp.where(is_f, -LOG2E * cum, jnp.where(lane < SM_W0 + 4, 0.5 * zs, 0.0)))


def _proj_call(h, pos, invf, gain, w, fbias, kvn, wukv, B, S):
    T = B * S
    tm = TM_PROJ
    nj = S // tm
    row = lambda b, j: (b * nj + j, 0)
    hrow = lambda b, j: (0, b * nj + j, 0)
    const2 = lambda b, j: (0, 0)
    tok = lambda w_, dt: jax.ShapeDtypeStruct((T, w_), dt)
    hm = lambda n: jax.ShapeDtypeStruct((n, T, HEAD_DIM), BF16)
    out_shape = [tok(256, BF16)] * 4 + [hm(2)] + [tok(64, BF16)] * 4 + [tok(256, BF16), hm(4), hm(4), hm(2)] \
        + [tok(64, BF16)] * 3 + [hm(4), hm(4), tok(LANES, F32)]
    spec_tok = lambda w_: pl.BlockSpec((tm, w_), row)
    spec_hm = lambda n: pl.BlockSpec((n, tm, HEAD_DIM), hrow)
    out_specs = [spec_tok(256)] * 4 + [spec_hm(2)] + [spec_tok(64)] * 4 + [spec_tok(256), spec_hm(4), spec_hm(4), spec_hm(2)] \
        + [spec_tok(64)] * 3 + [spec_hm(4), spec_hm(4), spec_tok(LANES)]
    in_specs = [
        pl.BlockSpec((tm, h.shape[1]), row),
        pl.BlockSpec((tm, 1), row),
        pl.BlockSpec((1, LANES), const2),
        pl.BlockSpec((1, h.shape[1]), const2),
        pl.BlockSpec(w.shape, const2),
        pl.BlockSpec((1, LANES), const2),
        pl.BlockSpec((1, DSA_KV_RANK), const2),
        pl.BlockSpec(wukv.shape, const2),
    ]
    return pl.pallas_call(
        _proj_kernel, grid=(B, nj), in_specs=in_specs, out_specs=out_specs, out_shape=out_shape,
        scratch_shapes=[pltpu.VMEM((8, LANES), F32)],
        compiler_params=_cparams(("arbitrary", "arbitrary")),
    )(h, pos, invf, gain, w, fbias, kvn, wukv)


def _gelu_tanh(x):
    return 0.5 * x * (1.0 + jnp.tanh(np.sqrt(2.0 / np.pi).astype(np.float32) * (x + 0.044715 * (x * x * x))))


def _cmp_kernel(ck_ref, cv_ref, pek_ref, w1k_ref, w2k_ref, pev_ref, w1v_ref, w2v_ref, kc_ref, vc_ref):
    half = NSA_CMP_STRIDE * HEAD_DIM
    for c_ref, pe_ref, w1_ref, w2_ref, o_ref in ((ck_ref, pek_ref, w1k_ref, w2k_ref, kc_ref),
                                                 (cv_ref, pev_ref, w1v_ref, w2v_ref, vc_ref)):
        c = c_ref[...]
        a = _dot(c, w1_ref[0:half, :])
        b = _dot(c, w1_ref[half:2 * half, :])
        n = b.shape[0]
        pe_h = _dot(pe_ref[...], w1_ref[...])
        hid = a + pltpu.roll(b, n - 1, 0) + pe_h[0:1, :]
        o_ref[...] = _dot(_gelu_tanh(hid).astype(BF16), w2_ref[...]).astype(BF16)


def _cmp_call(ck, cv, pek, w1k, w2k, pev, w1v, w2v, B, S):
    nch = S // NSA_CMP_STRIDE
    blk = lambda b: (b, 0)
    const = lambda b: (0, 0)
    cw = NSA_CMP_STRIDE * HEAD_DIM
    in_specs = [pl.BlockSpec((nch, cw), blk), pl.BlockSpec((nch, cw), blk)]
    for _ in range(2):
        in_specs += [pl.BlockSpec((8, 2 * cw), const), pl.BlockSpec((2 * cw, NSA_CMP_HIDDEN), const),
                     pl.BlockSpec((NSA_CMP_HIDDEN, HEAD_DIM), const)]
    out = jax.ShapeDtypeStruct((B * nch, HEAD_DIM), BF16)
    return pl.pallas_call(
        _cmp_kernel, grid=(B,), in_specs=in_specs,
        out_specs=[pl.BlockSpec((nch, HEAD_DIM), blk)] * 2, out_shape=[out, out],
        compiler_params=_cparams(("arbitrary",)),
    )(ck, cv, pek, w1k, w2k, pev, w1v, w2v)


def _attend(q, segments, extra_logit=None):
    scores = []
    for k, _, bias, allowed in segments:
        s = _dot_t(q, k)
        if bias is not None:
            s = s + bias
        if allowed is not None:
            s = jnp.where(allowed, s, NEG_INF)
        scores.append(s)
    m = functools.reduce(jnp.maximum, [jnp.max(s, axis=-1, keepdims=True) for s in scores])
    if extra_logit is not None:
        m = jnp.maximum(m, extra_logit)
    den = jnp.exp2(extra_logit - m) if extra_logit is not None else 0.0
    o = 0.0
    for s, (_, v, _, _) in zip(scores, segments):
        e = jnp.exp2(s - m)
        den = den + jnp.sum(e, axis=-1, keepdims=True)
        o = o + _dot(e.astype(BF16), v)
    return o / den


def _causal_split(i, tq, nk, S):
    n_full = nk - S // N_CAUSAL_CLASSES
    t_col = i * tq + _iota((tq, 1), 0)
    tail_causal = (n_full + _iota((1, nk - n_full), 1)) <= t_col
    return n_full, tail_causal


N_CAUSAL_CLASSES = 8


def _for_causal_class(i, tq, S, body):
    step = S // N_CAUSAL_CLASSES
    cls = (i * tq + tq - 1) // step
    for c in range(N_CAUSAL_CLASSES):
        pl.when(cls == c)(functools.partial(body, (c + 1) * step))


def _nsa_kernel(q_ref, sm_ref, kc_ref, vc_ref, ov_ref, ex_ref, ksel_ref, vsel_ref, kwin_ref, vwin_ref, o_ref):
    i = pl.program_id(1)
    tq = q_ref.shape[0]
    S = ksel_ref.shape[0]
    ncmp = kc_ref.shape[0]
    q = q_ref[...]
    sm = sm_ref[...]
    t_col = i * tq + _iota((tq, 1), 0)

    c_end = _iota((1, ncmp), 1) * NSA_CMP_STRIDE + (NSA_CMP_LEN - 1)
    allowed_c = c_end <= t_col
    kc = kc_ref[...]
    vc = vc_ref[...]
    o_cmp = []
    p_cmp = []
    for hh in range(N_HEADS):
        qh = q[:, hh * 64:(hh + 1) * 64]
        s = jnp.where(allowed_c, _dot_t(qh, kc), NEG_INF)
        m = jnp.max(s, axis=-1, keepdims=True)
        e = jnp.where(allowed_c, jnp.exp2(s - m), 0.0)
        den = jnp.sum(e, axis=-1, keepdims=True)
        p = e * jnp.where(den > 0.0, 1.0 / den, 0.0)
        p_cmp.append(p.astype(BF16))
        o_cmp.append(_dot(p_cmp[-1], vc))

    n_blk = S // NSA_SEL_BLOCK
    n_sel = min(NSA_N_SEL, n_blk)

    def unselected_block_bias():
        imp = sum(_dot_t(ov_ref[...], p) for p in p_cmp)
        jb = _iota((n_blk, tq), 0)
        cur = (i * tq + _iota((1, tq), 1)) // NSA_SEL_BLOCK
        forced = (jb == 0) | (jb == cur) | (jb == cur - 1)
        imp = jnp.where(forced, NSA_FORCE_SCORE, imp)
        imp = jnp.where(jb > cur, -NSA_FORCE_SCORE, imp)
        rank = jnp.zeros((n_blk, tq), F32)
        for jp in range(n_blk):
            row = imp[jp:jp + 1, :]
            ahead = (row > imp) | ((row == imp) & (jp < jb))
            rank = rank + jnp.where(ahead, 1.0, 0.0)
        unsel = jnp.transpose(jnp.where(rank < float(n_sel), 0.0, NEG_INF))
        return jnp.concatenate([unsel, jnp.zeros((tq, LANES - n_blk), F32)], axis=1).astype(BF16)

    nband = NSA_WINDOW + tq
    start = pl.multiple_of(jnp.maximum(i * tq - NSA_WINDOW, 0), math.gcd(tq, NSA_WINDOW))
    kwin = kwin_ref[pl.ds(start, nband), :]
    vwin = vwin_ref[pl.ds(start, nband), :]
    s_band = start + _iota((1, nband), 1)
    allowed_win = (s_band <= t_col) & (s_band > t_col - NSA_WINDOW)
    gate = lambda hh, k: sm[:, SM_GATE0 + 3 * hh + k:SM_GATE0 + 3 * hh + k + 1]
    partial_out = []
    for hh in range(N_HEADS):
        o_win = _attend(q[:, hh * 64:(hh + 1) * 64], [(kwin, vwin, None, allowed_win)])
        partial_out.append(gate(hh, 0) * o_cmp[hh] + gate(hh, 2) * o_win)

    def selected(nk):
        n_full, tail_causal = _causal_split(i, tq, nk, S)
        if nk <= n_sel * NSA_SEL_BLOCK:
            part_bias = lambda lo, hi: None
        else:
            bias = _dot(unselected_block_bias(), ex_ref[:, 0:nk])
            part_bias = lambda lo, hi: bias[:, lo:hi]
        segments = [(ksel_ref[n_full:nk, :], vsel_ref[n_full:nk, :], part_bias(n_full, nk), tail_causal)]
        if n_full:
            segments.append((ksel_ref[0:n_full, :], vsel_ref[0:n_full, :], part_bias(0, n_full), None))
        outs = []
        for hh in range(N_HEADS):
            o_sel = _attend(q[:, hh * 64:(hh + 1) * 64], segments)
            outs.append(partial_out[hh] + gate(hh, 1) * o_sel)
        o_ref[...] = jnp.concatenate(outs, axis=1).astype(BF16)

    _for_causal_class(i, tq, S, selected)


def _nsa_call(qa, sm, kcc, vcc, overlap, expand, ksel, vsel, kwin, vwin, B, S):
    T = B * S
    nq = S // TQ
    ncmp = S // NSA_CMP_STRIDE
    tile = lambda b, i: (b * nq + i, 0)
    seq = lambda b, i: (b, 0)
    const = lambda b, i: (0, 0)
    in_specs = [pl.BlockSpec((TQ, 256), tile), pl.BlockSpec((TQ, LANES), tile),
                pl.BlockSpec((ncmp, 64), seq), pl.BlockSpec((ncmp, 64), seq),
                pl.BlockSpec(overlap.shape, const), pl.BlockSpec(expand.shape, const)] \
        + [pl.BlockSpec((S, 64), seq)] * 4
    return pl.pallas_call(
        _nsa_kernel, grid=(B, nq), in_specs=in_specs, out_specs=pl.BlockSpec((TQ, 256), tile),
        out_shape=jax.ShapeDtypeStruct((T, 256), BF16),
        compiler_params=_cparams(("arbitrary", "arbitrary")),
    )(qa, sm, kcc, vcc, overlap, expand, ksel, vsel, kwin, vwin)


def _fox_kernel(q_ref, nbias_ref, k_ref, v_ref, o_ref):
    i = pl.program_id(1)
    tq = q_ref.shape[0]
    S = k_ref.shape[1]
    q = q_ref[...]

    def attend(nk):
        n_full, tail_causal = _causal_split(i, tq, nk, S)
        outs = []
        for hh in range(N_HEADS):
            segments = [(k_ref[hh, n_full:nk, :], v_ref[hh, n_full:nk, :], nbias_ref[0, hh:hh + 1, n_full:nk],
                         tail_causal)]
            if n_full:
                segments.append((k_ref[hh, 0:n_full, :], v_ref[hh, 0:n_full, :], nbias_ref[0, hh:hh + 1, 0:n_full],
                                 None))
            outs.append(_attend(q[:, hh * 64:(hh + 1) * 64], segments))
        o_ref[...] = jnp.concatenate(outs, axis=1).astype(BF16)

    _for_causal_class(i, tq, S, attend)


def _fox_call(qb, nbias, kb, vb, B, S):
    T = B * S
    nq = S // TQ
    tile = lambda b, i: (b * nq + i, 0)
    in_specs = [pl.BlockSpec((TQ, 256), tile),
                pl.BlockSpec((1, N_HEADS, S), lambda b, i: (b, 0, 0)),
                pl.BlockSpec((N_HEADS, S, 64), lambda b, i: (0, b, 0)),
                pl.BlockSpec((N_HEADS, S, 64), lambda b, i: (0, b, 0))]
    return pl.pallas_call(
        _fox_kernel, grid=(B, nq), in_specs=in_specs, out_specs=pl.BlockSpec((TQ, 256), tile),
        out_shape=jax.ShapeDtypeStruct((T, 256), BF16),
        compiler_params=_cparams(("arbitrary", "arbitrary")),
    )(qb, nbias, kb, vb)


def _dsa_kernel(q_ref, qi_ref, wrow_ref, ki_ref, cnt_ref, k_ref, v_ref, o_ref, score_ref):
    i = pl.program_id(1)
    tq = q_ref.shape[0]
    S = ki_ref.shape[0]
    qi = qi_ref[...]
    t_col = i * tq + _iota((tq, 1), 0)
    top_k = min(DSA_TOPK_MAX, S // 4)
    q = q_ref[...]

    def attend(nk, bias, allowed):
        outs = [_attend(q[:, hh * 64:(hh + 1) * 64], [(k_ref[hh, 0:nk, :], v_ref[hh, 0:nk, :], bias, allowed)])
                for hh in range(N_HEADS)]
        o_ref[...] = jnp.concatenate(outs, axis=1).astype(BF16)

    def float_of_ordered_bits(u):
        k = u ^ INT_MIN
        return lax.bitcast_convert_type(jnp.where(k >= 0, k, k ^ 0x7FFFFFFF), F32)

    def select_and_attend(nk):
        t_row = i * tq + _iota((1, tq), 1)
        k_eff = jnp.minimum(t_row + 1, top_k).astype(F32)
        w_rows = wrow_ref[0]
        score = jnp.zeros((nk, tq), F32)
        for hh in range(N_HEADS):
            lg = _dot_t(ki_ref[0:nk, :], qi[:, hh * 64:(hh + 1) * 64])
            score = score + w_rows[hh:hh + 1, :] * jnp.maximum(lg, 0.0)
        score_ref[0:nk, :] = jnp.where(_iota((nk, 1), 0) <= t_row, score, -jnp.inf)

        def search(it, u):
            cand = u | lax.shift_left(jnp.int32(1), 31 - it)
            thr = float_of_ordered_bits(cand)
            cnt = jnp.sum(jnp.where(score_ref[0:nk, :] >= thr, 1.0, 0.0), axis=0, keepdims=True)
            return jnp.where(cnt >= k_eff, cand, u)

        u = lax.fori_loop(0, 32, search, jnp.zeros((1, tq), I32))
        thr = float_of_ordered_bits(u)
        score = score_ref[0:nk, :]
        gt = score > thr
        eq = score == thr
        need = k_eff - jnp.sum(jnp.where(gt, 1.0, 0.0), axis=0, keepdims=True)
        eqb = jnp.where(eq, 1.0, 0.0).astype(BF16)
        run = jnp.zeros((1, tq), F32)
        bias_chunks = []
        for c in range(nk // LANES):
            sl = slice(c * LANES, (c + 1) * LANES)
            r = _dot(cnt_ref[...], eqb[sl, :])
            prefix = r[0:LANES, :] + run
            run = run + r[LANES:LANES + 1, :]
            keep = gt[sl, :] | (eq[sl, :] & (prefix < need))
            bias_chunks.append(jnp.transpose(jnp.where(keep, 0.0, NEG_INF)))
        attend(nk, jnp.concatenate(bias_chunks, axis=1), None)

    def body(nk):
        if nk * N_CAUSAL_CLASSES == S:
            all_kept = (i + 1) * tq <= top_k
            pl.when(all_kept)(lambda: attend(nk, None, _iota((1, nk), 1) <= t_col))
            pl.when(jnp.logical_not(all_kept))(lambda: select_and_attend(nk))
        else:
            select_and_attend(nk)

    _for_causal_class(i, tq, S, body)


def _dsa_call(qc, qi, w_rows, ki, cntmat, kc, vc, B, S):
    T = B * S
    nq = S // TQ
    tile = lambda b, i: (b * nq + i, 0)
    in_specs = [pl.BlockSpec((TQ, 256), tile), pl.BlockSpec((TQ, 256), tile),
                pl.BlockSpec((1, N_HEADS, TQ), lambda b, i: (b, 0, i)),
                pl.BlockSpec((S, 64), lambda b, i: (b, 0)),
                pl.BlockSpec(cntmat.shape, lambda b, i: (0, 0)),
                pl.BlockSpec((N_HEADS, S, 64), lambda b, i: (0, b, 0)),
                pl.BlockSpec((N_HEADS, S, 64), lambda b, i: (0, b, 0))]
    return pl.pallas_call(
        _dsa_kernel, grid=(B, nq), in_specs=in_specs, out_specs=pl.BlockSpec((TQ, 256), tile),
        out_shape=jax.ShapeDtypeStruct((T, 256), BF16),
        scratch_shapes=[pltpu.VMEM((S, TQ), F32)],
        compiler_params=_cparams(("arbitrary", "arbitrary")),
    )(qc, qi, w_rows, ki, cntmat, kc, vc)


def _swa_kernel(q_ref, sink_ref, k_ref, v_ref, o_ref):
    i = pl.program_id(1)
    tq = q_ref.shape[0]
    q = q_ref[...]
    t_col = i * tq + _iota((tq, 1), 0)
    nband = SWA_WINDOW + tq
    start = pl.multiple_of(jnp.maximum(i * tq - SWA_WINDOW, 0), math.gcd(tq, SWA_WINDOW))
    s_band = start + _iota((1, nband), 1)
    allowed = (s_band <= t_col) & (s_band > t_col - SWA_WINDOW)
    group = N_HEADS // SWA_KV_HEADS
    outs = []
    for hh in range(N_HEADS):
        k = k_ref[hh // group, pl.ds(start, nband), :]
        v = v_ref[hh // group, pl.ds(start, nband), :]
        sink = jnp.zeros((tq, 1), F32) + LOG2E * sink_ref[0:1, hh:hh + 1]
        outs.append(_attend(q[:, hh * 64:(hh + 1) * 64], [(k, v, None, allowed)], extra_logit=sink))
    o_ref[...] = jnp.concatenate(outs, axis=1).astype(BF16)


def _swa_call(qd, sinks, kd, vd, B, S):
    T = B * S
    nq = S // TQ
    tile = lambda b, i: (b * nq + i, 0)
    in_specs = [pl.BlockSpec((TQ, 256), tile), pl.BlockSpec((1, LANES), lambda b, i: (0, 0)),
                pl.BlockSpec((SWA_KV_HEADS, S, 64), lambda b, i: (0, b, 0)),
                pl.BlockSpec((SWA_KV_HEADS, S, 64), lambda b, i: (0, b, 0))]
    return pl.pallas_call(
        _swa_kernel, grid=(B, nq), in_specs=in_specs, out_specs=pl.BlockSpec((TQ, 256), tile),
        out_shape=jax.ShapeDtypeStruct((T, 256), BF16),
        compiler_params=_cparams(("arbitrary", "arbitrary")),
    )(qd, sinks, kd, vd)


RT_E1, RT_E2, RT_W1, RT_W2, RT_R1, RT_R2 = 0, 1, 2, 3, 4, 5
ROUTER_E0 = MOE_GROUPS


def _merge_kernel(h_ref, oa_ref, ob_ref, oc_ref, od_ref, gain_ref, wg_ref, wb_ref, wo_ref,
                  fgain_ref, wr_ref, br_ref, h2_ref, xn_ref, rt_ref, cnt_ref, base_ref):
    step = pl.program_id(0)
    tm = h_ref.shape[0]
    logit_chunks = []
    for rows in (slice(c * MERGE_CHUNK, (c + 1) * MERGE_CHUNK) for c in range(tm // MERGE_CHUNK)):
        h = h_ref[rows, :]
        hn = h * lax.rsqrt(jnp.mean(h * h, axis=-1, keepdims=True) + NORM_EPS) * gain_ref[...]
        hb = hn.astype(BF16)
        merged = jnp.zeros_like(h)
        for i, o_ref in enumerate((oa_ref, ob_ref, oc_ref, od_ref)):
            gate = 1.0 / (1.0 + jnp.exp(-_dot(hb, wg_ref[i])))
            merged = merged + gate * _dot(o_ref[rows, :], wb_ref[i])
        h2 = h + _dot(merged.astype(BF16), wo_ref[...])
        h2_ref[rows, :] = h2
        xn = h2 * lax.rsqrt(jnp.mean(h2 * h2, axis=-1, keepdims=True) + NORM_EPS) * fgain_ref[...]
        xn_ref[rows, :] = xn
        logit_chunks.append(jnp.dot(xn, wr_ref[...], preferred_element_type=F32, precision=lax.Precision.HIGHEST))
    logits = jnp.concatenate(logit_chunks, axis=0) + br_ref[...]

    lane = _iota((tm, LANES), 1)
    lanef = lane.astype(F32)
    big = float(LANES)
    is_g = lane < MOE_GROUPS
    gl = jnp.where(is_g, logits, NEG_INF)
    gmax = jnp.max(gl, axis=-1, keepdims=True)
    grp = jnp.min(jnp.where(is_g & (gl == gmax), lanef, big), axis=-1, keepdims=True)
    p_grp = 1.0 / jnp.sum(jnp.where(is_g, jnp.exp(gl - gmax), 0.0), axis=-1, keepdims=True)
    lo = ROUTER_E0 + grp * MOE_EXPERTS_PER_GROUP
    in_grp = (lanef >= lo) & (lanef < lo + MOE_EXPERTS_PER_GROUP)
    el = jnp.where(in_grp, logits, NEG_INF)
    v1 = jnp.max(el, axis=-1, keepdims=True)
    l1 = jnp.min(jnp.where(in_grp & (el == v1), lanef, big), axis=-1, keepdims=True)
    el2 = jnp.where(lanef == l1, NEG_INF, el)
    v2 = jnp.max(el2, axis=-1, keepdims=True)
    l2 = jnp.min(jnp.where(in_grp & (lanef != l1) & (el2 == v2), lanef, big), axis=-1, keepdims=True)
    e21 = jnp.exp(v2 - v1)
    w1 = p_grp / (1.0 + e21)
    w2 = p_grp * e21 / (1.0 + e21)

    oh1 = jnp.where(lanef == l1, 1.0, 0.0)
    oh2 = jnp.where(lanef == l2, 1.0, 0.0)

    @pl.when(step == 0)
    def _():
        base_ref[...] = jnp.zeros_like(base_ref)

    both = oh1 + oh2
    strict = (_iota((tm, tm), 0) > _iota((tm, tm), 1)).astype(BF16)
    before = _dot(strict, both.astype(BF16)) + base_ref[0:1, :]
    r1 = jnp.sum(oh1 * before, axis=-1, keepdims=True)
    r2 = jnp.sum(oh2 * before, axis=-1, keepdims=True)
    total = base_ref[0:1, :] + jnp.sum(both, axis=0, keepdims=True)
    base_ref[0:1, :] = total
    cnt_ref[...] = jnp.broadcast_to(total, cnt_ref.shape)

    rt = jnp.zeros((tm, LANES), F32)
    for ln, val in ((RT_E1, l1 - ROUTER_E0), (RT_E2, l2 - ROUTER_E0), (RT_W1, w1), (RT_W2, w2), (RT_R1, r1), (RT_R2, r2)):
        rt = jnp.where(lane == ln, val, rt)
    rt_ref[...] = rt


def _merge_call(h, oa, ob, oc, od, gain, wg, wb, wo, fgain, wr, br):
    T, D = h.shape
    tm = TM_MERGE
    tile = lambda i: (i, 0)
    c2 = lambda i: (0, 0)
    c3 = lambda i: (0, 0, 0)
    once = pl.Buffered(1)
    in_specs = [pl.BlockSpec((tm, D), tile)] + [pl.BlockSpec((tm, 256), tile)] * 4 + [
        pl.BlockSpec((1, D), c2),
        pl.BlockSpec(wg.shape, c3, pipeline_mode=once),
        pl.BlockSpec(wb.shape, c3, pipeline_mode=once),
        pl.BlockSpec(wo.shape, c2, pipeline_mode=once),
        pl.BlockSpec((1, D), c2),
        pl.BlockSpec(wr.shape, c2),
        pl.BlockSpec((1, LANES), c2),
    ]
    out_shape = [jax.ShapeDtypeStruct((T, D), F32), jax.ShapeDtypeStruct((T, D), F32),
                 jax.ShapeDtypeStruct((T, LANES), F32), jax.ShapeDtypeStruct((8, LANES), F32)]
    out_specs = [pl.BlockSpec((tm, D), tile), pl.BlockSpec((tm, D), tile), pl.BlockSpec((tm, LANES), tile),
                 pl.BlockSpec((8, LANES), c2)]
    return pl.pallas_call(
        _merge_kernel, grid=(T // tm,), in_specs=in_specs, out_specs=out_specs, out_shape=out_shape,
        scratch_shapes=[pltpu.VMEM((8, LANES), F32)],
        compiler_params=_cparams(("arbitrary",)),
    )(h, oa, ob, oc, od, gain, wg, wb, wo, fgain, wr, br)


def _row_copy(src, src_row, dst, dst_row, sem):
    return pltpu.make_async_copy(src.at[pl.ds(src_row, 1)], dst.at[pl.ds(dst_row, 1)], sem)


def _dest_kernel(rt_ref, ps_ref, d_ref):
    rt = rt_ref[...]
    lane = _iota(rt.shape, 1)
    lanef = lane.astype(F32)
    ps = ps_ref[...]

    def dest(e_lane, r_lane):
        start = jnp.sum(jnp.where(lanef == rt[:, e_lane:e_lane + 1], ps, 0.0), axis=-1, keepdims=True)
        return start + rt[:, r_lane:r_lane + 1]

    d = jnp.where(lane == 0, dest(RT_E1, RT_R1), jnp.where(lane == 1, dest(RT_E2, RT_R2), 0.0))
    d_ref[...] = d.astype(I32)


def _dest_call(rt, pstart_row):
    T = rt.shape[0]
    tm = TM_MERGE
    return pl.pallas_call(
        _dest_kernel, grid=(T // tm,),
        in_specs=[pl.BlockSpec((tm, LANES), lambda i: (i, 0)), pl.BlockSpec((1, LANES), lambda i: (0, 0))],
        out_specs=pl.BlockSpec((tm, LANES), lambda i: (i, 0)), out_shape=jax.ShapeDtypeStruct((T, LANES), I32),
        compiler_params=_cparams(("arbitrary",)),
    )(rt, pstart_row)


DMA_ISSUE_UNROLL = 8


def _dispatch_kernel(d1_ref, d2_ref, xn_ref, xs_in_ref, xs_ref, sem):
    del xs_in_ref
    base = pl.program_id(0) * TD_DISPATCH

    def copies(r):
        t = base + r
        return (_row_copy(xn_ref, r, xs_ref, d1_ref[t], sem), _row_copy(xn_ref, r, xs_ref, d2_ref[t], sem))

    def start(r, c):
        for cp in copies(r):
            cp.start()
        return c

    def wait(r, c):
        for cp in copies(r):
            cp.wait()
        return c

    lax.fori_loop(0, TD_DISPATCH, start, 0, unroll=DMA_ISSUE_UNROLL)
    lax.fori_loop(0, TD_DISPATCH, wait, 0)


def _dispatch_call(d1, d2, xn, xs_zero):
    T, D = xn.shape
    any_spec = pl.BlockSpec(memory_space=pl.ANY)
    grid_spec = pltpu.PrefetchScalarGridSpec(
        num_scalar_prefetch=2, grid=(T // TD_DISPATCH,),
        in_specs=[pl.BlockSpec((TD_DISPATCH, D), lambda i, *_: (i, 0)), any_spec], out_specs=any_spec,
        scratch_shapes=[pltpu.SemaphoreType.DMA(())])
    return pl.pallas_call(
        _dispatch_kernel, grid_spec=grid_spec, out_shape=jax.ShapeDtypeStruct(xs_zero.shape, xs_zero.dtype),
        input_output_aliases={3: 0},
        compiler_params=pltpu.CompilerParams(dimension_semantics=("arbitrary",), has_side_effects=True,
                                             vmem_limit_bytes=VMEM_LIMIT),
    )(d1, d2, xn, xs_zero)


def _expert_kernel(be_ref, used_ref, x_ref, wg_ref, wu_ref, wd_ref, y_ref):
    i = pl.program_id(0)

    @pl.when(i < used_ref[0])
    def _():
        x = x_ref[...].astype(BF16)
        g = _dot(x, wg_ref[0, 0].astype(BF16))
        u = _dot(x, wu_ref[0, 0].astype(BF16))
        act = g / (1.0 + jnp.exp(-g)) * u
        y_ref[...] = _dot(act.astype(BF16), wd_ref[0, 0].astype(BF16))

    @pl.when(i >= used_ref[0])
    def _():
        y_ref[...] = jnp.zeros_like(y_ref)


def _expert_call(blk_expert, used, xs, wg, wu, wd, layer):
    rows, D = xs.shape
    rb = MOE_ROW_BLOCK
    DE = wg.shape[3]
    xmap = lambda i, be, used: (jnp.maximum(jnp.minimum(i, used[0] - 1), 0), 0)
    wmap = lambda i, be, used: (layer, be[i], 0, 0)
    grid_spec = pltpu.PrefetchScalarGridSpec(
        num_scalar_prefetch=2, grid=(rows // rb,),
        in_specs=[pl.BlockSpec((rb, D), xmap), pl.BlockSpec((1, 1, D, DE), wmap), pl.BlockSpec((1, 1, D, DE), wmap),
                  pl.BlockSpec((1, 1, DE, D), wmap)],
        out_specs=pl.BlockSpec((rb, D), lambda i, be, used: (i, 0)))
    return pl.pallas_call(
        _expert_kernel, grid_spec=grid_spec, out_shape=jax.ShapeDtypeStruct((rows, D), F32),
        compiler_params=_cparams(("arbitrary",)),
    )(blk_expert, used, xs, wg, wu, wd)


def _combine_kernel(final_norm, d1_ref, d2_ref, h_ref, rt_ref, gain_ref, ys_ref, o_ref, buf_ref, sem):
    base = pl.program_id(0) * TC_COMBINE

    def copies(r):
        t = base + r
        return (_row_copy(ys_ref, d1_ref[t], buf_ref.at[0], r, sem),
                _row_copy(ys_ref, d2_ref[t], buf_ref.at[1], r, sem))

    def start(r, c):
        for cp in copies(r):
            cp.start()
        return c

    def wait(r, c):
        for cp in copies(r):
            cp.wait()
        return c

    lax.fori_loop(0, TC_COMBINE, start, 0, unroll=DMA_ISSUE_UNROLL)
    lax.fori_loop(0, TC_COMBINE, wait, 0)
    rt = rt_ref[...]
    out = h_ref[...] + rt[:, RT_W1:RT_W1 + 1] * buf_ref[0] + rt[:, RT_W2:RT_W2 + 1] * buf_ref[1]
    if final_norm:
        out = out * lax.rsqrt(jnp.mean(out * out, axis=-1, keepdims=True) + NORM_EPS) * gain_ref[...]
    o_ref[...] = out


def _combine_call(d1, d2, h, rt, gain, ys, final_norm):
    T, D = h.shape
    tc = TC_COMBINE
    tile = lambda i, *_: (i, 0)
    grid_spec = pltpu.PrefetchScalarGridSpec(
        num_scalar_prefetch=2, grid=(T // tc,),
        in_specs=[pl.BlockSpec((tc, D), tile), pl.BlockSpec((tc, LANES), tile),
                  pl.BlockSpec((1, D), lambda i, *_: (0, 0)), pl.BlockSpec(memory_space=pl.ANY)],
        out_specs=pl.BlockSpec((tc, D), tile),
        scratch_shapes=[pltpu.VMEM((2, tc, D), F32), pltpu.SemaphoreType.DMA(())])
    return pl.pallas_call(
        functools.partial(_combine_kernel, final_norm), grid_spec=grid_spec,
        out_shape=jax.ShapeDtypeStruct((T, D), F32),
        compiler_params=_cparams(("arbitrary",)),
    )(d1, d2, h, rt, gain, ys)


def _in_proj_columns():
    widths = (('q_a', 256), ('k_cmp', 64), ('v_cmp', 64), ('k_sel', 64), ('v_sel', 64), ('k_win', 64), ('v_win', 64),
              ('g_a', 12), ('q_b', 256), ('k_b', 256), ('v_b', 256), ('f_b', 4), ('q_c', 256), ('ckv_c', 128),
              ('qi_c', 256), ('ki_c', 64), ('wi_c', 4), ('q_d', 256), ('k_d', 128), ('v_d', 128))
    off, cols = 0, {}
    for name, w in widths:
        cols[name] = np.arange(off, off + w)
        off += w
    cat = lambda names: np.concatenate([cols[n] for n in names])
    rope = cat(('q_a', 'q_c', 'qi_c', 'q_d', 'k_d', 'k_cmp', 'k_sel', 'k_win', 'ki_c'))
    plain = cat(('q_b', 'k_b', 'v_b', 'v_d', 'ckv_c', 'v_cmp', 'v_sel', 'v_win'))
    small = cat(('g_a', 'f_b', 'wi_c'))
    assert rope.size == N_ROPE_COLS and plain.size == N_PLAIN_COLS
    return np.concatenate([rope, plain, small]), N_SMALL_COLS - small.size


def _static_tables(S):
    n_cmp_pad = S // NSA_CMP_STRIDE
    n_blk = S // NSA_SEL_BLOCK
    c0 = np.arange(n_cmp_pad) * NSA_CMP_STRIDE
    b0 = np.arange(n_blk) * NSA_SEL_BLOCK
    ov = ((c0[:, None] < b0[None, :] + NSA_SEL_BLOCK) & (c0[:, None] + NSA_CMP_LEN > b0[None, :])).astype(np.float32)
    overlap = np.ascontiguousarray(ov.T)
    expand = np.zeros((LANES, S), np.float32)
    expand[np.arange(S) // NSA_SEL_BLOCK, np.arange(S)] = 1.0
    strict_lower = (np.arange(LANES)[:, None] > np.arange(LANES)[None, :]).astype(np.float32)
    cntmat = np.concatenate([strict_lower, np.ones((16, LANES), np.float32)], axis=0)
    return jnp.asarray(overlap, BF16), jnp.asarray(expand, BF16), jnp.asarray(cntmat, BF16)


def kernel(x, positions, attn_norm, w_in, nsa_pe_k, nsa_w1_k, nsa_w2_k, nsa_pe_v, nsa_w1_v, nsa_w2_v, fox_forget_bias, dsa_kv_norm, dsa_w_ukv, swa_sinks, w_branch, w_gate, w_out, ffn_norm, moe_w_group, moe_b_group, moe_w_expert, moe_b_expert, moe_w_gate, moe_w_up, moe_w_down, final_norm):
    B, S, D = x.shape
    T = B * S
    depth = w_in.shape[0]
    perm, n_pad = _in_proj_columns()
    overlap, expand, cntmat = _static_tables(S)
    half = HEAD_DIM // 2
    inv_freq = ROPE_THETA ** (-jnp.arange(half, dtype=F32) / half)
    invf = jnp.tile(inv_freq, LANES // half).reshape(1, LANES)
    pos = positions.reshape(T, 1).astype(I32)
    small_pad = lambda v, off: jnp.zeros((1, LANES), F32).at[0, off:off + v.shape[0]].set(v.astype(F32))

    n_rows = -(-(T * 2 + MOE_N_EXPERTS * (MOE_ROW_BLOCK - 1)) // MOE_ROW_BLOCK) * MOE_ROW_BLOCK
    h = x.reshape(T, D)
    for l in range(depth):
        w = jnp.pad(w_in[l][:, perm], ((0, 0), (0, n_pad))).astype(BF16)
        (qa, qc, qi, qd, kd, kcmp, ksel, kwin, ki, qb, kb, vb, vd, vcmp, vsel, vwin, kc, vc, sm) = _proj_call(
            h, pos, invf, attn_norm[l].reshape(1, D), w, small_pad(fox_forget_bias[l], SM_F0),
            dsa_kv_norm[l].reshape(1, DSA_KV_RANK), dsa_w_ukv[l].astype(BF16), B, S)

        cw = NSA_CMP_STRIDE * HEAD_DIM
        pe_rows = lambda pe: jnp.broadcast_to(pe.reshape(1, 2 * cw), (8, 2 * cw)).astype(BF16)
        kcc, vcc = _cmp_call(kcmp.reshape(T // NSA_CMP_STRIDE, cw), vcmp.reshape(T // NSA_CMP_STRIDE, cw),
                             pe_rows(nsa_pe_k[l]), nsa_w1_k[l].astype(BF16), nsa_w2_k[l].astype(BF16),
                             pe_rows(nsa_pe_v[l]), nsa_w1_v[l].astype(BF16), nsa_w2_v[l].astype(BF16), B, S)
        o_a = _nsa_call(qa, sm, kcc, vcc, overlap, expand, ksel, vsel, kwin, vwin, B, S)

        key_bias = sm[:, SM_F0:SM_F0 + N_HEADS].reshape(B, S, N_HEADS).transpose(0, 2, 1)
        o_b = _fox_call(qb, key_bias, kb, vb, B, S)
        idx_w = sm[:, SM_W0:SM_W0 + N_HEADS].reshape(B, S, N_HEADS).transpose(0, 2, 1)
        o_c = _dsa_call(qc, qi, idx_w, ki, cntmat, kc, vc, B, S)
        o_d = _swa_call(qd, small_pad(swa_sinks[l], 0), kd, vd, B, S)

        wr = jnp.zeros((D, LANES), F32).at[:, :MOE_GROUPS].set(moe_w_group[l]) \
            .at[:, ROUTER_E0:ROUTER_E0 + MOE_N_EXPERTS].set(moe_w_expert[l])
        br = jnp.zeros((1, LANES), F32).at[0, :MOE_GROUPS].set(moe_b_group[l]) \
            .at[0, ROUTER_E0:ROUTER_E0 + MOE_N_EXPERTS].set(moe_b_expert[l])
        h2, xn, rt, cnt = _merge_call(h, o_a, o_b, o_c, o_d, attn_norm[l].reshape(1, D), w_gate[l].astype(BF16),
                                      w_branch[l].astype(BF16), w_out[l].astype(BF16), ffn_norm[l].reshape(1, D),
                                      wr, br)

        counts = cnt[0, ROUTER_E0:ROUTER_E0 + MOE_N_EXPERTS].astype(I32)
        padded = (counts + MOE_ROW_BLOCK - 1) // MOE_ROW_BLOCK * MOE_ROW_BLOCK
        p_end = jnp.cumsum(padded)
        p_start = p_end - padded
        n_blocks = n_rows // MOE_ROW_BLOCK
        blk_first_row = jnp.arange(n_blocks, dtype=I32) * MOE_ROW_BLOCK
        blk_expert = jnp.minimum(jnp.sum((p_end[None, :] <= blk_first_row[:, None]).astype(I32), axis=1),
                                 MOE_N_EXPERTS - 1).astype(I32)
        used = (p_end[-1:] // MOE_ROW_BLOCK).astype(I32)
        dest = _dest_call(rt, small_pad(p_start, 0))
        d1, d2 = dest[:, 0], dest[:, 1]

        xs = _dispatch_call(d1, d2, xn, jnp.zeros((n_rows, D), F32))
        ys = _expert_call(blk_expert, used, xs, moe_w_gate, moe_w_up, moe_w_down, l)
        last = l == depth - 1
        h = _combine_call(d1, d2, h2, rt, final_norm.reshape(1, D), ys, last)
    return h.reshape(B, S, D)
```

```python
import functools
import math

import numpy as np
import jax
import jax.numpy as jnp
from jax import lax
from jax.experimental import pallas as pl
from jax.experimental.pallas import tpu as pltpu

F32 = jnp.float32
BF16 = jnp.bfloat16
I32 = jnp.int32

HEAD_DIM = 64
N_HEADS = 4
MIX_WIDTH = N_HEADS * HEAD_DIM
ROPE_THETA = 10000.0
NORM_EPS = 1e-6
NEG_INF = -1e30
INT_MIN = -2 ** 31
LOG2E = float(np.log2(np.e))

NSA_CMP_LEN = 32
NSA_CMP_STRIDE = 16
NSA_CMP_HIDDEN = 256
NSA_SEL_BLOCK = 64
NSA_N_SEL = 16
NSA_WINDOW = 512
NSA_FORCE_SCORE = 1e4
DSA_KV_RANK = 128
DSA_TOPK_MAX = 256
SWA_WINDOW = 128
SWA_KV_HEADS = 2
MOE_GROUPS = 8
MOE_EXPERTS_PER_GROUP = 8
MOE_N_EXPERTS = 64
MOE_D_EXPERT = 256

LANES = 128
VMEM_LIMIT = 48 * 1024 * 1024

N_ROPE_COLS = 1408
N_PLAIN_COLS = 1216
N_SMALL_COLS = 128
N_PROJ_COLS = N_ROPE_COLS + N_PLAIN_COLS + N_SMALL_COLS
SM_GATE0, SM_F0, SM_W0 = 0, 12, 16

TM_PROJ = 512
PROJ_CHUNK = 128
TQ = 256
TM_MERGE = 512
MERGE_CHUNK = 128
MOE_ROW_BLOCK = 256
TD_DISPATCH = 256
TC_COMBINE = 256


def _cparams(sem):
    return pltpu.CompilerParams(dimension_semantics=sem, vmem_limit_bytes=VMEM_LIMIT)


def _dot(a, b):
    return jnp.dot(a, b, preferred_element_type=F32)


def _dot_t(a, b):
    return lax.dot_general(a, b, (((1,), (1,)), ((), ())), preferred_element_type=F32)


def _iota(shape, dim):
    return lax.broadcasted_iota(I32, shape, dim)


def _rope_slab(x, cos, sin_signed, first_half):
    rot = jnp.where(first_half, pltpu.roll(x, 96, 1), pltpu.roll(x, 32, 1))
    return x * cos + rot * sin_signed


def _proj_kernel(h_ref, pos_ref, invf_ref, gain_ref, w_ref, fbias_ref, kvn_ref, wukv_ref,
                 qa_ref, qc_ref, qi_ref, qd_ref, kd_ref, kcmp_ref, ksel_ref, kwin_ref, ki_ref,
                 qb_ref, kb_ref, vb_ref, vd_ref, vcmp_ref, vsel_ref, vwin_ref, kc_ref, vc_ref,
                 sm_ref, carry_ref):
    j = pl.program_id(1)
    tm = h_ref.shape[0]
    tc = PROJ_CHUNK

    @pl.when(j == 0)
    def _():
        carry_ref[...] = jnp.zeros_like(carry_ref)

    lane = _iota((tc, LANES), 1)
    first_half = (lane % HEAD_DIM) < (HEAD_DIM // 2)
    is_f = (lane >= SM_F0) & (lane < SM_W0)
    tri = (_iota((tc, tc), 0) >= _iota((tc, tc), 1)).astype(F32)
    scale = HEAD_DIM ** -0.5
    softmax_scale = scale * LOG2E

    for rows in (slice(c * tc, (c + 1) * tc) for c in range(tm // tc)):
        h = h_ref[rows, :]
        hn = h * lax.rsqrt(jnp.mean(h * h, axis=-1, keepdims=True) + NORM_EPS) * gain_ref[...]
        hb = hn.astype(BF16)

        ang = pos_ref[rows, :].astype(F32) * invf_ref[...]
        cos = jnp.cos(ang)
        sin = jnp.sin(ang)
        sin_signed = jnp.where(first_half, -sin, sin)
        rope = lambda x: _rope_slab(x, cos, sin_signed, first_half)

        zr = _dot(hb, w_ref[:, 0:N_ROPE_COLS])
        slab = lambda k: rope(zr[:, k * LANES:(k + 1) * LANES])
        for q_ref, k0, sc in ((qa_ref, 0, softmax_scale), (qc_ref, 2, softmax_scale), (qi_ref, 4, scale),
                              (qd_ref, 6, softmax_scale)):
            q_ref[rows, :] = (jnp.concatenate([slab(k0), slab(k0 + 1)], axis=1) * sc).astype(BF16)
        s8 = slab(8).astype(BF16)
        kd_ref[0, rows, :] = s8[:, :HEAD_DIM]
        kd_ref[1, rows, :] = s8[:, HEAD_DIM:]
        s9 = slab(9).astype(BF16)
        kcmp_ref[rows, :] = s9[:, :HEAD_DIM]
        ksel_ref[rows, :] = s9[:, HEAD_DIM:]
        s10 = slab(10).astype(BF16)
        kwin_ref[rows, :] = s10[:, :HEAD_DIM]
        ki_ref[rows, :] = s10[:, HEAD_DIM:]

        zp = _dot(hb, w_ref[:, N_ROPE_COLS:N_ROPE_COLS + N_PLAIN_COLS])
        qb_ref[rows, :] = (zp[:, 0:256] * softmax_scale).astype(BF16)
        for hh in range(N_HEADS):
            kb_ref[hh, rows, :] = zp[:, 256 + hh * 64:256 + (hh + 1) * 64].astype(BF16)
            vb_ref[hh, rows, :] = zp[:, 512 + hh * 64:512 + (hh + 1) * 64].astype(BF16)
        vd_ref[0, rows, :] = zp[:, 768:832].astype(BF16)
        vd_ref[1, rows, :] = zp[:, 832:896].astype(BF16)
        ckv = zp[:, 896:1024]
        vcmp_ref[rows, :] = zp[:, 1024:1088].astype(BF16)
        vsel_ref[rows, :] = zp[:, 1088:1152].astype(BF16)
        vwin_ref[rows, :] = zp[:, 1152:1216].astype(BF16)

        ckvn = ckv * lax.rsqrt(jnp.mean(ckv * ckv, axis=-1, keepdims=True) + NORM_EPS) * kvn_ref[...]
        kvc = _dot(ckvn.astype(BF16), wukv_ref[...])
        for k in range(2):
            kk = rope(kvc[:, k * LANES:(k + 1) * LANES]).astype(BF16)
            kc_ref[2 * k, rows, :] = kk[:, :HEAD_DIM]
            kc_ref[2 * k + 1, rows, :] = kk[:, HEAD_DIM:]
        for hh in range(N_HEADS):
            vc_ref[hh, rows, :] = kvc[:, 256 + hh * 64:256 + (hh + 1) * 64].astype(BF16)

        zs = _dot(hb, w_ref[:, N_ROPE_COLS + N_PLAIN_COLS:N_PROJ_COLS])
        sig = 1.0 / (1.0 + jnp.exp(-zs))
        xf = zs + fbias_ref[...]
        logf = jnp.minimum(xf, 0.0) - jnp.log(1.0 + jnp.exp(-jnp.abs(xf)))
        logf = jnp.where(is_f, logf, 0.0)
        cum = jnp.dot(tri, logf, preferred_element_type=F32, precision=lax.Precision.HIGHEST)
        cum = cum + carry_ref[0:1, :]
        carry_ref[0:1, :] = cum[tc - 1:tc, :]
        sm_ref[rows, :] = jnp.where(lane < SM_F0, sig,
                                    jnp.where(is_f, -LOG2E * cum, jnp.where(lane < SM_W0 + 4, 0.5 * zs, 0.0)))


def _proj_call(h, pos, invf, gain, w, fbias, kvn, wukv, B, S):
    T = B * S
    tm = TM_PROJ
    nj = S // tm
    row = lambda b, j: (b * nj + j, 0)
    hrow = lambda b, j: (0, b * nj + j, 0)
    const2 = lambda b, j: (0, 0)
    tok = lambda w_, dt: jax.ShapeDtypeStruct((T, w_), dt)
    hm = lambda n: jax.ShapeDtypeStruct((n, T, HEAD_DIM), BF16)
    out_shape = [tok(256, BF16)] * 4 + [hm(2)] + [tok(64, BF16)] * 4 + [tok(256, BF16), hm(4), hm(4), hm(2)] \
        + [tok(64, BF16)] * 3 + [hm(4), hm(4), tok(LANES, F32)]
    spec_tok = lambda w_: pl.BlockSpec((tm, w_), row)
    spec_hm = lambda n: pl.BlockSpec((n, tm, HEAD_DIM), hrow)
    out_specs = [spec_tok(256)] * 4 + [spec_hm(2)] + [spec_tok(64)] * 4 + [spec_tok(256), spec_hm(4), spec_hm(4), spec_hm(2)] \
        + [spec_tok(64)] * 3 + [spec_hm(4), spec_hm(4), spec_tok(LANES)]
    in_specs = [
        pl.BlockSpec((tm, h.shape[1]), row),
        pl.BlockSpec((tm, 1), row),
        pl.BlockSpec((1, LANES), const2),
        pl.BlockSpec((1, h.shape[1]), const2),
        pl.BlockSpec(w.shape, const2),
        pl.BlockSpec((1, LANES), const2),
        pl.BlockSpec((1, DSA_KV_RANK), const2),
        pl.BlockSpec(wukv.shape, const2),
    ]
    return pl.pallas_call(
        _proj_kernel, grid=(B, nj), in_specs=in_specs, out_specs=out_specs, out_shape=out_shape,
        scratch_shapes=[pltpu.VMEM((8, LANES), F32)],
        compiler_params=_cparams(("arbitrary", "arbitrary")),
    )(h, pos, invf, gain, w, fbias, kvn, wukv)


def _gelu_tanh(x):
    return 0.5 * x * (1.0 + jnp.tanh(np.sqrt(2.0 / np.pi).astype(np.float32) * (x + 0.044715 * (x * x * x))))


def _cmp_kernel(ck_ref, cv_ref, pek_ref, w1k_ref, w2k_ref, pev_ref, w1v_ref, w2v_ref, kc_ref, vc_ref):
    half = NSA_CMP_STRIDE * HEAD_DIM
    for c_ref, pe_ref, w1_ref, w2_ref, o_ref in ((ck_ref, pek_ref, w1k_ref, w2k_ref, kc_ref),
                                                 (cv_ref, pev_ref, w1v_ref, w2v_ref, vc_ref)):
        c = c_ref[...]
        a = _dot(c, w1_ref[0:half, :])
        b = _dot(c, w1_ref[half:2 * half, :])
        n = b.shape[0]
        pe_h = _dot(pe_ref[...], w1_ref[...])
        hid = a + pltpu.roll(b, n - 1, 0) + pe_h[0:1, :]
        o_ref[...] = _dot(_gelu_tanh(hid).astype(BF16), w2_ref[...]).astype(BF16)


def _cmp_call(ck, cv, pek, w1k, w2k, pev, w1v, w2v, B, S):
    nch = S // NSA_CMP_STRIDE
    blk = lambda b: (b, 0)
    const = lambda b: (0, 0)
    cw = NSA_CMP_STRIDE * HEAD_DIM
    in_specs = [pl.BlockSpec((nch, cw), blk), pl.BlockSpec((nch, cw), blk)]
    for _ in range(2):
        in_specs += [pl.BlockSpec((8, 2 * cw), const), pl.BlockSpec((2 * cw, NSA_CMP_HIDDEN), const),
                     pl.BlockSpec((NSA_CMP_HIDDEN, HEAD_DIM), const)]
    out = jax.ShapeDtypeStruct((B * nch, HEAD_DIM), BF16)
    return pl.pallas_call(
        _cmp_kernel, grid=(B,), in_specs=in_specs,
        out_specs=[pl.BlockSpec((nch, HEAD_DIM), blk)] * 2, out_shape=[out, out],
        compiler_params=_cparams(("arbitrary",)),
    )(ck, cv, pek, w1k, w2k, pev, w1v, w2v)


def _attend(q, segments, extra_logit=None):
    scores = []
    for k, _, bias, allowed in segments:
        s = _dot_t(q, k)
        if bias is not None:
            s = s + bias
        if allowed is not None:
            s = jnp.where(allowed, s, NEG_INF)
        scores.append(s)
    m = functools.reduce(jnp.maximum, [jnp.max(s, axis=-1, keepdims=True) for s in scores])
    if extra_logit is not None:
        m = jnp.maximum(m, extra_logit)
    den = jnp.exp2(extra_logit - m) if extra_logit is not None else 0.0
    o = 0.0
    for s, (_, v, _, _) in zip(scores, segments):
        e = jnp.exp2(s - m)
        den = den + jnp.sum(e, axis=-1, keepdims=True)
        o = o + _dot(e.astype(BF16), v)
    return o / den


def _causal_split(i, tq, nk, S):
    n_full = nk - S // N_CAUSAL_CLASSES
    t_col = i * tq + _iota((tq, 1), 0)
    tail_causal = (n_full + _iota((1, nk - n_full), 1)) <= t_col
    return n_full, tail_causal


N_CAUSAL_CLASSES = 8


def _for_causal_class(i, tq, S, body):
    step = S // N_CAUSAL_CLASSES
    cls = (i * tq + tq - 1) // step
    for c in range(N_CAUSAL_CLASSES):
        pl.when(cls == c)(functools.partial(body, (c + 1) * step))


def _nsa_kernel(q_ref, sm_ref, kc_ref, vc_ref, ov_ref, ex_ref, ksel_ref, vsel_ref, kwin_ref, vwin_ref, o_ref):
    i = pl.program_id(1)
    tq = q_ref.shape[0]
    S = ksel_ref.shape[0]
    ncmp = kc_ref.shape[0]
    q = q_ref[...]
    sm = sm_ref[...]
    t_col = i * tq + _iota((tq, 1), 0)

    c_end = _iota((1, ncmp), 1) * NSA_CMP_STRIDE + (NSA_CMP_LEN - 1)
    allowed_c = c_end <= t_col
    kc = kc_ref[...]
    vc = vc_ref[...]
    o_cmp = []
    p_cmp = []
    for hh in range(N_HEADS):
        qh = q[:, hh * 64:(hh + 1) * 64]
        s = jnp.where(allowed_c, _dot_t(qh, kc), NEG_INF)
        m = jnp.max(s, axis=-1, keepdims=True)
        e = jnp.where(allowed_c, jnp.exp2(s - m), 0.0)
        den = jnp.sum(e, axis=-1, keepdims=True)
        p = e * jnp.where(den > 0.0, 1.0 / den, 0.0)
        p_cmp.append(p.astype(BF16))
        o_cmp.append(_dot(p_cmp[-1], vc))

    n_blk = S // NSA_SEL_BLOCK
    n_sel = min(NSA_N_SEL, n_blk)

    def unselected_block_bias():
        imp = sum(_dot_t(ov_ref[...], p) for p in p_cmp)
        jb = _iota((n_blk, tq), 0)
        cur = (i * tq + _iota((1, tq), 1)) // NSA_SEL_BLOCK
        forced = (jb == 0) | (jb == cur) | (jb == cur - 1)
        imp = jnp.where(forced, NSA_FORCE_SCORE, imp)
        imp = jnp.where(jb > cur, -NSA_FORCE_SCORE, imp)
        rank = jnp.zeros((n_blk, tq), F32)
        for jp in range(n_blk):
            row = imp[jp:jp + 1, :]
            ahead = (row > imp) | ((row == imp) & (jp < jb))
            rank = rank + jnp.where(ahead, 1.0, 0.0)
        unsel = jnp.transpose(jnp.where(rank < float(n_sel), 0.0, NEG_INF))
        return jnp.concatenate([unsel, jnp.zeros((tq, LANES - n_blk), F32)], axis=1).astype(BF16)

    nband = NSA_WINDOW + tq
    start = pl.multiple_of(jnp.maximum(i * tq - NSA_WINDOW, 0), math.gcd(tq, NSA_WINDOW))
    kwin = kwin_ref[pl.ds(start, nband), :]
    vwin = vwin_ref[pl.ds(start, nband), :]
    s_band = start + _iota((1, nband), 1)
    allowed_win = (s_band <= t_col) & (s_band > t_col - NSA_WINDOW)
    gate = lambda hh, k: sm[:, SM_GATE0 + 3 * hh + k:SM_GATE0 + 3 * hh + k + 1]
    partial_out = []
    for hh in range(N_HEADS):
        o_win = _attend(q[:, hh * 64:(hh + 1) * 64], [(kwin, vwin, None, allowed_win)])
        partial_out.append(gate(hh, 0) * o_cmp[hh] + gate(hh, 2) * o_win)

    def selected(nk):
        n_full, tail_causal = _causal_split(i, tq, nk, S)
        if nk <= n_sel * NSA_SEL_BLOCK:
            part_bias = lambda lo, hi: None
        else:
            bias = _dot(unselected_block_bias(), ex_ref[:, 0:nk])
            part_bias = lambda lo, hi: bias[:, lo:hi]
        segments = [(ksel_ref[n_full:nk, :], vsel_ref[n_full:nk, :], part_bias(n_full, nk), tail_causal)]
        if n_full:
            segments.append((ksel_ref[0:n_full, :], vsel_ref[0:n_full, :], part_bias(0, n_full), None))
        outs = []
        for hh in range(N_HEADS):
            o_sel = _attend(q[:, hh * 64:(hh + 1) * 64], segments)
            outs.append(partial_out[hh] + gate(hh, 1) * o_sel)
        o_ref[...] = jnp.concatenate(outs, axis=1).astype(BF16)

    _for_causal_class(i, tq, S, selected)


def _nsa_call(qa, sm, kcc, vcc, overlap, expand, ksel, vsel, kwin, vwin, B, S):
    T = B * S
    nq = S // TQ
    ncmp = S // NSA_CMP_STRIDE
    tile = lambda b, i: (b * nq + i, 0)
    seq = lambda b, i: (b, 0)
    const = lambda b, i: (0, 0)
    in_specs = [pl.BlockSpec((TQ, 256), tile), pl.BlockSpec((TQ, LANES), tile),
                pl.BlockSpec((ncmp, 64), seq), pl.BlockSpec((ncmp, 64), seq),
                pl.BlockSpec(overlap.shape, const), pl.BlockSpec(expand.shape, const)] \
        + [pl.BlockSpec((S, 64), seq)] * 4
    return pl.pallas_call(
        _nsa_kernel, grid=(B, nq), in_specs=in_specs, out_specs=pl.BlockSpec((TQ, 256), tile),
        out_shape=jax.ShapeDtypeStruct((T, 256), BF16),
        compiler_params=_cparams(("arbitrary", "arbitrary")),
    )(qa, sm, kcc, vcc, overlap, expand, ksel, vsel, kwin, vwin)


def _fox_kernel(q_ref, nbias_ref, k_ref, v_ref, o_ref):
    i = pl.program_id(1)
    tq = q_ref.shape[0]
    S = k_ref.shape[1]
    q = q_ref[...]

    def attend(nk):
        n_full, tail_causal = _causal_split(i, tq, nk, S)
        outs = []
        for hh in range(N_HEADS):
            segments = [(k_ref[hh, n_full:nk, :], v_ref[hh, n_full:nk, :], nbias_ref[0, hh:hh + 1, n_full:nk],
                         tail_causal)]
            if n_full:
                segments.append((k_ref[hh, 0:n_full, :], v_ref[hh, 0:n_full, :], nbias_ref[0, hh:hh + 1, 0:n_full],
                                 None))
            outs.append(_attend(q[:, hh * 64:(hh + 1) * 64], segments))
        o_ref[...] = jnp.concatenate(outs, axis=1).astype(BF16)

    _for_causal_class(i, tq, S, attend)


def _fox_call(qb, nbias, kb, vb, B, S):
    T = B * S
    nq = S // TQ
    tile = lambda b, i: (b * nq + i, 0)
    in_specs = [pl.BlockSpec((TQ, 256), tile),
                pl.BlockSpec((1, N_HEADS, S), lambda b, i: (b, 0, 0)),
                pl.BlockSpec((N_HEADS, S, 64), lambda b, i: (0, b, 0)),
                pl.BlockSpec((N_HEADS, S, 64), lambda b, i: (0, b, 0))]
    return pl.pallas_call(
        _fox_kernel, grid=(B, nq), in_specs=in_specs, out_specs=pl.BlockSpec((TQ, 256), tile),
        out_shape=jax.ShapeDtypeStruct((T, 256), BF16),
        compiler_params=_cparams(("arbitrary", "arbitrary")),
    )(qb, nbias, kb, vb)


def _dsa_kernel(q_ref, qi_ref, wrow_ref, ki_ref, cnt_ref, k_ref, v_ref, o_ref, score_ref):
    i = pl.program_id(1)
    tq = q_ref.shape[0]
    S = ki_ref.shape[0]
    qi = qi_ref[...]
    t_col = i * tq + _iota((tq, 1), 0)
    top_k = min(DSA_TOPK_MAX, S // 4)
    q = q_ref[...]

    def attend(nk, bias, allowed):
        outs = [_attend(q[:, hh * 64:(hh + 1) * 64], [(k_ref[hh, 0:nk, :], v_ref[hh, 0:nk, :], bias, allowed)])
                for hh in range(N_HEADS)]
        o_ref[...] = jnp.concatenate(outs, axis=1).astype(BF16)

    def float_of_ordered_bits(u):
        k = u ^ INT_MIN
        return lax.bitcast_convert_type(jnp.where(k >= 0, k, k ^ 0x7FFFFFFF), F32)

    def select_and_attend(nk):
        t_row = i * tq + _iota((1, tq), 1)
        k_eff = jnp.minimum(t_row + 1, top_k).astype(F32)
        w_rows = wrow_ref[0]
        score = jnp.zeros((nk, tq), F32)
        for hh in range(N_HEADS):
            lg = _dot_t(ki_ref[0:nk, :], qi[:, hh * 64:(hh + 1) * 64])
            score = score + w_rows[hh:hh + 1, :] * jnp.maximum(lg, 0.0)
        score_ref[0:nk, :] = jnp.where(_iota((nk, 1), 0) <= t_row, score, -jnp.inf)

        def search(it, u):
            cand = u | lax.shift_left(jnp.int32(1), 31 - it)
            thr = float_of_ordered_bits(cand)
            cnt = jnp.sum(jnp.where(score_ref[0:nk, :] >= thr, 1.0, 0.0), axis=0, keepdims=True)
            return jnp.where(cnt >= k_eff, cand, u)

        u = lax.fori_loop(0, 32, search, jnp.zeros((1, tq), I32))
        thr = float_of_ordered_bits(u)
        score = score_ref[0:nk, :]
        gt = score > thr
        eq = score == thr
        need = k_eff - jnp.sum(jnp.where(gt, 1.0, 0.0), axis=0, keepdims=True)
        eqb = jnp.where(eq, 1.0, 0.0).astype(BF16)
        run = jnp.zeros((1, tq), F32)
        bias_chunks = []
        for c in range(nk // LANES):
            sl = slice(c * LANES, (c + 1) * LANES)
            r = _dot(cnt_ref[...], eqb[sl, :])
            prefix = r[0:LANES, :] + run
            run = run + r[LANES:LANES + 1, :]
            keep = gt[sl, :] | (eq[sl, :] & (prefix < need))
            bias_chunks.append(jnp.transpose(jnp.where(keep, 0.0, NEG_INF)))
        attend(nk, jnp.concatenate(bias_chunks, axis=1), None)

    def body(nk):
        if nk * N_CAUSAL_CLASSES == S:
            all_kept = (i + 1) * tq <= top_k
            pl.when(all_kept)(lambda: attend(nk, None, _iota((1, nk), 1) <= t_col))
            pl.when(jnp.logical_not(all_kept))(lambda: select_and_attend(nk))
        else:
            select_and_attend(nk)

    _for_causal_class(i, tq, S, body)


def _dsa_call(qc, qi, w_rows, ki, cntmat, kc, vc, B, S):
    T = B * S
    nq = S // TQ
    tile = lambda b, i: (b * nq + i, 0)
    in_specs = [pl.BlockSpec((TQ, 256), tile), pl.BlockSpec((TQ, 256), tile),
                pl.BlockSpec((1, N_HEADS, TQ), lambda b, i: (b, 0, i)),
                pl.BlockSpec((S, 64), lambda b, i: (b, 0)),
                pl.BlockSpec(cntmat.shape, lambda b, i: (0, 0)),
                pl.BlockSpec((N_HEADS, S, 64), lambda b, i: (0, b, 0)),
                pl.BlockSpec((N_HEADS, S, 64), lambda b, i: (0, b, 0))]
    return pl.pallas_call(
        _dsa_kernel, grid=(B, nq), in_specs=in_specs, out_specs=pl.BlockSpec((TQ, 256), tile),
        out_shape=jax.ShapeDtypeStruct((T, 256), BF16),
        scratch_shapes=[pltpu.VMEM((S, TQ), F32)],
        compiler_params=_cparams(("arbitrary", "arbitrary")),
    )(qc, qi, w_rows, ki, cntmat, kc, vc)


def _swa_kernel(q_ref, sink_ref, k_ref, v_ref, o_ref):
    i = pl.program_id(1)
    tq = q_ref.shape[0]
    q = q_ref[...]
    t_col = i * tq + _iota((tq, 1), 0)
    nband = SWA_WINDOW + tq
    start = pl.multiple_of(jnp.maximum(i * tq - SWA_WINDOW, 0), math.gcd(tq, SWA_WINDOW))
    s_band = start + _iota((1, nband), 1)
    allowed = (s_band <= t_col) & (s_band > t_col - SWA_WINDOW)
    group = N_HEADS // SWA_KV_HEADS
    outs = []
    for hh in range(N_HEADS):
        k = k_ref[hh // group, pl.ds(start, nband), :]
        v = v_ref[hh // group, pl.ds(start, nband), :]
        sink = jnp.zeros((tq, 1), F32) + LOG2E * sink_ref[0:1, hh:hh + 1]
        outs.append(_attend(q[:, hh * 64:(hh + 1) * 64], [(k, v, None, allowed)], extra_logit=sink))
    o_ref[...] = jnp.concatenate(outs, axis=1).astype(BF16)


def _swa_call(qd, sinks, kd, vd, B, S):
    T = B * S
    nq = S // TQ
    tile = lambda b, i: (b * nq + i, 0)
    in_specs = [pl.BlockSpec((TQ, 256), tile), pl.BlockSpec((1, LANES), lambda b, i: (0, 0)),
                pl.BlockSpec((SWA_KV_HEADS, S, 64), lambda b, i: (0, b, 0)),
                pl.BlockSpec((SWA_KV_HEADS, S, 64), lambda b, i: (0, b, 0))]
    return pl.pallas_call(
        _swa_kernel, grid=(B, nq), in_specs=in_specs, out_specs=pl.BlockSpec((TQ, 256), tile),
        out_shape=jax.ShapeDtypeStruct((T, 256), BF16),
        compiler_params=_cparams(("arbitrary", "arbitrary")),
    )(qd, sinks, kd, vd)


RT_E1, RT_E2, RT_W1, RT_W2, RT_R1, RT_R2 = 0, 1, 2, 3, 4, 5
ROUTER_E0 = MOE_GROUPS


def _merge_kernel(h_ref, oa_ref, ob_ref, oc_ref, od_ref, gain_ref, wg_ref, wb_ref, wo_ref,
                  fgain_ref, wr_ref, br_ref, h2_ref, xn_ref, rt_ref, cnt_ref, base_ref):
    step = pl.program_id(0)
    tm = h_ref.shape[0]
    logit_chunks = []
    for rows in (slice(c * MERGE_CHUNK, (c + 1) * MERGE_CHUNK) for c in range(tm // MERGE_CHUNK)):
        h = h_ref[rows, :]
        hn = h * lax.rsqrt(jnp.mean(h * h, axis=-1, keepdims=True) + NORM_EPS) * gain_ref[...]
        hb = hn.astype(BF16)
        merged = jnp.zeros_like(h)
        for i, o_ref in enumerate((oa_ref, ob_ref, oc_ref, od_ref)):
            gate = 1.0 / (1.0 + jnp.exp(-_dot(hb, wg_ref[i])))
            merged = merged + gate * _dot(o_ref[rows, :], wb_ref[i])
        h2 = h + _dot(merged.astype(BF16), wo_ref[...])
        h2_ref[rows, :] = h2
        xn = h2 * lax.rsqrt(jnp.mean(h2 * h2, axis=-1, keepdims=True) + NORM_EPS) * fgain_ref[...]
        xn_ref[rows, :] = xn
        logit_chunks.append(jnp.dot(xn, wr_ref[...], preferred_element_type=F32, precision=lax.Precision.HIGHEST))
    logits = jnp.concatenate(logit_chunks, axis=0) + br_ref[...]

    lane = _iota((tm, LANES), 1)
    lanef = lane.astype(F32)
    big = float(LANES)
    is_g = lane < MOE_GROUPS
    gl = jnp.where(is_g, logits, NEG_INF)
    gmax = jnp.max(gl, axis=-1, keepdims=True)
    grp = jnp.min(jnp.where(is_g & (gl == gmax), lanef, big), axis=-1, keepdims=True)
    p_grp = 1.0 / jnp.sum(jnp.where(is_g, jnp.exp(gl - gmax), 0.0), axis=-1, keepdims=True)
    lo = ROUTER_E0 + grp * MOE_EXPERTS_PER_GROUP
    in_grp = (lanef >= lo) & (lanef < lo + MOE_EXPERTS_PER_GROUP)
    el = jnp.where(in_grp, logits, NEG_INF)
    v1 = jnp.max(el, axis=-1, keepdims=True)
    l1 = jnp.min(jnp.where(in_grp & (el == v1), lanef, big), axis=-1, keepdims=True)
    el2 = jnp.where(lanef == l1, NEG_INF, el)
    v2 = jnp.max(el2, axis=-1, keepdims=True)
    l2 = jnp.min(jnp.where(in_grp & (lanef != l1) & (el2 == v2), lanef, big), axis=-1, keepdims=True)
    e21 = jnp.exp(v2 - v1)
    w1 = p_grp / (1.0 + e21)
    w2 = p_grp * e21 / (1.0 + e21)

    oh1 = jnp.where(lanef == l1, 1.0, 0.0)
    oh2 = jnp.where(lanef == l2, 1.0, 0.0)

    @pl.when(step == 0)
    def _():
        base_ref[...] = jnp.zeros_like(base_ref)

    both = oh1 + oh2
    strict = (_iota((tm, tm), 0) > _iota((tm, tm), 1)).astype(BF16)
    before = _dot(strict, both.astype(BF16)) + base_ref[0:1, :]
    r1 = jnp.sum(oh1 * before, axis=-1, keepdims=True)
    r2 = jnp.sum(oh2 * before, axis=-1, keepdims=True)
    total = base_ref[0:1, :] + jnp.sum(both, axis=0, keepdims=True)
    base_ref[0:1, :] = total
    cnt_ref[...] = jnp.broadcast_to(total, cnt_ref.shape)

    rt = jnp.zeros((tm, LANES), F32)
    for ln, val in ((RT_E1, l1 - ROUTER_E0), (RT_E2, l2 - ROUTER_E0), (RT_W1, w1), (RT_W2, w2), (RT_R1, r1), (RT_R2, r2)):
        rt = jnp.where(lane == ln, val, rt)
    rt_ref[...] = rt


def _merge_call(h, oa, ob, oc, od, gain, wg, wb, wo, fgain, wr, br):
    T, D = h.shape
    tm = TM_MERGE
    tile = lambda i: (i, 0)
    c2 = lambda i: (0, 0)
    c3 = lambda i: (0, 0, 0)
    once = pl.Buffered(1)
    in_specs = [pl.BlockSpec((tm, D), tile)] + [pl.BlockSpec((tm, 256), tile)] * 4 + [
        pl.BlockSpec((1, D), c2),
        pl.BlockSpec(wg.shape, c3, pipeline_mode=once),
        pl.BlockSpec(wb.shape, c3, pipeline_mode=once),
        pl.BlockSpec(wo.shape, c2, pipeline_mode=once),
        pl.BlockSpec((1, D), c2),
        pl.BlockSpec(wr.shape, c2),
        pl.BlockSpec((1, LANES), c2),
    ]
    out_shape = [jax.ShapeDtypeStruct((T, D), F32), jax.ShapeDtypeStruct((T, D), F32),
                 jax.ShapeDtypeStruct((T, LANES), F32), jax.ShapeDtypeStruct((8, LANES), F32)]
    out_specs = [pl.BlockSpec((tm, D), tile), pl.BlockSpec((tm, D), tile), pl.BlockSpec((tm, LANES), tile),
                 pl.BlockSpec((8, LANES), c2)]
    return pl.pallas_call(
        _merge_kernel, grid=(T // tm,), in_specs=in_specs, out_specs=out_specs, out_shape=out_shape,
        scratch_shapes=[pltpu.VMEM((8, LANES), F32)],
        compiler_params=_cparams(("arbitrary",)),
    )(h, oa, ob, oc, od, gain, wg, wb, wo, fgain, wr, br)


def _row_copy(src, src_row, dst, dst_row, sem):
    return pltpu.make_async_copy(src.at[pl.ds(src_row, 1)], dst.at[pl.ds(dst_row, 1)], sem)


def _dest_kernel(rt_ref, ps_ref, d_ref):
    rt = rt_ref[...]
    lane = _iota(rt.shape, 1)
    lanef = lane.astype(F32)
    ps = ps_ref[...]

    def dest(e_lane, r_lane):
        start = jnp.sum(jnp.where(lanef == rt[:, e_lane:e_lane + 1], ps, 0.0), axis=-1, keepdims=True)
        return start + rt[:, r_lane:r_lane + 1]

    d = jnp.where(lane == 0, dest(RT_E1, RT_R1), jnp.where(lane == 1, dest(RT_E2, RT_R2), 0.0))
    d_ref[...] = d.astype(I32)


def _dest_call(rt, pstart_row):
    T = rt.shape[0]
    tm = TM_MERGE
    return pl.pallas_call(
        _dest_kernel, grid=(T // tm,),
        in_specs=[pl.BlockSpec((tm, LANES), lambda i: (i, 0)), pl.BlockSpec((1, LANES), lambda i: (0, 0))],
        out_specs=pl.BlockSpec((tm, LANES), lambda i: (i, 0)), out_shape=jax.ShapeDtypeStruct((T, LANES), I32),
        compiler_params=_cparams(("arbitrary",)),
    )(rt, pstart_row)


DMA_ISSUE_UNROLL = 8


def _dispatch_kernel(d1_ref, d2_ref, xn_ref, xs_in_ref, xs_ref, sem):
    del xs_in_ref
    base = pl.program_id(0) * TD_DISPATCH

    def copies(r):
        t = base + r
        return (_row_copy(xn_ref, r, xs_ref, d1_ref[t], sem), _row_copy(xn_ref, r, xs_ref, d2_ref[t], sem))

    def start(r, c):
        for priority, cp in enumerate(copies(r)):
            cp.start(priority=priority)
        return c

    def wait(r, c):
        for cp in copies(r):
            cp.wait()
        return c

    lax.fori_loop(0, TD_DISPATCH, start, 0, unroll=DMA_ISSUE_UNROLL)
    lax.fori_loop(0, TD_DISPATCH, wait, 0, unroll=DMA_ISSUE_UNROLL)


def _dispatch_call(d1, d2, xn, xs_zero):
    T, D = xn.shape
    any_spec = pl.BlockSpec(memory_space=pl.ANY)
    grid_spec = pltpu.PrefetchScalarGridSpec(
        num_scalar_prefetch=2, grid=(T // TD_DISPATCH,),
        in_specs=[pl.BlockSpec((TD_DISPATCH, D), lambda i, *_: (i, 0)), any_spec], out_specs=any_spec,
        scratch_shapes=[pltpu.SemaphoreType.DMA(())])
    return pl.pallas_call(
        _dispatch_kernel, grid_spec=grid_spec, out_shape=jax.ShapeDtypeStruct(xs_zero.shape, xs_zero.dtype),
        input_output_aliases={3: 0},
        compiler_params=pltpu.CompilerParams(dimension_semantics=("arbitrary",), has_side_effects=True,
                                             vmem_limit_bytes=VMEM_LIMIT),
    )(d1, d2, xn, xs_zero)


def _expert_kernel(be_ref, used_ref, x_ref, wg_ref, wu_ref, wd_ref, y_ref):
    i = pl.program_id(0)

    @pl.when(i < used_ref[0])
    def _():
        x = x_ref[...].astype(BF16)
        g = _dot(x, wg_ref[0, 0].astype(BF16))
        u = _dot(x, wu_ref[0, 0].astype(BF16))
        act = g / (1.0 + jnp.exp(-g)) * u
        y_ref[...] = _dot(act.astype(BF16), wd_ref[0, 0].astype(BF16))

    @pl.when(i >= used_ref[0])
    def _():
        y_ref[...] = jnp.zeros_like(y_ref)


def _expert_call(blk_expert, used, xs, wg, wu, wd, layer):
    rows, D = xs.shape
    rb = MOE_ROW_BLOCK
    DE = wg.shape[3]
    xmap = lambda i, be, used: (jnp.maximum(jnp.minimum(i, used[0] - 1), 0), 0)
    wmap = lambda i, be, used: (layer, be[i], 0, 0)
    grid_spec = pltpu.PrefetchScalarGridSpec(
        num_scalar_prefetch=2, grid=(rows // rb,),
        in_specs=[pl.BlockSpec((rb, D), xmap), pl.BlockSpec((1, 1, D, DE), wmap), pl.BlockSpec((1, 1, D, DE), wmap),
                  pl.BlockSpec((1, 1, DE, D), wmap)],
        out_specs=pl.BlockSpec((rb, D), lambda i, be, used: (i, 0)))
    return pl.pallas_call(
        _expert_kernel, grid_spec=grid_spec, out_shape=jax.ShapeDtypeStruct((rows, D), F32),
        compiler_params=_cparams(("arbitrary",)),
    )(blk_expert, used, xs, wg, wu, wd)


def _combine_kernel(final_norm, d1_ref, d2_ref, h_ref, rt_ref, gain_ref, ys_ref, o_ref, buf_ref, sem):
    base = pl.program_id(0) * TC_COMBINE

    def copies(r):
        t = base + r
        return (_row_copy(ys_ref, d1_ref[t], buf_ref.at[0], r, sem),
                _row_copy(ys_ref, d2_ref[t], buf_ref.at[1], r, sem))

    def start(r, c):
        for priority, cp in enumerate(copies(r)):
            cp.start(priority=priority)
        return c

    def wait(r, c):
        for cp in copies(r):
            cp.wait()
        return c

    lax.fori_loop(0, TC_COMBINE, start, 0, unroll=DMA_ISSUE_UNROLL)
    lax.fori_loop(0, TC_COMBINE, wait, 0, unroll=DMA_ISSUE_UNROLL)
    rt = rt_ref[...]
    out = h_ref[...] + rt[:, RT_W1:RT_W1 + 1] * buf_ref[0] + rt[:, RT_W2:RT_W2 + 1] * buf_ref[1]
    if final_norm:
        out = out * lax.rsqrt(jnp.mean(out * out, axis=-1, keepdims=True) + NORM_EPS) * gain_ref[...]
    o_ref[...] = out


def _combine_call(d1, d2, h, rt, gain, ys, final_norm):
    T, D = h.shape
    tc = TC_COMBINE
    tile = lambda i, *_: (i, 0)
    grid_spec = pltpu.PrefetchScalarGridSpec(
        num_scalar_prefetch=2, grid=(T // tc,),
        in_specs=[pl.BlockSpec((tc, D), tile), pl.BlockSpec((tc, LANES), tile),
                  pl.BlockSpec((1, D), lambda i, *_: (0, 0)), pl.BlockSpec(memory_space=pl.ANY)],
        out_specs=pl.BlockSpec((tc, D), tile),
        scratch_shapes=[pltpu.VMEM((2, tc, D), F32), pltpu.SemaphoreType.DMA(())])
    return pl.pallas_call(
        functools.partial(_combine_kernel, final_norm), grid_spec=grid_spec,
        out_shape=jax.ShapeDtypeStruct((T, D), F32),
        compiler_params=_cparams(("arbitrary",)),
    )(d1, d2, h, rt, gain, ys)


def _in_proj_columns():
    widths = (('q_a', 256), ('k_cmp', 64), ('v_cmp', 64), ('k_sel', 64), ('v_sel', 64), ('k_win', 64), ('v_win', 64),
              ('g_a', 12), ('q_b', 256), ('k_b', 256), ('v_b', 256), ('f_b', 4), ('q_c', 256), ('ckv_c', 128),
              ('qi_c', 256), ('ki_c', 64), ('wi_c', 4), ('q_d', 256), ('k_d', 128), ('v_d', 128))
    off, cols = 0, {}
    for name, w in widths:
        cols[name] = np.arange(off, off + w)
        off += w
    cat = lambda names: np.concatenate([cols[n] for n in names])
    rope = cat(('q_a', 'q_c', 'qi_c', 'q_d', 'k_d', 'k_cmp', 'k_sel', 'k_win', 'ki_c'))
    plain = cat(('q_b', 'k_b', 'v_b', 'v_d', 'ckv_c', 'v_cmp', 'v_sel', 'v_win'))
    small = cat(('g_a', 'f_b', 'wi_c'))
    assert rope.size == N_ROPE_COLS and plain.size == N_PLAIN_COLS
    return np.concatenate([rope, plain, small]), N_SMALL_COLS - small.size


def _static_tables(S):
    n_cmp_pad = S // NSA_CMP_STRIDE
    n_blk = S // NSA_SEL_BLOCK
    c0 = np.arange(n_cmp_pad) * NSA_CMP_STRIDE
    b0 = np.arange(n_blk) * NSA_SEL_BLOCK
    ov = ((c0[:, None] < b0[None, :] + NSA_SEL_BLOCK) & (c0[:, None] + NSA_CMP_LEN > b0[None, :])).astype(np.float32)
    overlap = np.ascontiguousarray(ov.T)
    expand = np.zeros((LANES, S), np.float32)
    expand[np.arange(S) // NSA_SEL_BLOCK, np.arange(S)] = 1.0
    strict_lower = (np.arange(LANES)[:, None] > np.arange(LANES)[None, :]).astype(np.float32)
    cntmat = np.concatenate([strict_lower, np.ones((16, LANES), np.float32)], axis=0)
    return jnp.asarray(overlap, BF16), jnp.asarray(expand, BF16), jnp.asarray(cntmat, BF16)


def kernel(x, positions, attn_norm, w_in, nsa_pe_k, nsa_w1_k, nsa_w2_k, nsa_pe_v, nsa_w1_v, nsa_w2_v, fox_forget_bias, dsa_kv_norm, dsa_w_ukv, swa_sinks, w_branch, w_gate, w_out, ffn_norm, moe_w_group, moe_b_group, moe_w_expert, moe_b_expert, moe_w_gate, moe_w_up, moe_w_down, final_norm):
    B, S, D = x.shape
    T = B * S
    depth = w_in.shape[0]
    perm, n_pad = _in_proj_columns()
    overlap, expand, cntmat = _static_tables(S)
    half = HEAD_DIM // 2
    inv_freq = ROPE_THETA ** (-jnp.arange(half, dtype=F32) / half)
    invf = jnp.tile(inv_freq, LANES // half).reshape(1, LANES)
    pos = positions.reshape(T, 1).astype(I32)
    small_pad = lambda v, off: jnp.zeros((1, LANES), F32).at[0, off:off + v.shape[0]].set(v.astype(F32))

    n_rows = -(-(T * 2 + MOE_N_EXPERTS * (MOE_ROW_BLOCK - 1)) // MOE_ROW_BLOCK) * MOE_ROW_BLOCK
    h = x.reshape(T, D)
    for l in range(depth):
        w = jnp.pad(w_in[l][:, perm], ((0, 0), (0, n_pad))).astype(BF16)
        (qa, qc, qi, qd, kd, kcmp, ksel, kwin, ki, qb, kb, vb, vd, vcmp, vsel, vwin, kc, vc, sm) = _proj_call(
            h, pos, invf, attn_norm[l].reshape(1, D), w, small_pad(fox_forget_bias[l], SM_F0),
            dsa_kv_norm[l].reshape(1, DSA_KV_RANK), dsa_w_ukv[l].astype(BF16), B, S)

        cw = NSA_CMP_STRIDE * HEAD_DIM
        pe_rows = lambda pe: jnp.broadcast_to(pe.reshape(1, 2 * cw), (8, 2 * cw)).astype(BF16)
        kcc, vcc = _cmp_call(kcmp.reshape(T // NSA_CMP_STRIDE, cw), vcmp.reshape(T // NSA_CMP_STRIDE, cw),
                             pe_rows(nsa_pe_k[l]), nsa_w1_k[l].astype(BF16), nsa_w2_k[l].astype(BF16),
                             pe_rows(nsa_pe_v[l]), nsa_w1_v[l].astype(BF16), nsa_w2_v[l].astype(BF16), B, S)
        o_a = _nsa_call(qa, sm, kcc, vcc, overlap, expand, ksel, vsel, kwin, vwin, B, S)

        key_bias = sm[:, SM_F0:SM_F0 + N_HEADS].reshape(B, S, N_HEADS).transpose(0, 2, 1)
        o_b = _fox_call(qb, key_bias, kb, vb, B, S)
        idx_w = sm[:, SM_W0:SM_W0 + N_HEADS].reshape(B, S, N_HEADS).transpose(0, 2, 1)
        o_c = _dsa_call(qc, qi, idx_w, ki, cntmat, kc, vc, B, S)
        o_d = _swa_call(qd, small_pad(swa_sinks[l], 0), kd, vd, B, S)

        wr = jnp.zeros((D, LANES), F32).at[:, :MOE_GROUPS].set(moe_w_group[l]) \
            .at[:, ROUTER_E0:ROUTER_E0 + MOE_N_EXPERTS].set(moe_w_expert[l])
        br = jnp.zeros((1, LANES), F32).at[0, :MOE_GROUPS].set(moe_b_group[l]) \
            .at[0, ROUTER_E0:ROUTER_E0 + MOE_N_EXPERTS].set(moe_b_expert[l])
        h2, xn, rt, cnt = _merge_call(h, o_a, o_b, o_c, o_d, attn_norm[l].reshape(1, D), w_gate[l].astype(BF16),
                                      w_branch[l].astype(BF16), w_out[l].astype(BF16), ffn_norm[l].reshape(1, D),
                                      wr, br)

        counts = cnt[0, ROUTER_E0:ROUTER_E0 + MOE_N_EXPERTS].astype(I32)
        padded = (counts + MOE_ROW_BLOCK - 1) // MOE_ROW_BLOCK * MOE_ROW_BLOCK
        p_end = jnp.cumsum(padded)
        p_start = p_end - padded
        n_blocks = n_rows // MOE_ROW_BLOCK
        blk_first_row = jnp.arange(n_blocks, dtype=I32) * MOE_ROW_BLOCK
        blk_expert = jnp.minimum(jnp.sum((p_end[None, :] <= blk_first_row[:, None]).astype(I32), axis=1),
                                 MOE_N_EXPERTS - 1).astype(I32)
        used = (p_end[-1:] // MOE_ROW_BLOCK).astype(I32)
        dest = _dest_call(rt, small_pad(p_start, 0))
        d1, d2 = dest[:, 0], dest[:, 1]

        xs = _dispatch_call(d1, d2, xn, jnp.zeros((n_rows, D), F32))
        ys = _expert_call(blk_expert, used, xs, moe_w_gate, moe_w_up, moe_w_down, l)
        last = l == depth - 1
        h = _combine_call(d1, d2, h2, rt, final_norm.reshape(1, D), ys, last)
    return h.reshape(B, S, D)
```

```python
import functools
import math

import numpy as np
import jax
import jax.numpy as jnp
from jax import lax
from jax.experimental import pallas as pl
from jax.experimental.pallas import tpu as pltpu

F32 = jnp.float32
BF16 = jnp.bfloat16
I32 = jnp.int32

HEAD_DIM = 64
N_HEADS = 4
MIX_WIDTH = N_HEADS * HEAD_DIM
ROPE_THETA = 10000.0
NORM_EPS = 1e-6
NEG_INF = -1e30
INT_MIN = -2 ** 31
LOG2E = float(np.log2(np.e))

NSA_CMP_LEN = 32
NSA_CMP_STRIDE = 16
NSA_CMP_HIDDEN = 256
NSA_SEL_BLOCK = 64
NSA_N_SEL = 16
NSA_WINDOW = 512
NSA_FORCE_SCORE = 1e4
DSA_KV_RANK = 128
DSA_TOPK_MAX = 256
DSA_COUNT_CHAINS = 4
SWA_WINDOW = 128
SWA_KV_HEADS = 2
MOE_GROUPS = 8
MOE_EXPERTS_PER_GROUP = 8
MOE_N_EXPERTS = 64
MOE_D_EXPERT = 256

LANES = 128
VMEM_LIMIT = 48 * 1024 * 1024

N_ROPE_COLS = 1408
N_PLAIN_COLS = 1216
N_SMALL_COLS = 128
N_PROJ_COLS = N_ROPE_COLS + N_PLAIN_COLS + N_SMALL_COLS
SM_GATE0, SM_F0, SM_W0 = 0, 12, 16

TM_PROJ = 512
PROJ_CHUNK = 128
TQ = 256
TM_MERGE = 512
MERGE_CHUNK = 128
MOE_ROW_BLOCK = 256
TD_DISPATCH = 256
TC_COMBINE = 256


def _cparams(sem):
    return pltpu.CompilerParams(dimension_semantics=sem, vmem_limit_bytes=VMEM_LIMIT)


def _dot(a, b):
    return jnp.dot(a, b, preferred_element_type=F32)


def _dot_t(a, b):
    return lax.dot_general(a, b, (((1,), (1,)), ((), ())), preferred_element_type=F32)


def _iota(shape, dim):
    return lax.broadcasted_iota(I32, shape, dim)


def _rope_slab(x, cos, sin_signed, first_half):
    rot = jnp.where(first_half, pltpu.roll(x, 96, 1), pltpu.roll(x, 32, 1))
    return x * cos + rot * sin_signed


def _proj_kernel(h_ref, pos_ref, invf_ref, gain_ref, w_ref, fbias_ref, kvn_ref, wukv_ref,
                 qa_ref, qc_ref, qi_ref, qd_ref, kd_ref, kcmp_ref, ksel_ref, kwin_ref, ki_ref,
                 qb_ref, kb_ref, vb_ref, vd_ref, vcmp_ref, vsel_ref, vwin_ref, kc_ref, vc_ref,
                 sm_ref, carry_ref):
    j = pl.program_id(1)
    tm = h_ref.shape[0]
    tc = PROJ_CHUNK

    @pl.when(j == 0)
    def _():
        carry_ref[...] = jnp.zeros_like(carry_ref)

    lane = _iota((tc, LANES), 1)
    first_half = (lane % HEAD_DIM) < (HEAD_DIM // 2)
    is_f = (lane >= SM_F0) & (lane < SM_W0)
    tri = (_iota((tc, tc), 0) >= _iota((tc, tc), 1)).astype(F32)
    scale = HEAD_DIM ** -0.5
    softmax_scale = scale * LOG2E

    for rows in (slice(c * tc, (c + 1) * tc) for c in range(tm // tc)):
        h = h_ref[rows, :]
        hn = h * lax.rsqrt(jnp.mean(h * h, axis=-1, keepdims=True) + NORM_EPS) * gain_ref[...]
        hb = hn.astype(BF16)

        ang = pos_ref[rows, :].astype(F32) * invf_ref[...]
        cos = jnp.cos(ang)
        sin = jnp.sin(ang)
        sin_signed = jnp.where(first_half, -sin, sin)
        rope = lambda x: _rope_slab(x, cos, sin_signed, first_half)

        zr = _dot(hb, w_ref[:, 0:N_ROPE_COLS])
        slab = lambda k: rope(zr[:, k * LANES:(k + 1) * LANES])
        for q_ref, k0, sc in ((qa_ref, 0, softmax_scale), (qc_ref, 2, softmax_scale), (qi_ref, 4, scale),
                              (qd_ref, 6, softmax_scale)):
            q_ref[rows, :] = (jnp.concatenate([slab(k0), slab(k0 + 1)], axis=1) * sc).astype(BF16)
        s8 = slab(8).astype(BF16)
        kd_ref[0, rows, :] = s8[:, :HEAD_DIM]
        kd_ref[1, rows, :] = s8[:, HEAD_DIM:]
        s9 = slab(9).astype(BF16)
        kcmp_ref[rows, :] = s9[:, :HEAD_DIM]
        ksel_ref[rows, :] = s9[:, HEAD_DIM:]
        s10 = slab(10).astype(BF16)
        kwin_ref[rows, :] = s10[:, :HEAD_DIM]
        ki_ref[rows, :] = s10[:, HEAD_DIM:]

        zp = _dot(hb, w_ref[:, N_ROPE_COLS:N_ROPE_COLS + N_PLAIN_COLS])
        qb_ref[rows, :] = (zp[:, 0:256] * softmax_scale).astype(BF16)
        for hh in range(N_HEADS):
            kb_ref[hh, rows, :] = zp[:, 256 + hh * 64:256 + (hh + 1) * 64].astype(BF16)
            vb_ref[hh, rows, :] = zp[:, 512 + hh * 64:512 + (hh + 1) * 64].astype(BF16)
        vd_ref[0, rows, :] = zp[:, 768:832].astype(BF16)
        vd_ref[1, rows, :] = zp[:, 832:896].astype(BF16)
        ckv = zp[:, 896:1024]
        vcmp_ref[rows, :] = zp[:, 1024:1088].astype(BF16)
        vsel_ref[rows, :] = zp[:, 1088:1152].astype(BF16)
        vwin_ref[rows, :] = zp[:, 1152:1216].astype(BF16)

        ckvn = ckv * lax.rsqrt(jnp.mean(ckv * ckv, axis=-1, keepdims=True) + NORM_EPS) * kvn_ref[...]
        kvc = _dot(ckvn.astype(BF16), wukv_ref[...])
        for k in range(2):
            kk = rope(kvc[:, k * LANES:(k + 1) * LANES]).astype(BF16)
            kc_ref[2 * k, rows, :] = kk[:, :HEAD_DIM]
            kc_ref[2 * k + 1, rows, :] = kk[:, HEAD_DIM:]
        for hh in range(N_HEADS):
            vc_ref[hh, rows, :] = kvc[:, 256 + hh * 64:256 + (hh + 1) * 64].astype(BF16)

        zs = _dot(hb, w_ref[:, N_ROPE_COLS + N_PLAIN_COLS:N_PROJ_COLS])
        sig = 1.0 / (1.0 + jnp.exp(-zs))
        xf = zs + fbias_ref[...]
        logf = jnp.minimum(xf, 0.0) - jnp.log(1.0 + jnp.exp(-jnp.abs(xf)))
        logf = jnp.where(is_f, logf, 0.0)
        cum = jnp.dot(tri, logf, preferred_element_type=F32, precision=lax.Precision.HIGHEST)
        cum = cum + carry_ref[0:1, :]
        carry_ref[0:1, :] = cum[tc - 1:tc, :]
        sm_ref[rows, :] = jnp.where(lane < SM_F0, sig,
                                    jnp.where(is_f, -LOG2E * cum, jnp.where(lane < SM_W0 + 4, 0.5 * zs, 0.0)))


def _proj_call(h, pos, invf, gain, w, fbias, kvn, wukv, B, S):
    T = B * S
    tm = TM_PROJ
    nj = S // tm
    row = lambda b, j: (b * nj + j, 0)
    hrow = lambda b, j: (0, b * nj + j, 0)
    const2 = lambda b, j: (0, 0)
    tok = lambda w_, dt: jax.ShapeDtypeStruct((T, w_), dt)
    hm = lambda n: jax.ShapeDtypeStruct((n, T, HEAD_DIM), BF16)
    out_shape = [tok(256, BF16)] * 4 + [hm(2)] + [tok(64, BF16)] * 4 + [tok(256, BF16), hm(4), hm(4), hm(2)] \
        + [tok(64, BF16)] * 3 + [hm(4), hm(4), tok(LANES, F32)]
    spec_tok = lambda w_: pl.BlockSpec((tm, w_), row)
    spec_hm = lambda n: pl.BlockSpec((n, tm, HEAD_DIM), hrow)
    out_specs = [spec_tok(256)] * 4 + [spec_hm(2)] + [spec_tok(64)] * 4 + [spec_tok(256), spec_hm(4), spec_hm(4), spec_hm(2)] \
        + [spec_tok(64)] * 3 + [spec_hm(4), spec_hm(4), spec_tok(LANES)]
    in_specs = [
        pl.BlockSpec((tm, h.shape[1]), row),
        pl.BlockSpec((tm, 1), row),
        pl.BlockSpec((1, LANES), const2),
        pl.BlockSpec((1, h.shape[1]), const2),
        pl.BlockSpec(w.shape, const2),
        pl.BlockSpec((1, LANES), const2),
        pl.BlockSpec((1, DSA_KV_RANK), const2),
        pl.BlockSpec(wukv.shape, const2),
    ]
    return pl.pallas_call(
        _proj_kernel, grid=(B, nj), in_specs=in_specs, out_specs=out_specs, out_shape=out_shape,
        scratch_shapes=[pltpu.VMEM((8, LANES), F32)],
        compiler_params=_cparams(("arbitrary", "arbitrary")),
    )(h, pos, invf, gain, w, fbias, kvn, wukv)


def _gelu_tanh(x):
    return 0.5 * x * (1.0 + jnp.tanh(np.sqrt(2.0 / np.pi).astype(np.float32) * (x + 0.044715 * (x * x * x))))


def _cmp_kernel(ck_ref, cv_ref, pek_ref, w1k_ref, w2k_ref, pev_ref, w1v_ref, w2v_ref, kc_ref, vc_ref):
    half = NSA_CMP_STRIDE * HEAD_DIM
    for c_ref, pe_ref, w1_ref, w2_ref, o_ref in ((ck_ref, pek_ref, w1k_ref, w2k_ref, kc_ref),
                                                 (cv_ref, pev_ref, w1v_ref, w2v_ref, vc_ref)):
        c = c_ref[...]
        a = _dot(c, w1_ref[0:half, :])
        b = _dot(c, w1_ref[half:2 * half, :])
        n = b.shape[0]
        pe_h = _dot(pe_ref[...], w1_ref[...])
        hid = a + pltpu.roll(b, n - 1, 0) + pe_h[0:1, :]
        o_ref[...] = _dot(_gelu_tanh(hid).astype(BF16), w2_ref[...]).astype(BF16)


def _cmp_call(ck, cv, pek, w1k, w2k, pev, w1v, w2v, B, S):
    nch = S // NSA_CMP_STRIDE
    blk = lambda b: (b, 0)
    const = lambda b: (0, 0)
    cw = NSA_CMP_STRIDE * HEAD_DIM
    in_specs = [pl.BlockSpec((nch, cw), blk), pl.BlockSpec((nch, cw), blk)]
    for _ in range(2):
        in_specs += [pl.BlockSpec((8, 2 * cw), const), pl.BlockSpec((2 * cw, NSA_CMP_HIDDEN), const),
                     pl.BlockSpec((NSA_CMP_HIDDEN, HEAD_DIM), const)]
    out = jax.ShapeDtypeStruct((B * nch, HEAD_DIM), BF16)
    return pl.pallas_call(
        _cmp_kernel, grid=(B,), in_specs=in_specs,
        out_specs=[pl.BlockSpec((nch, HEAD_DIM), blk)] * 2, out_shape=[out, out],
        compiler_params=_cparams(("arbitrary",)),
    )(ck, cv, pek, w1k, w2k, pev, w1v, w2v)


def _attend(q, segments, extra_logit=None):
    scores = []
    for k, _, bias, allowed in segments:
        s = _dot_t(q, k)
        if bias is not None:
            s = s + bias
        if allowed is not None:
            s = jnp.where(allowed, s, NEG_INF)
        scores.append(s)
    m = functools.reduce(jnp.maximum, [jnp.max(s, axis=-1, keepdims=True) for s in scores])
    if extra_logit is not None:
        m = jnp.maximum(m, extra_logit)
    den = jnp.exp2(extra_logit - m) if extra_logit is not None else 0.0
    o = 0.0
    for s, (_, v, _, _) in zip(scores, segments):
        e = jnp.exp2(s - m)
        den = den + jnp.sum(e, axis=-1, keepdims=True)
        o = o + _dot(e.astype(BF16), v)
    return o / den


def _causal_split(i, tq, nk, S):
    n_full = nk - S // N_CAUSAL_CLASSES
    t_col = i * tq + _iota((tq, 1), 0)
    tail_causal = (n_full + _iota((1, nk - n_full), 1)) <= t_col
    return n_full, tail_causal


N_CAUSAL_CLASSES = 8


def _for_causal_class(i, tq, S, body):
    step = S // N_CAUSAL_CLASSES
    cls = (i * tq + tq - 1) // step
    for c in range(N_CAUSAL_CLASSES):
        pl.when(cls == c)(functools.partial(body, (c + 1) * step))


def _nsa_kernel(q_ref, sm_ref, kc_ref, vc_ref, ov_ref, ex_ref, ksel_ref, vsel_ref, kwin_ref, vwin_ref, o_ref):
    i = pl.program_id(1)
    tq = q_ref.shape[0]
    S = ksel_ref.shape[0]
    ncmp = kc_ref.shape[0]
    q = q_ref[...]
    sm = sm_ref[...]
    t_col = i * tq + _iota((tq, 1), 0)

    c_end = _iota((1, ncmp), 1) * NSA_CMP_STRIDE + (NSA_CMP_LEN - 1)
    allowed_c = c_end <= t_col
    kc = kc_ref[...]
    vc = vc_ref[...]
    o_cmp = []
    p_cmp = []
    for hh in range(N_HEADS):
        qh = q[:, hh * 64:(hh + 1) * 64]
        s = jnp.where(allowed_c, _dot_t(qh, kc), NEG_INF)
        m = jnp.max(s, axis=-1, keepdims=True)
        e = jnp.where(allowed_c, jnp.exp2(s - m), 0.0)
        den = jnp.sum(e, axis=-1, keepdims=True)
        p = e * jnp.where(den > 0.0, 1.0 / den, 0.0)
        p_cmp.append(p.astype(BF16))
        o_cmp.append(_dot(p_cmp[-1], vc))

    n_blk = S // NSA_SEL_BLOCK
    n_sel = min(NSA_N_SEL, n_blk)

    def unselected_block_bias():
        imp = sum(_dot_t(ov_ref[...], p) for p in p_cmp)
        jb = _iota((n_blk, tq), 0)
        cur = (i * tq + _iota((1, tq), 1)) // NSA_SEL_BLOCK
        forced = (jb == 0) | (jb == cur) | (jb == cur - 1)
        imp = jnp.where(forced, NSA_FORCE_SCORE, imp)
        imp = jnp.where(jb > cur, -NSA_FORCE_SCORE, imp)
        rank = jnp.zeros((n_blk, tq), F32)
        for jp in range(n_blk):
            row = imp[jp:jp + 1, :]
            ahead = (row > imp) | ((row == imp) & (jp < jb))
            rank = rank + jnp.where(ahead, 1.0, 0.0)
        unsel = jnp.transpose(jnp.where(rank < float(n_sel), 0.0, NEG_INF))
        return jnp.concatenate([unsel, jnp.zeros((tq, LANES - n_blk), F32)], axis=1).astype(BF16)

    nband = NSA_WINDOW + tq
    start = pl.multiple_of(jnp.maximum(i * tq - NSA_WINDOW, 0), math.gcd(tq, NSA_WINDOW))
    kwin = kwin_ref[pl.ds(start, nband), :]
    vwin = vwin_ref[pl.ds(start, nband), :]
    s_band = start + _iota((1, nband), 1)
    allowed_win = (s_band <= t_col) & (s_band > t_col - NSA_WINDOW)
    gate = lambda hh, k: sm[:, SM_GATE0 + 3 * hh + k:SM_GATE0 + 3 * hh + k + 1]
    partial_out = []
    for hh in range(N_HEADS):
        o_win = _attend(q[:, hh * 64:(hh + 1) * 64], [(kwin, vwin, None, allowed_win)])
        partial_out.append(gate(hh, 0) * o_cmp[hh] + gate(hh, 2) * o_win)

    def selected(nk):
        n_full, tail_causal = _causal_split(i, tq, nk, S)
        if nk <= n_sel * NSA_SEL_BLOCK:
            part_bias = lambda lo, hi: None
        else:
            bias = _dot(unselected_block_bias(), ex_ref[:, 0:nk])
            part_bias = lambda lo, hi: bias[:, lo:hi]
        segments = [(ksel_ref[n_full:nk, :], vsel_ref[n_full:nk, :], part_bias(n_full, nk), tail_causal)]
        if n_full:
            segments.append((ksel_ref[0:n_full, :], vsel_ref[0:n_full, :], part_bias(0, n_full), None))
        outs = []
        for hh in range(N_HEADS):
            o_sel = _attend(q[:, hh * 64:(hh + 1) * 64], segments)
            outs.append(partial_out[hh] + gate(hh, 1) * o_sel)
        o_ref[...] = jnp.concatenate(outs, axis=1).astype(BF16)

    _for_causal_class(i, tq, S, selected)


def _nsa_call(qa, sm, kcc, vcc, overlap, expand, ksel, vsel, kwin, vwin, B, S):
    T = B * S
    nq = S // TQ
    ncmp = S // NSA_CMP_STRIDE
    tile = lambda b, i: (b * nq + i, 0)
    seq = lambda b, i: (b, 0)
    const = lambda b, i: (0, 0)
    in_specs = [pl.BlockSpec((TQ, 256), tile), pl.BlockSpec((TQ, LANES), tile),
                pl.BlockSpec((ncmp, 64), seq), pl.BlockSpec((ncmp, 64), seq),
                pl.BlockSpec(overlap.shape, const), pl.BlockSpec(expand.shape, const)] \
        + [pl.BlockSpec((S, 64), seq)] * 4
    return pl.pallas_call(
        _nsa_kernel, grid=(B, nq), in_specs=in_specs, out_specs=pl.BlockSpec((TQ, 256), tile),
        out_shape=jax.ShapeDtypeStruct((T, 256), BF16),
        compiler_params=_cparams(("arbitrary", "arbitrary")),
    )(qa, sm, kcc, vcc, overlap, expand, ksel, vsel, kwin, vwin)


def _fox_kernel(q_ref, nbias_ref, k_ref, v_ref, o_ref):
    i = pl.program_id(1)
    tq = q_ref.shape[0]
    S = k_ref.shape[1]
    q = q_ref[...]

    def attend(nk):
        n_full, tail_causal = _causal_split(i, tq, nk, S)
        outs = []
        for hh in range(N_HEADS):
            segments = [(k_ref[hh, n_full:nk, :], v_ref[hh, n_full:nk, :], nbias_ref[0, hh:hh + 1, n_full:nk],
                         tail_causal)]
            if n_full:
                segments.append((k_ref[hh, 0:n_full, :], v_ref[hh, 0:n_full, :], nbias_ref[0, hh:hh + 1, 0:n_full],
                                 None))
            outs.append(_attend(q[:, hh * 64:(hh + 1) * 64], segments))
        o_ref[...] = jnp.concatenate(outs, axis=1).astype(BF16)

    _for_causal_class(i, tq, S, attend)


def _fox_call(qb, nbias, kb, vb, B, S):
    T = B * S
    nq = S // TQ
    tile = lambda b, i: (b * nq + i, 0)
    in_specs = [pl.BlockSpec((TQ, 256), tile),
                pl.BlockSpec((1, N_HEADS, S), lambda b, i: (b, 0, 0)),
                pl.BlockSpec((N_HEADS, S, 64), lambda b, i: (0, b, 0)),
                pl.BlockSpec((N_HEADS, S, 64), lambda b, i: (0, b, 0))]
    return pl.pallas_call(
        _fox_kernel, grid=(B, nq), in_specs=in_specs, out_specs=pl.BlockSpec((TQ, 256), tile),
        out_shape=jax.ShapeDtypeStruct((T, 256), BF16),
        compiler_params=_cparams(("arbitrary", "arbitrary")),
    )(qb, nbias, kb, vb)


def _dsa_kernel(q_ref, qi_ref, wrow_ref, ki_ref, cnt_ref, k_ref, v_ref, o_ref, score_ref):
    i = pl.program_id(1)
    tq = q_ref.shape[0]
    S = ki_ref.shape[0]
    qi = qi_ref[...]
    t_col = i * tq + _iota((tq, 1), 0)
    top_k = min(DSA_TOPK_MAX, S // 4)
    q = q_ref[...]

    def attend(nk, bias, allowed):
        outs = [_attend(q[:, hh * 64:(hh + 1) * 64], [(k_ref[hh, 0:nk, :], v_ref[hh, 0:nk, :], bias, allowed)])
                for hh in range(N_HEADS)]
        o_ref[...] = jnp.concatenate(outs, axis=1).astype(BF16)

    def float_of_ordered_bits(u):
        k = u ^ INT_MIN
        return lax.bitcast_convert_type(jnp.where(k >= 0, k, k ^ 0x7FFFFFFF), F32)

    def select_and_attend(nk):
        t_row = i * tq + _iota((1, tq), 1)
        k_eff = jnp.minimum(t_row + 1, top_k).astype(F32)
        w_rows = wrow_ref[0]
        score = jnp.zeros((nk, tq), F32)
        for hh in range(N_HEADS):
            lg = _dot_t(ki_ref[0:nk, :], qi[:, hh * 64:(hh + 1) * 64])
            score = score + w_rows[hh:hh + 1, :] * jnp.maximum(lg, 0.0)
        score_ref[0:nk, :] = jnp.where(_iota((nk, 1), 0) <= t_row, score, -jnp.inf)

        def search(it, u):
            cand = u | lax.shift_left(jnp.int32(1), 31 - it)
            thr = float_of_ordered_bits(cand)
            part = nk // DSA_COUNT_CHAINS
            cnt = sum(jnp.sum(jnp.where(score_ref[g * part:(g + 1) * part, :] >= thr, 1.0, 0.0), axis=0, keepdims=True)
                      for g in range(DSA_COUNT_CHAINS))
            return jnp.where(cnt >= k_eff, cand, u)

        u = lax.fori_loop(0, 32, search, jnp.zeros((1, tq), I32))
        thr = float_of_ordered_bits(u)
        score = score_ref[0:nk, :]
        gt = score > thr
        eq = score == thr
        need = k_eff - jnp.sum(jnp.where(gt, 1.0, 0.0), axis=0, keepdims=True)
        eqb = jnp.where(eq, 1.0, 0.0).astype(BF16)
        run = jnp.zeros((1, tq), F32)
        bias_chunks = []
        for c in range(nk // LANES):
            sl = slice(c * LANES, (c + 1) * LANES)
            r = _dot(cnt_ref[...], eqb[sl, :])
            prefix = r[0:LANES, :] + run
            run = run + r[LANES:LANES + 1, :]
            keep = gt[sl, :] | (eq[sl, :] & (prefix < need))
            bias_chunks.append(jnp.transpose(jnp.where(keep, 0.0, NEG_INF)))
        attend(nk, jnp.concatenate(bias_chunks, axis=1), None)

    def body(nk):
        if nk * N_CAUSAL_CLASSES == S:
            all_kept = (i + 1) * tq <= top_k
            pl.when(all_kept)(lambda: attend(nk, None, _iota((1, nk), 1) <= t_col))
            pl.when(jnp.logical_not(all_kept))(lambda: select_and_attend(nk))
        else:
            select_and_attend(nk)

    _for_causal_class(i, tq, S, body)


def _dsa_call(qc, qi, w_rows, ki, cntmat, kc, vc, B, S):
    T = B * S
    nq = S // TQ
    tile = lambda b, i: (b * nq + i, 0)
    in_specs = [pl.BlockSpec((TQ, 256), tile), pl.BlockSpec((TQ, 256), tile),
                pl.BlockSpec((1, N_HEADS, TQ), lambda b, i: (b, 0, i)),
                pl.BlockSpec((S, 64), lambda b, i: (b, 0)),
                pl.BlockSpec(cntmat.shape, lambda b, i: (0, 0)),
                pl.BlockSpec((N_HEADS, S, 64), lambda b, i: (0, b, 0)),
                pl.BlockSpec((N_HEADS, S, 64), lambda b, i: (0, b, 0))]
    return pl.pallas_call(
        _dsa_kernel, grid=(B, nq), in_specs=in_specs, out_specs=pl.BlockSpec((TQ, 256), tile),
        out_shape=jax.ShapeDtypeStruct((T, 256), BF16),
        scratch_shapes=[pltpu.VMEM((S, TQ), F32)],
        compiler_params=_cparams(("arbitrary", "arbitrary")),
    )(qc, qi, w_rows, ki, cntmat, kc, vc)


def _swa_kernel(q_ref, sink_ref, k_ref, v_ref, o_ref):
    i = pl.program_id(1)
    tq = q_ref.shape[0]
    q = q_ref[...]
    t_col = i * tq + _iota((tq, 1), 0)
    nband = SWA_WINDOW + tq
    start = pl.multiple_of(jnp.maximum(i * tq - SWA_WINDOW, 0), math.gcd(tq, SWA_WINDOW))
    s_band = start + _iota((1, nband), 1)
    allowed = (s_band <= t_col) & (s_band > t_col - SWA_WINDOW)
    group = N_HEADS // SWA_KV_HEADS
    outs = []
    for hh in range(N_HEADS):
        k = k_ref[hh // group, pl.ds(start, nband), :]
        v = v_ref[hh // group, pl.ds(start, nband), :]
        sink = jnp.zeros((tq, 1), F32) + LOG2E * sink_ref[0:1, hh:hh + 1]
        outs.append(_attend(q[:, hh * 64:(hh + 1) * 64], [(k, v, None, allowed)], extra_logit=sink))
    o_ref[...] = jnp.concatenate(outs, axis=1).astype(BF16)


def _swa_call(qd, sinks, kd, vd, B, S):
    T = B * S
    nq = S // TQ
    tile = lambda b, i: (b * nq + i, 0)
    in_specs = [pl.BlockSpec((TQ, 256), tile), pl.BlockSpec((1, LANES), lambda b, i: (0, 0)),
                pl.BlockSpec((SWA_KV_HEADS, S, 64), lambda b, i: (0, b, 0)),
                pl.BlockSpec((SWA_KV_HEADS, S, 64), lambda b, i: (0, b, 0))]
    return pl.pallas_call(
        _swa_kernel, grid=(B, nq), in_specs=in_specs, out_specs=pl.BlockSpec((TQ, 256), tile),
        out_shape=jax.ShapeDtypeStruct((T, 256), BF16),
        compiler_params=_cparams(("arbitrary", "arbitrary")),
    )(qd, sinks, kd, vd)


RT_E1, RT_E2, RT_W1, RT_W2, RT_R1, RT_R2 = 0, 1, 2, 3, 4, 5
ROUTER_E0 = MOE_GROUPS


def _merge_kernel(h_ref, oa_ref, ob_ref, oc_ref, od_ref, gain_ref, wg_ref, wb_ref, wo_ref,
                  fgain_ref, wr_ref, br_ref, h2_ref, xn_ref, rt_ref, cnt_ref, base_ref):
    step = pl.program_id(0)
    tm = h_ref.shape[0]
    logit_chunks = []
    for rows in (slice(c * MERGE_CHUNK, (c + 1) * MERGE_CHUNK) for c in range(tm // MERGE_CHUNK)):
        h = h_ref[rows, :]
        hn = h * lax.rsqrt(jnp.mean(h * h, axis=-1, keepdims=True) + NORM_EPS) * gain_ref[...]
        hb = hn.astype(BF16)
        merged = jnp.zeros_like(h)
        for i, o_ref in enumerate((oa_ref, ob_ref, oc_ref, od_ref)):
            gate = 1.0 / (1.0 + jnp.exp(-_dot(hb, wg_ref[i])))
            merged = merged + gate * _dot(o_ref[rows, :], wb_ref[i])
        h2 = h + _dot(merged.astype(BF16), wo_ref[...])
        h2_ref[rows, :] = h2
        xn = h2 * lax.rsqrt(jnp.mean(h2 * h2, axis=-1, keepdims=True) + NORM_EPS) * fgain_ref[...]
        xn_ref[rows, :] = xn
        logit_chunks.append(jnp.dot(xn, wr_ref[...], preferred_element_type=F32, precision=lax.Precision.HIGHEST))
    logits = jnp.concatenate(logit_chunks, axis=0) + br_ref[...]

    lane = _iota((tm, LANES), 1)
    lanef = lane.astype(F32)
    big = float(LANES)
    is_g = lane < MOE_GROUPS
    gl = jnp.where(is_g, logits, NEG_INF)
    gmax = jnp.max(gl, axis=-1, keepdims=True)
    grp = jnp.min(jnp.where(is_g & (gl == gmax), lanef, big), axis=-1, keepdims=True)
    p_grp = 1.0 / jnp.sum(jnp.where(is_g, jnp.exp(gl - gmax), 0.0), axis=-1, keepdims=True)
    lo = ROUTER_E0 + grp * MOE_EXPERTS_PER_GROUP
    in_grp = (lanef >= lo) & (lanef < lo + MOE_EXPERTS_PER_GROUP)
    el = jnp.where(in_grp, logits, NEG_INF)
    v1 = jnp.max(el, axis=-1, keepdims=True)
    l1 = jnp.min(jnp.where(in_grp & (el == v1), lanef, big), axis=-1, keepdims=True)
    el2 = jnp.where(lanef == l1, NEG_INF, el)
    v2 = jnp.max(el2, axis=-1, keepdims=True)
    l2 = jnp.min(jnp.where(in_grp & (lanef != l1) & (el2 == v2), lanef, big), axis=-1, keepdims=True)
    e21 = jnp.exp(v2 - v1)
    w1 = p_grp / (1.0 + e21)
    w2 = p_grp * e21 / (1.0 + e21)

    oh1 = jnp.where(lanef == l1, 1.0, 0.0)
    oh2 = jnp.where(lanef == l2, 1.0, 0.0)

    @pl.when(step == 0)
    def _():
        base_ref[...] = jnp.zeros_like(base_ref)

    both = oh1 + oh2
    strict = (_iota((tm, tm), 0) > _iota((tm, tm), 1)).astype(BF16)
    before = _dot(strict, both.astype(BF16)) + base_ref[0:1, :]
    r1 = jnp.sum(oh1 * before, axis=-1, keepdims=True)
    r2 = jnp.sum(oh2 * before, axis=-1, keepdims=True)
    total = base_ref[0:1, :] + jnp.sum(both, axis=0, keepdims=True)
    base_ref[0:1, :] = total
    cnt_ref[...] = jnp.broadcast_to(total, cnt_ref.shape)

    rt = jnp.zeros((tm, LANES), F32)
    for ln, val in ((RT_E1, l1 - ROUTER_E0), (RT_E2, l2 - ROUTER_E0), (RT_W1, w1), (RT_W2, w2), (RT_R1, r1), (RT_R2, r2)):
        rt = jnp.where(lane == ln, val, rt)
    rt_ref[...] = rt


def _merge_call(h, oa, ob, oc, od, gain, wg, wb, wo, fgain, wr, br):
    T, D = h.shape
    tm = TM_MERGE
    tile = lambda i: (i, 0)
    c2 = lambda i: (0, 0)
    c3 = lambda i: (0, 0, 0)
    once = pl.Buffered(1)
    in_specs = [pl.BlockSpec((tm, D), tile)] + [pl.BlockSpec((tm, 256), tile)] * 4 + [
        pl.BlockSpec((1, D), c2),
        pl.BlockSpec(wg.shape, c3, pipeline_mode=once),
        pl.BlockSpec(wb.shape, c3, pipeline_mode=once),
        pl.BlockSpec(wo.shape, c2, pipeline_mode=once),
        pl.BlockSpec((1, D), c2),
        pl.BlockSpec(wr.shape, c2),
        pl.BlockSpec((1, LANES), c2),
    ]
    out_shape = [jax.ShapeDtypeStruct((T, D), F32), jax.ShapeDtypeStruct((T, D), F32),
                 jax.ShapeDtypeStruct((T, LANES), F32), jax.ShapeDtypeStruct((8, LANES), F32)]
    out_specs = [pl.BlockSpec((tm, D), tile), pl.BlockSpec((tm, D), tile), pl.BlockSpec((tm, LANES), tile),
                 pl.BlockSpec((8, LANES), c2)]
    return pl.pallas_call(
        _merge_kernel, grid=(T // tm,), in_specs=in_specs, out_specs=out_specs, out_shape=out_shape,
        scratch_shapes=[pltpu.VMEM((8, LANES), F32)],
        compiler_params=_cparams(("arbitrary",)),
    )(h, oa, ob, oc, od, gain, wg, wb, wo, fgain, wr, br)


def _row_copy(src, src_row, dst, dst_row, sem):
    return pltpu.make_async_copy(src.at[pl.ds(src_row, 1)], dst.at[pl.ds(dst_row, 1)], sem)


def _dest_kernel(rt_ref, ps_ref, d_ref):
    rt = rt_ref[...]
    lane = _iota(rt.shape, 1)
    lanef = lane.astype(F32)
    ps = ps_ref[...]

    def dest(e_lane, r_lane):
        start = jnp.sum(jnp.where(lanef == rt[:, e_lane:e_lane + 1], ps, 0.0), axis=-1, keepdims=True)
        return start + rt[:, r_lane:r_lane + 1]

    d = jnp.where(lane == 0, dest(RT_E1, RT_R1), jnp.where(lane == 1, dest(RT_E2, RT_R2), 0.0))
    d_ref[...] = d.astype(I32)


def _dest_call(rt, pstart_row):
    T = rt.shape[0]
    tm = TM_MERGE
    return pl.pallas_call(
        _dest_kernel, grid=(T // tm,),
        in_specs=[pl.BlockSpec((tm, LANES), lambda i: (i, 0)), pl.BlockSpec((1, LANES), lambda i: (0, 0))],
        out_specs=pl.BlockSpec((tm, LANES), lambda i: (i, 0)), out_shape=jax.ShapeDtypeStruct((T, LANES), I32),
        compiler_params=_cparams(("arbitrary",)),
    )(rt, pstart_row)


DMA_ISSUE_UNROLL = 8


def _dispatch_kernel(d1_ref, d2_ref, xn_ref, xs_in_ref, xs_ref, sem):
    del xs_in_ref
    base = pl.program_id(0) * TD_DISPATCH

    def copies(r):
        t = base + r
        return (_row_copy(xn_ref, r, xs_ref, d1_ref[t], sem), _row_copy(xn_ref, r, xs_ref, d2_ref[t], sem))

    def start(r, c):
        for priority, cp in enumerate(copies(r)):
            cp.start(priority=priority)
        return c

    def wait(r, c):
        for cp in copies(r):
            cp.wait()
        return c

    lax.fori_loop(0, TD_DISPATCH, start, 0, unroll=DMA_ISSUE_UNROLL)
    lax.fori_loop(0, TD_DISPATCH, wait, 0, unroll=DMA_ISSUE_UNROLL)


def _dispatch_call(d1, d2, xn, xs_zero):
    T, D = xn.shape
    any_spec = pl.BlockSpec(memory_space=pl.ANY)
    grid_spec = pltpu.PrefetchScalarGridSpec(
        num_scalar_prefetch=2, grid=(T // TD_DISPATCH,),
        in_specs=[pl.BlockSpec((TD_DISPATCH, D), lambda i, *_: (i, 0)), any_spec], out_specs=any_spec,
        scratch_shapes=[pltpu.SemaphoreType.DMA(())])
    return pl.pallas_call(
        _dispatch_kernel, grid_spec=grid_spec, out_shape=jax.ShapeDtypeStruct(xs_zero.shape, xs_zero.dtype),
        input_output_aliases={3: 0},
        compiler_params=pltpu.CompilerParams(dimension_semantics=("arbitrary",), has_side_effects=True,
                                             vmem_limit_bytes=VMEM_LIMIT),
    )(d1, d2, xn, xs_zero)


def _expert_kernel(be_ref, used_ref, x_ref, wg_ref, wu_ref, wd_ref, y_ref):
    i = pl.program_id(0)

    @pl.when(i < used_ref[0])
    def _():
        x = x_ref[...].astype(BF16)
        g = _dot(x, wg_ref[0, 0].astype(BF16))
        u = _dot(x, wu_ref[0, 0].astype(BF16))
        act = g / (1.0 + jnp.exp(-g)) * u
        y_ref[...] = _dot(act.astype(BF16), wd_ref[0, 0].astype(BF16))

    @pl.when(i >= used_ref[0])
    def _():
        y_ref[...] = jnp.zeros_like(y_ref)


def _expert_call(blk_expert, used, xs, wg, wu, wd, layer):
    rows, D = xs.shape
    rb = MOE_ROW_BLOCK
    DE = wg.shape[3]
    xmap = lambda i, be, used: (jnp.maximum(jnp.minimum(i, used[0] - 1), 0), 0)
    wmap = lambda i, be, used: (layer, be[i], 0, 0)
    grid_spec = pltpu.PrefetchScalarGridSpec(
        num_scalar_prefetch=2, grid=(rows // rb,),
        in_specs=[pl.BlockSpec((rb, D), xmap), pl.BlockSpec((1, 1, D, DE), wmap), pl.BlockSpec((1, 1, D, DE), wmap),
                  pl.BlockSpec((1, 1, DE, D), wmap)],
        out_specs=pl.BlockSpec((rb, D), lambda i, be, used: (i, 0)))
    return pl.pallas_call(
        _expert_kernel, grid_spec=grid_spec, out_shape=jax.ShapeDtypeStruct((rows, D), F32),
        compiler_params=_cparams(("arbitrary",)),
    )(blk_expert, used, xs, wg, wu, wd)


def _combine_kernel(final_norm, d1_ref, d2_ref, h_ref, rt_ref, gain_ref, ys_ref, o_ref, buf_ref, sem):
    base = pl.program_id(0) * TC_COMBINE

    def copies(r):
        t = base + r
        return (_row_copy(ys_ref, d1_ref[t], buf_ref.at[0], r, sem),
                _row_copy(ys_ref, d2_ref[t], buf_ref.at[1], r, sem))

    def start(r, c):
        for priority, cp in enumerate(copies(r)):
            cp.start(priority=priority)
        return c

    def wait(r, c):
        for cp in copies(r):
            cp.wait()
        return c

    lax.fori_loop(0, TC_COMBINE, start, 0, unroll=DMA_ISSUE_UNROLL)
    lax.fori_loop(0, TC_COMBINE, wait, 0, unroll=DMA_ISSUE_UNROLL)
    rt = rt_ref[...]
    out = h_ref[...] + rt[:, RT_W1:RT_W1 + 1] * buf_ref[0] + rt[:, RT_W2:RT_W2 + 1] * buf_ref[1]
    if final_norm:
        out = out * lax.rsqrt(jnp.mean(out * out, axis=-1, keepdims=True) + NORM_EPS) * gain_ref[...]
    o_ref[...] = out


def _combine_call(d1, d2, h, rt, gain, ys, final_norm):
    T, D = h.shape
    tc = TC_COMBINE
    tile = lambda i, *_: (i, 0)
    grid_spec = pltpu.PrefetchScalarGridSpec(
        num_scalar_prefetch=2, grid=(T // tc,),
        in_specs=[pl.BlockSpec((tc, D), tile), pl.BlockSpec((tc, LANES), tile),
                  pl.BlockSpec((1, D), lambda i, *_: (0, 0)), pl.BlockSpec(memory_space=pl.ANY)],
        out_specs=pl.BlockSpec((tc, D), tile),
        scratch_shapes=[pltpu.VMEM((2, tc, D), F32), pltpu.SemaphoreType.DMA(())])
    return pl.pallas_call(
        functools.partial(_combine_kernel, final_norm), grid_spec=grid_spec,
        out_shape=jax.ShapeDtypeStruct((T, D), F32),
        compiler_params=_cparams(("arbitrary",)),
    )(d1, d2, h, rt, gain, ys)


def _in_proj_columns():
    widths = (('q_a', 256), ('k_cmp', 64), ('v_cmp', 64), ('k_sel', 64), ('v_sel', 64), ('k_win', 64), ('v_win', 64),
              ('g_a', 12), ('q_b', 256), ('k_b', 256), ('v_b', 256), ('f_b', 4), ('q_c', 256), ('ckv_c', 128),
              ('qi_c', 256), ('ki_c', 64), ('wi_c', 4), ('q_d', 256), ('k_d', 128), ('v_d', 128))
    off, cols = 0, {}
    for name, w in widths:
        cols[name] = np.arange(off, off + w)
        off += w
    cat = lambda names: np.concatenate([cols[n] for n in names])
    rope = cat(('q_a', 'q_c', 'qi_c', 'q_d', 'k_d', 'k_cmp', 'k_sel', 'k_win', 'ki_c'))
    plain = cat(('q_b', 'k_b', 'v_b', 'v_d', 'ckv_c', 'v_cmp', 'v_sel', 'v_win'))
    small = cat(('g_a', 'f_b', 'wi_c'))
    assert rope.size == N_ROPE_COLS and plain.size == N_PLAIN_COLS
    return np.concatenate([rope, plain, small]), N_SMALL_COLS - small.size


def _static_tables(S):
    n_cmp_pad = S // NSA_CMP_STRIDE
    n_blk = S // NSA_SEL_BLOCK
    c0 = np.arange(n_cmp_pad) * NSA_CMP_STRIDE
    b0 = np.arange(n_blk) * NSA_SEL_BLOCK
    ov = ((c0[:, None] < b0[None, :] + NSA_SEL_BLOCK) & (c0[:, None] + NSA_CMP_LEN > b0[None, :])).astype(np.float32)
    overlap = np.ascontiguousarray(ov.T)
    expand = np.zeros((LANES, S), np.float32)
    expand[np.arange(S) // NSA_SEL_BLOCK, np.arange(S)] = 1.0
    strict_lower = (np.arange(LANES)[:, None] > np.arange(LANES)[None, :]).astype(np.float32)
    cntmat = np.concatenate([strict_lower, np.ones((16, LANES), np.float32)], axis=0)
    return jnp.asarray(overlap, BF16), jnp.asarray(expand, BF16), jnp.asarray(cntmat, BF16)


def kernel(x, positions, attn_norm, w_in, nsa_pe_k, nsa_w1_k, nsa_w2_k, nsa_pe_v, nsa_w1_v, nsa_w2_v, fox_forget_bias, dsa_kv_norm, dsa_w_ukv, swa_sinks, w_branch, w_gate, w_out, ffn_norm, moe_w_group, moe_b_group, moe_w_expert, moe_b_expert, moe_w_gate, moe_w_up, moe_w_down, final_norm):
    B, S, D = x.shape
    T = B * S
    depth = w_in.shape[0]
    perm, n_pad = _in_proj_columns()
    overlap, expand, cntmat = _static_tables(S)
    half = HEAD_DIM // 2
    inv_freq = ROPE_THETA ** (-jnp.arange(half, dtype=F32) / half)
    invf = jnp.tile(inv_freq, LANES // half).reshape(1, LANES)
    pos = positions.reshape(T, 1).astype(I32)
    small_pad = lambda v, off: jnp.zeros((1, LANES), F32).at[0, off:off + v.shape[0]].set(v.astype(F32))

    n_rows = -(-(T * 2 + MOE_N_EXPERTS * (MOE_ROW_BLOCK - 1)) // MOE_ROW_BLOCK) * MOE_ROW_BLOCK
    h = x.reshape(T, D)
    for l in range(depth):
        w = jnp.pad(w_in[l][:, perm], ((0, 0), (0, n_pad))).astype(BF16)
        (qa, qc, qi, qd, kd, kcmp, ksel, kwin, ki, qb, kb, vb, vd, vcmp, vsel, vwin, kc, vc, sm) = _proj_call(
            h, pos, invf, attn_norm[l].reshape(1, D), w, small_pad(fox_forget_bias[l], SM_F0),
            dsa_kv_norm[l].reshape(1, DSA_KV_RANK), dsa_w_ukv[l].astype(BF16), B, S)

        cw = NSA_CMP_STRIDE * HEAD_DIM
        pe_rows = lambda pe: jnp.broadcast_to(pe.reshape(1, 2 * cw), (8, 2 * cw)).astype(BF16)
        kcc, vcc = _cmp_call(kcmp.reshape(T // NSA_CMP_STRIDE, cw), vcmp.reshape(T // NSA_CMP_STRIDE, cw),
                             pe_rows(nsa_pe_k[l]), nsa_w1_k[l].astype(BF16), nsa_w2_k[l].astype(BF16),
                             pe_rows(nsa_pe_v[l]), nsa_w1_v[l].astype(BF16), nsa_w2_v[l].astype(BF16), B, S)
        o_a = _nsa_call(qa, sm, kcc, vcc, overlap, expand, ksel, vsel, kwin, vwin, B, S)

        key_bias = sm[:, SM_F0:SM_F0 + N_HEADS].reshape(B, S, N_HEADS).transpose(0, 2, 1)
        o_b = _fox_call(qb, key_bias, kb, vb, B, S)
        idx_w = sm[:, SM_W0:SM_W0 + N_HEADS].reshape(B, S, N_HEADS).transpose(0, 2, 1)
        o_c = _dsa_call(qc, qi, idx_w, ki, cntmat, kc, vc, B, S)
        o_d = _swa_call(qd, small_pad(swa_sinks[l], 0), kd, vd, B, S)

        wr = jnp.zeros((D, LANES), F32).at[:, :MOE_GROUPS].set(moe_w_group[l]) \
            .at[:, ROUTER_E0:ROUTER_E0 + MOE_N_EXPERTS].set(moe_w_expert[l])
        br = jnp.zeros((1, LANES), F32).at[0, :MOE_GROUPS].set(moe_b_group[l]) \
            .at[0, ROUTER_E0:ROUTER_E0 + MOE_N_EXPERTS].set(moe_b_expert[l])
        h2, xn, rt, cnt = _merge_call(h, o_a, o_b, o_c, o_d, attn_norm[l].reshape(1, D), w_gate[l].astype(BF16),
                                      w_branch[l].astype(BF16), w_out[l].astype(BF16), ffn_norm[l].reshape(1, D),
                                      wr, br)

        counts = cnt[0, ROUTER_E0:ROUTER_E0 + MOE_N_EXPERTS].astype(I32)
        padded = (counts + MOE_ROW_BLOCK - 1) // MOE_ROW_BLOCK * MOE_ROW_BLOCK
        p_end = jnp.cumsum(padded)
        p_start = p_end - padded
        n_blocks = n_rows // MOE_ROW_BLOCK
        blk_first_row = jnp.arange(n_blocks, dtype=I32) * MOE_ROW_BLOCK
        blk_expert = jnp.minimum(jnp.sum((p_end[None, :] <= blk_first_row[:, None]).astype(I32), axis=1),
                                 MOE_N_EXPERTS - 1).astype(I32)
        used = (p_end[-1:] // MOE_ROW_BLOCK).astype(I32)
        dest = _dest_call(rt, small_pad(p_start, 0))
        d1, d2 = dest[:, 0], dest[:, 1]

        xs = _dispatch_call(d1, d2, xn, jnp.zeros((n_rows, D), F32))
        ys = _expert_call(blk_expert, used, xs, moe_w_gate, moe_w_up, moe_w_down, l)
        last = l == depth - 1
        h = _combine_call(d1, d2, h2, rt, final_norm.reshape(1, D), ys, last)
    return h.reshape(B, S, D)
```

```python
import functools
import math

import numpy as np
import jax
import jax.numpy as jnp
from jax import lax
from jax.experimental import pallas as pl
from jax.experimental.pallas import tpu as pltpu

F32 = jnp.float32
BF16 = jnp.bfloat16
I32 = jnp.int32

HEAD_DIM = 64
N_HEADS = 4
MIX_WIDTH = N_HEADS * HEAD_DIM
ROPE_THETA = 10000.0
NORM_EPS = 1e-6
NEG_INF = -1e30
INT_MIN = -2 ** 31
LOG2E = float(np.log2(np.e))

NSA_CMP_LEN = 32
NSA_CMP_STRIDE = 16
NSA_CMP_HIDDEN = 256
NSA_SEL_BLOCK = 64
NSA_N_SEL = 16
NSA_WINDOW = 512
NSA_FORCE_SCORE = 1e4
DSA_KV_RANK = 128
DSA_TOPK_MAX = 256
DSA_COUNT_CHAINS = 4
SWA_WINDOW = 128
SWA_KV_HEADS = 2
MOE_GROUPS = 8
MOE_EXPERTS_PER_GROUP = 8
MOE_N_EXPERTS = 64
MOE_D_EXPERT = 256

LANES = 128
VMEM_LIMIT = 48 * 1024 * 1024

N_ROPE_COLS = 1408
N_PLAIN_COLS = 1216
N_SMALL_COLS = 128
N_PROJ_COLS = N_ROPE_COLS + N_PLAIN_COLS + N_SMALL_COLS
SM_GATE0, SM_F0, SM_W0 = 0, 12, 16

TM_PROJ = 512
PROJ_CHUNK = 128
TQ = 256
TM_MERGE = 512
MERGE_CHUNK = 128
MOE_ROW_BLOCK = 256
TD_DISPATCH = 256
TC_COMBINE = 256


def _cparams(sem):
    return pltpu.CompilerParams(dimension_semantics=sem, vmem_limit_bytes=VMEM_LIMIT)


def _dot(a, b):
    return jnp.dot(a, b, preferred_element_type=F32)


def _dot_t(a, b):
    return lax.dot_general(a, b, (((1,), (1,)), ((), ())), preferred_element_type=F32)


def _iota(shape, dim):
    return lax.broadcasted_iota(I32, shape, dim)


def _rope_slab(x, cos, sin_signed, first_half):
    rot = jnp.where(first_half, pltpu.roll(x, 96, 1), pltpu.roll(x, 32, 1))
    return x * cos + rot * sin_signed


def _proj_kernel(h_ref, pos_ref, invf_ref, gain_ref, w_ref, fbias_ref, kvn_ref, wukv_ref,
                 qa_ref, qc_ref, qi_ref, qd_ref, kd_ref, kcmp_ref, ksel_ref, kwin_ref, ki_ref,
                 qb_ref, kb_ref, vb_ref, vd_ref, vcmp_ref, vsel_ref, vwin_ref, kc_ref, vc_ref,
                 sm_ref, carry_ref):
    j = pl.program_id(1)
    tm = h_ref.shape[0]
    tc = PROJ_CHUNK

    @pl.when(j == 0)
    def _():
        carry_ref[...] = jnp.zeros_like(carry_ref)

    lane = _iota((tc, LANES), 1)
    first_half = (lane % HEAD_DIM) < (HEAD_DIM // 2)
    is_f = (lane >= SM_F0) & (lane < SM_W0)
    tri = (_iota((tc, tc), 0) >= _iota((tc, tc), 1)).astype(F32)
    scale = HEAD_DIM ** -0.5
    softmax_scale = scale * LOG2E

    for rows in (slice(c * tc, (c + 1) * tc) for c in range(tm // tc)):
        h = h_ref[rows, :]
        hn = h * lax.rsqrt(jnp.mean(h * h, axis=-1, keepdims=True) + NORM_EPS) * gain_ref[...]
        hb = hn.astype(BF16)

        ang = pos_ref[rows, :].astype(F32) * invf_ref[...]
        cos = jnp.cos(ang)
        sin = jnp.sin(ang)
        sin_signed = jnp.where(first_half, -sin, sin)
        rope = lambda x: _rope_slab(x, cos, sin_signed, first_half)

        zr = _dot(hb, w_ref[:, 0:N_ROPE_COLS])
        slab = lambda k: rope(zr[:, k * LANES:(k + 1) * LANES])
        for q_ref, k0, sc in ((qa_ref, 0, softmax_scale), (qc_ref, 2, softmax_scale), (qi_ref, 4, scale),
                              (qd_ref, 6, softmax_scale)):
            q_ref[rows, :] = (jnp.concatenate([slab(k0), slab(k0 + 1)], axis=1) * sc).astype(BF16)
        s8 = slab(8).astype(BF16)
        kd_ref[0, rows, :] = s8[:, :HEAD_DIM]
        kd_ref[1, rows, :] = s8[:, HEAD_DIM:]
        s9 = slab(9).astype(BF16)
        kcmp_ref[rows, :] = s9[:, :HEAD_DIM]
        ksel_ref[rows, :] = s9[:, HEAD_DIM:]
        s10 = slab(10).astype(BF16)
        kwin_ref[rows, :] = s10[:, :HEAD_DIM]
        ki_ref[rows, :] = s10[:, HEAD_DIM:]

        zp = _dot(hb, w_ref[:, N_ROPE_COLS:N_ROPE_COLS + N_PLAIN_COLS])
        qb_ref[rows, :] = (zp[:, 0:256] * softmax_scale).astype(BF16)
        for hh in range(N_HEADS):
            kb_ref[hh, rows, :] = zp[:, 256 + hh * 64:256 + (hh + 1) * 64].astype(BF16)
            vb_ref[hh, rows, :] = zp[:, 512 + hh * 64:512 + (hh + 1) * 64].astype(BF16)
        vd_ref[0, rows, :] = zp[:, 768:832].astype(BF16)
        vd_ref[1, rows, :] = zp[:, 832:896].astype(BF16)
        ckv = zp[:, 896:1024]
        vcmp_ref[rows, :] = zp[:, 1024:1088].astype(BF16)
        vsel_ref[rows, :] = zp[:, 1088:1152].astype(BF16)
        vwin_ref[rows, :] = zp[:, 1152:1216].astype(BF16)

        ckvn = ckv * lax.rsqrt(jnp.mean(ckv * ckv, axis=-1, keepdims=True) + NORM_EPS) * kvn_ref[...]
        kvc = _dot(ckvn.astype(BF16), wukv_ref[...])
        for k in range(2):
            kk = rope(kvc[:, k * LANES:(k + 1) * LANES]).astype(BF16)
            kc_ref[2 * k, rows, :] = kk[:, :HEAD_DIM]
            kc_ref[2 * k + 1, rows, :] = kk[:, HEAD_DIM:]
        for hh in range(N_HEADS):
            vc_ref[hh, rows, :] = kvc[:, 256 + hh * 64:256 + (hh + 1) * 64].astype(BF16)

        zs = _dot(hb, w_ref[:, N_ROPE_COLS + N_PLAIN_COLS:N_PROJ_COLS])
        sig = 1.0 / (1.0 + jnp.exp(-zs))
        xf = zs + fbias_ref[...]
        logf = jnp.minimum(xf, 0.0) - jnp.log(1.0 + jnp.exp(-jnp.abs(xf)))
        logf = jnp.where(is_f, logf, 0.0)
        cum = jnp.dot(tri, logf, preferred_element_type=F32, precision=lax.Precision.HIGHEST)
        cum = cum + carry_ref[0:1, :]
        carry_ref[0:1, :] = cum[tc - 1:tc, :]
        sm_ref[rows, :] = jnp.where(lane < SM_F0, sig,
                                    jnp.where(is_f, -LOG2E * cum, jnp.where(lane < SM_W0 + 4, 0.5 * zs, 0.0)))


def _proj_call(h, pos, invf, gain, w, fbias, kvn, wukv, B, S):
    T = B * S
    tm = TM_PROJ
    nj = S // tm
    row = lambda b, j: (b * nj + j, 0)
    hrow = lambda b, j: (0, b * nj + j, 0)
    const2 = lambda b, j: (0, 0)
    tok = lambda w_, dt: jax.ShapeDtypeStruct((T, w_), dt)
    hm = lambda n: jax.ShapeDtypeStruct((n, T, HEAD_DIM), BF16)
    out_shape = [tok(256, BF16)] * 4 + [hm(2)] + [tok(64, BF16)] * 4 + [tok(256, BF16), hm(4), hm(4), hm(2)] \
        + [tok(64, BF16)] * 3 + [hm(4), hm(4), tok(LANES, F32)]
    spec_tok = lambda w_: pl.BlockSpec((tm, w_), row)
    spec_hm = lambda n: pl.BlockSpec((n, tm, HEAD_DIM), hrow)
    out_specs = [spec_tok(256)] * 4 + [spec_hm(2)] + [spec_tok(64)] * 4 + [spec_tok(256), spec_hm(4), spec_hm(4), spec_hm(2)] \
        + [spec_tok(64)] * 3 + [spec_hm(4), spec_hm(4), spec_tok(LANES)]
    in_specs = [
        pl.BlockSpec((tm, h.shape[1]), row),
        pl.BlockSpec((tm, 1), row),
        pl.BlockSpec((1, LANES), const2),
        pl.BlockSpec((1, h.shape[1]), const2),
        pl.BlockSpec(w.shape, const2),
        pl.BlockSpec((1, LANES), const2),
        pl.BlockSpec((1, DSA_KV_RANK), const2),
        pl.BlockSpec(wukv.shape, const2),
    ]
    return pl.pallas_call(
        _proj_kernel, grid=(B, nj), in_specs=in_specs, out_specs=out_specs, out_shape=out_shape,
        scratch_shapes=[pltpu.VMEM((8, LANES), F32)],
        compiler_params=_cparams(("arbitrary", "arbitrary")),
    )(h, pos, invf, gain, w, fbias, kvn, wukv)


def _gelu_tanh(x):
    return 0.5 * x * (1.0 + jnp.tanh(np.sqrt(2.0 / np.pi).astype(np.float32) * (x + 0.044715 * (x * x * x))))


def _cmp_kernel(ck_ref, cv_ref, pek_ref, w1k_ref, w2k_ref, pev_ref, w1v_ref, w2v_ref, kc_ref, vc_ref):
    half = NSA_CMP_STRIDE * HEAD_DIM
    for c_ref, pe_ref, w1_ref, w2_ref, o_ref in ((ck_ref, pek_ref, w1k_ref, w2k_ref, kc_ref),
                                                 (cv_ref, pev_ref, w1v_ref, w2v_ref, vc_ref)):
        c = c_ref[...]
        a = _dot(c, w1_ref[0:half, :])
        b = _dot(c, w1_ref[half:2 * half, :])
        n = b.shape[0]
        pe_h = _dot(pe_ref[...], w1_ref[...])
        hid = a + pltpu.roll(b, n - 1, 0) + pe_h[0:1, :]
        o_ref[...] = _dot(_gelu_tanh(hid).astype(BF16), w2_ref[...]).astype(BF16)


def _cmp_call(ck, cv, pek, w1k, w2k, pev, w1v, w2v, B, S):
    nch = S // NSA_CMP_STRIDE
    blk = lambda b: (b, 0)
    const = lambda b: (0, 0)
    cw = NSA_CMP_STRIDE * HEAD_DIM
    in_specs = [pl.BlockSpec((nch, cw), blk), pl.BlockSpec((nch, cw), blk)]
    for _ in range(2):
        in_specs += [pl.BlockSpec((8, 2 * cw), const), pl.BlockSpec((2 * cw, NSA_CMP_HIDDEN), const),
                     pl.BlockSpec((NSA_CMP_HIDDEN, HEAD_DIM), const)]
    out = jax.ShapeDtypeStruct((B * nch, HEAD_DIM), BF16)
    return pl.pallas_call(
        _cmp_kernel, grid=(B,), in_specs=in_specs,
        out_specs=[pl.BlockSpec((nch, HEAD_DIM), blk)] * 2, out_shape=[out, out],
        compiler_params=_cparams(("arbitrary",)),
    )(ck, cv, pek, w1k, w2k, pev, w1v, w2v)


def _attend(q, segments, extra_logit=None):
    scores = []
    for k, _, bias, allowed in segments:
        s = _dot_t(q, k)
        if bias is not None:
            s = s + bias
        if allowed is not None:
            s = jnp.where(allowed, s, NEG_INF)
        scores.append(s)
    m = functools.reduce(jnp.maximum, [jnp.max(s, axis=-1, keepdims=True) for s in scores])
    if extra_logit is not None:
        m = jnp.maximum(m, extra_logit)
    den = jnp.exp2(extra_logit - m) if extra_logit is not None else 0.0
    o = 0.0
    for s, (_, v, _, _) in zip(scores, segments):
        e = jnp.exp2(s - m)
        den = den + jnp.sum(e, axis=-1, keepdims=True)
        o = o + _dot(e.astype(BF16), v)
    return o / den


def _causal_split(i, tq, nk, S):
    n_full = nk - S // N_CAUSAL_CLASSES
    t_col = i * tq + _iota((tq, 1), 0)
    tail_causal = (n_full + _iota((1, nk - n_full), 1)) <= t_col
    return n_full, tail_causal


N_CAUSAL_CLASSES = 8


def _for_causal_class(i, tq, S, body):
    step = S // N_CAUSAL_CLASSES
    cls = (i * tq + tq - 1) // step
    for c in range(N_CAUSAL_CLASSES):
        pl.when(cls == c)(functools.partial(body, (c + 1) * step))


def _nsa_kernel(q_ref, sm_ref, kc_ref, vc_ref, ov_ref, ex_ref, ksel_ref, vsel_ref, kwin_ref, vwin_ref, o_ref):
    i = pl.program_id(1)
    tq = q_ref.shape[0]
    S = ksel_ref.shape[0]
    ncmp = kc_ref.shape[0]
    q = q_ref[...]
    sm = sm_ref[...]
    t_col = i * tq + _iota((tq, 1), 0)

    c_end = _iota((1, ncmp), 1) * NSA_CMP_STRIDE + (NSA_CMP_LEN - 1)
    allowed_c = c_end <= t_col
    kc = kc_ref[...]
    vc = vc_ref[...]
    o_cmp = []
    p_cmp = []
    for hh in range(N_HEADS):
        qh = q[:, hh * 64:(hh + 1) * 64]
        s = jnp.where(allowed_c, _dot_t(qh, kc), NEG_INF)
        m = jnp.max(s, axis=-1, keepdims=True)
        e = jnp.where(allowed_c, jnp.exp2(s - m), 0.0)
        den = jnp.sum(e, axis=-1, keepdims=True)
        p = e * jnp.where(den > 0.0, 1.0 / den, 0.0)
        p_cmp.append(p.astype(BF16))
        o_cmp.append(_dot(p_cmp[-1], vc))

    n_blk = S // NSA_SEL_BLOCK
    n_sel = min(NSA_N_SEL, n_blk)

    def unselected_block_bias():
        imp = sum(_dot_t(ov_ref[...], p) for p in p_cmp)
        jb = _iota((n_blk, tq), 0)
        cur = (i * tq + _iota((1, tq), 1)) // NSA_SEL_BLOCK
        forced = (jb == 0) | (jb == cur) | (jb == cur - 1)
        imp = jnp.where(forced, NSA_FORCE_SCORE, imp)
        imp = jnp.where(jb > cur, -NSA_FORCE_SCORE, imp)
        rank = jnp.zeros((n_blk, tq), F32)
        for jp in range(n_blk):
            row = imp[jp:jp + 1, :]
            ahead = (row > imp) | ((row == imp) & (jp < jb))
            rank = rank + jnp.where(ahead, 1.0, 0.0)
        unsel = jnp.transpose(jnp.where(rank < float(n_sel), 0.0, NEG_INF))
        return jnp.concatenate([unsel, jnp.zeros((tq, LANES - n_blk), F32)], axis=1).astype(BF16)

    nband = NSA_WINDOW + tq
    start = pl.multiple_of(jnp.maximum(i * tq - NSA_WINDOW, 0), math.gcd(tq, NSA_WINDOW))
    kwin = kwin_ref[pl.ds(start, nband), :]
    vwin = vwin_ref[pl.ds(start, nband), :]
    s_band = start + _iota((1, nband), 1)
    allowed_win = (s_band <= t_col) & (s_band > t_col - NSA_WINDOW)
    gate = lambda hh, k: sm[:, SM_GATE0 + 3 * hh + k:SM_GATE0 + 3 * hh + k + 1]
    partial_out = []
    for hh in range(N_HEADS):
        o_win = _attend(q[:, hh * 64:(hh + 1) * 64], [(kwin, vwin, None, allowed_win)])
        partial_out.append(gate(hh, 0) * o_cmp[hh] + gate(hh, 2) * o_win)

    def selected(nk):
        n_full, tail_causal = _causal_split(i, tq, nk, S)
        if nk <= n_sel * NSA_SEL_BLOCK:
            part_bias = lambda lo, hi: None
        else:
            bias = _dot(unselected_block_bias(), ex_ref[:, 0:nk])
            part_bias = lambda lo, hi: bias[:, lo:hi]
        segments = [(ksel_ref[n_full:nk, :], vsel_ref[n_full:nk, :], part_bias(n_full, nk), tail_causal)]
        if n_full:
            segments.append((ksel_ref[0:n_full, :], vsel_ref[0:n_full, :], part_bias(0, n_full), None))
        outs = []
        for hh in range(N_HEADS):
            o_sel = _attend(q[:, hh * 64:(hh + 1) * 64], segments)
            outs.append(partial_out[hh] + gate(hh, 1) * o_sel)
        o_ref[...] = jnp.concatenate(outs, axis=1).astype(BF16)

    _for_causal_class(i, tq, S, selected)


def _nsa_call(qa, sm, kcc, vcc, overlap, expand, ksel, vsel, kwin, vwin, B, S):
    T = B * S
    nq = S // TQ
    ncmp = S // NSA_CMP_STRIDE
    tile = lambda b, i: (b * nq + i, 0)
    seq = lambda b, i: (b, 0)
    const = lambda b, i: (0, 0)
    in_specs = [pl.BlockSpec((TQ, 256), tile), pl.BlockSpec((TQ, LANES), tile),
                pl.BlockSpec((ncmp, 64), seq), pl.BlockSpec((ncmp, 64), seq),
                pl.BlockSpec(overlap.shape, const), pl.BlockSpec(expand.shape, const)] \
        + [pl.BlockSpec((S, 64), seq)] * 4
    return pl.pallas_call(
        _nsa_kernel, grid=(B, nq), in_specs=in_specs, out_specs=pl.BlockSpec((TQ, 256), tile),
        out_shape=jax.ShapeDtypeStruct((T, 256), BF16),
        compiler_params=_cparams(("arbitrary", "arbitrary")),
    )(qa, sm, kcc, vcc, overlap, expand, ksel, vsel, kwin, vwin)


def _fox_swa_kernel(q_ref, nbias_ref, k_ref, v_ref, qd_ref, sink_ref, kd_ref, vd_ref, o_ref, od_ref):
    i = pl.program_id(1)
    tq = q_ref.shape[0]
    S = k_ref.shape[1]
    q = q_ref[...]

    def attend(nk):
        n_full, tail_causal = _causal_split(i, tq, nk, S)
        outs = []
        for hh in range(N_HEADS):
            segments = [(k_ref[hh, n_full:nk, :], v_ref[hh, n_full:nk, :], nbias_ref[0, hh:hh + 1, n_full:nk],
                         tail_causal)]
            if n_full:
                segments.append((k_ref[hh, 0:n_full, :], v_ref[hh, 0:n_full, :], nbias_ref[0, hh:hh + 1, 0:n_full],
                                 None))
            outs.append(_attend(q[:, hh * 64:(hh + 1) * 64], segments))
        o_ref[...] = jnp.concatenate(outs, axis=1).astype(BF16)
        _swa_tile(i, qd_ref, sink_ref, kd_ref, vd_ref, od_ref)

    _for_causal_class(i, tq, S, attend)


def _fox_swa_call(qb, nbias, kb, vb, qd, sinks, kd, vd, B, S):
    T = B * S
    nq = S // TQ
    tile = lambda b, i: (b * nq + i, 0)
    heads = lambda n: pl.BlockSpec((n, S, 64), lambda b, i: (0, b, 0))
    in_specs = [pl.BlockSpec((TQ, 256), tile), pl.BlockSpec((1, N_HEADS, S), lambda b, i: (b, 0, 0)),
                heads(N_HEADS), heads(N_HEADS),
                pl.BlockSpec((TQ, 256), tile), pl.BlockSpec((1, LANES), lambda b, i: (0, 0)),
                heads(SWA_KV_HEADS), heads(SWA_KV_HEADS)]
    out = jax.ShapeDtypeStruct((T, 256), BF16)
    return pl.pallas_call(
        _fox_swa_kernel, grid=(B, nq), in_specs=in_specs, out_specs=[pl.BlockSpec((TQ, 256), tile)] * 2,
        out_shape=[out, out], compiler_params=_cparams(("arbitrary", "arbitrary")),
    )(qb, nbias, kb, vb, qd, sinks, kd, vd)


def _dsa_kernel(q_ref, qi_ref, wrow_ref, ki_ref, cnt_ref, k_ref, v_ref, o_ref, score_ref):
    i = pl.program_id(1)
    tq = q_ref.shape[0]
    S = ki_ref.shape[0]
    qi = qi_ref[...]
    t_col = i * tq + _iota((tq, 1), 0)
    top_k = min(DSA_TOPK_MAX, S // 4)
    q = q_ref[...]

    def attend(nk, bias, allowed):
        outs = [_attend(q[:, hh * 64:(hh + 1) * 64], [(k_ref[hh, 0:nk, :], v_ref[hh, 0:nk, :], bias, allowed)])
                for hh in range(N_HEADS)]
        o_ref[...] = jnp.concatenate(outs, axis=1).astype(BF16)

    def float_of_ordered_bits(u):
        k = u ^ INT_MIN
        return lax.bitcast_convert_type(jnp.where(k >= 0, k, k ^ 0x7FFFFFFF), F32)

    def select_and_attend(nk):
        t_row = i * tq + _iota((1, tq), 1)
        k_eff = jnp.minimum(t_row + 1, top_k).astype(F32)
        w_rows = wrow_ref[0]
        score = jnp.zeros((nk, tq), F32)
        for hh in range(N_HEADS):
            lg = _dot_t(ki_ref[0:nk, :], qi[:, hh * 64:(hh + 1) * 64])
            score = score + w_rows[hh:hh + 1, :] * jnp.maximum(lg, 0.0)
        score_ref[0:nk, :] = jnp.where(_iota((nk, 1), 0) <= t_row, score, -jnp.inf)

        def search(it, u):
            cand = u | lax.shift_left(jnp.int32(1), 31 - it)
            thr = float_of_ordered_bits(cand)
            part = nk // DSA_COUNT_CHAINS
            cnt = sum(jnp.sum(jnp.where(score_ref[g * part:(g + 1) * part, :] >= thr, 1.0, 0.0), axis=0, keepdims=True)
                      for g in range(DSA_COUNT_CHAINS))
            return jnp.where(cnt >= k_eff, cand, u)

        u = lax.fori_loop(0, 32, search, jnp.zeros((1, tq), I32))
        thr = float_of_ordered_bits(u)
        score = score_ref[0:nk, :]
        gt = score > thr
        eq = score == thr
        need = k_eff - jnp.sum(jnp.where(gt, 1.0, 0.0), axis=0, keepdims=True)
        eqb = jnp.where(eq, 1.0, 0.0).astype(BF16)
        run = jnp.zeros((1, tq), F32)
        bias_chunks = []
        for c in range(nk // LANES):
            sl = slice(c * LANES, (c + 1) * LANES)
            r = _dot(cnt_ref[...], eqb[sl, :])
            prefix = r[0:LANES, :] + run
            run = run + r[LANES:LANES + 1, :]
            keep = gt[sl, :] | (eq[sl, :] & (prefix < need))
            bias_chunks.append(jnp.transpose(jnp.where(keep, 0.0, NEG_INF)))
        attend(nk, jnp.concatenate(bias_chunks, axis=1), None)

    def body(nk):
        if nk * N_CAUSAL_CLASSES == S:
            all_kept = (i + 1) * tq <= top_k
            pl.when(all_kept)(lambda: attend(nk, None, _iota((1, nk), 1) <= t_col))
            pl.when(jnp.logical_not(all_kept))(lambda: select_and_attend(nk))
        else:
            select_and_attend(nk)

    _for_causal_class(i, tq, S, body)


def _dsa_call(qc, qi, w_rows, ki, cntmat, kc, vc, B, S):
    T = B * S
    nq = S // TQ
    tile = lambda b, i: (b * nq + i, 0)
    in_specs = [pl.BlockSpec((TQ, 256), tile), pl.BlockSpec((TQ, 256), tile),
                pl.BlockSpec((1, N_HEADS, TQ), lambda b, i: (b, 0, i)),
                pl.BlockSpec((S, 64), lambda b, i: (b, 0)),
                pl.BlockSpec(cntmat.shape, lambda b, i: (0, 0)),
                pl.BlockSpec((N_HEADS, S, 64), lambda b, i: (0, b, 0)),
                pl.BlockSpec((N_HEADS, S, 64), lambda b, i: (0, b, 0))]
    return pl.pallas_call(
        _dsa_kernel, grid=(B, nq), in_specs=in_specs, out_specs=pl.BlockSpec((TQ, 256), tile),
        out_shape=jax.ShapeDtypeStruct((T, 256), BF16),
        scratch_shapes=[pltpu.VMEM((S, TQ), F32)],
        compiler_params=_cparams(("arbitrary", "arbitrary")),
    )(qc, qi, w_rows, ki, cntmat, kc, vc)


def _swa_tile(i, q_ref, sink_ref, k_ref, v_ref, o_ref):
    tq = q_ref.shape[0]
    q = q_ref[...]
    t_col = i * tq + _iota((tq, 1), 0)
    nband = SWA_WINDOW + tq
    start = pl.multiple_of(jnp.maximum(i * tq - SWA_WINDOW, 0), math.gcd(tq, SWA_WINDOW))
    s_band = start + _iota((1, nband), 1)
    allowed = (s_band <= t_col) & (s_band > t_col - SWA_WINDOW)
    group = N_HEADS // SWA_KV_HEADS
    outs = []
    for hh in range(N_HEADS):
        k = k_ref[hh // group, pl.ds(start, nband), :]
        v = v_ref[hh // group, pl.ds(start, nband), :]
        sink = jnp.zeros((tq, 1), F32) + LOG2E * sink_ref[0:1, hh:hh + 1]
        outs.append(_attend(q[:, hh * 64:(hh + 1) * 64], [(k, v, None, allowed)], extra_logit=sink))
    o_ref[...] = jnp.concatenate(outs, axis=1).astype(BF16)


RT_E1, RT_E2, RT_W1, RT_W2, RT_R1, RT_R2 = 0, 1, 2, 3, 4, 5
ROUTER_E0 = MOE_GROUPS


def _merge_kernel(h_ref, oa_ref, ob_ref, oc_ref, od_ref, gain_ref, wg_ref, wb_ref, wo_ref,
                  fgain_ref, wr_ref, br_ref, h2_ref, xn_ref, rt_ref, cnt_ref, base_ref):
    step = pl.program_id(0)
    tm = h_ref.shape[0]
    logit_chunks = []
    for rows in (slice(c * MERGE_CHUNK, (c + 1) * MERGE_CHUNK) for c in range(tm // MERGE_CHUNK)):
        h = h_ref[rows, :]
        hn = h * lax.rsqrt(jnp.mean(h * h, axis=-1, keepdims=True) + NORM_EPS) * gain_ref[...]
        hb = hn.astype(BF16)
        merged = jnp.zeros_like(h)
        for i, o_ref in enumerate((oa_ref, ob_ref, oc_ref, od_ref)):
            gate = 1.0 / (1.0 + jnp.exp(-_dot(hb, wg_ref[i])))
            merged = merged + gate * _dot(o_ref[rows, :], wb_ref[i])
        h2 = h + _dot(merged.astype(BF16), wo_ref[...])
        h2_ref[rows, :] = h2
        xn = h2 * lax.rsqrt(jnp.mean(h2 * h2, axis=-1, keepdims=True) + NORM_EPS) * fgain_ref[...]
        xn_ref[rows, :] = xn
        logit_chunks.append(jnp.dot(xn, wr_ref[...], preferred_element_type=F32, precision=lax.Precision.HIGHEST))
    logits = jnp.concatenate(logit_chunks, axis=0) + br_ref[...]

    lane = _iota((tm, LANES), 1)
    lanef = lane.astype(F32)
    big = float(LANES)
    is_g = lane < MOE_GROUPS
    gl = jnp.where(is_g, logits, NEG_INF)
    gmax = jnp.max(gl, axis=-1, keepdims=True)
    grp = jnp.min(jnp.where(is_g & (gl == gmax), lanef, big), axis=-1, keepdims=True)
    p_grp = 1.0 / jnp.sum(jnp.where(is_g, jnp.exp(gl - gmax), 0.0), axis=-1, keepdims=True)
    lo = ROUTER_E0 + grp * MOE_EXPERTS_PER_GROUP
    in_grp = (lanef >= lo) & (lanef < lo + MOE_EXPERTS_PER_GROUP)
    el = jnp.where(in_grp, logits, NEG_INF)
    v1 = jnp.max(el, axis=-1, keepdims=True)
    l1 = jnp.min(jnp.where(in_grp & (el == v1), lanef, big), axis=-1, keepdims=True)
    el2 = jnp.where(lanef == l1, NEG_INF, el)
    v2 = jnp.max(el2, axis=-1, keepdims=True)
    l2 = jnp.min(jnp.where(in_grp & (lanef != l1) & (el2 == v2), lanef, big), axis=-1, keepdims=True)
    e21 = jnp.exp(v2 - v1)
    w1 = p_grp / (1.0 + e21)
    w2 = p_grp * e21 / (1.0 + e21)

    oh1 = jnp.where(lanef == l1, 1.0, 0.0)
    oh2 = jnp.where(lanef == l2, 1.0, 0.0)

    @pl.when(step == 0)
    def _():
        base_ref[...] = jnp.zeros_like(base_ref)

    both = oh1 + oh2
    strict = (_iota((tm, tm), 0) > _iota((tm, tm), 1)).astype(BF16)
    before = _dot(strict, both.astype(BF16)) + base_ref[0:1, :]
    r1 = jnp.sum(oh1 * before, axis=-1, keepdims=True)
    r2 = jnp.sum(oh2 * before, axis=-1, keepdims=True)
    total = base_ref[0:1, :] + jnp.sum(both, axis=0, keepdims=True)
    base_ref[0:1, :] = total
    cnt_ref[...] = jnp.broadcast_to(total, cnt_ref.shape)

    rt = jnp.zeros((tm, LANES), F32)
    for ln, val in ((RT_E1, l1 - ROUTER_E0), (RT_E2, l2 - ROUTER_E0), (RT_W1, w1), (RT_W2, w2), (RT_R1, r1), (RT_R2, r2)):
        rt = jnp.where(lane == ln, val, rt)
    rt_ref[...] = rt


def _merge_call(h, oa, ob, oc, od, gain, wg, wb, wo, fgain, wr, br):
    T, D = h.shape
    tm = TM_MERGE
    tile = lambda i: (i, 0)
    c2 = lambda i: (0, 0)
    c3 = lambda i: (0, 0, 0)
    once = pl.Buffered(1)
    in_specs = [pl.BlockSpec((tm, D), tile)] + [pl.BlockSpec((tm, 256), tile)] * 4 + [
        pl.BlockSpec((1, D), c2),
        pl.BlockSpec(wg.shape, c3, pipeline_mode=once),
        pl.BlockSpec(wb.shape, c3, pipeline_mode=once),
        pl.BlockSpec(wo.shape, c2, pipeline_mode=once),
        pl.BlockSpec((1, D), c2),
        pl.BlockSpec(wr.shape, c2),
        pl.BlockSpec((1, LANES), c2),
    ]
    out_shape = [jax.ShapeDtypeStruct((T, D), F32), jax.ShapeDtypeStruct((T, D), F32),
                 jax.ShapeDtypeStruct((T, LANES), F32), jax.ShapeDtypeStruct((8, LANES), F32)]
    out_specs = [pl.BlockSpec((tm, D), tile), pl.BlockSpec((tm, D), tile), pl.BlockSpec((tm, LANES), tile),
                 pl.BlockSpec((8, LANES), c2)]
    return pl.pallas_call(
        _merge_kernel, grid=(T // tm,), in_specs=in_specs, out_specs=out_specs, out_shape=out_shape,
        scratch_shapes=[pltpu.VMEM((8, LANES), F32)],
        compiler_params=_cparams(("arbitrary",)),
    )(h, oa, ob, oc, od, gain, wg, wb, wo, fgain, wr, br)


def _row_copy(src, src_row, dst, dst_row, sem):
    return pltpu.make_async_copy(src.at[pl.ds(src_row, 1)], dst.at[pl.ds(dst_row, 1)], sem)


def _dest_kernel(rt_ref, ps_ref, d_ref):
    rt = rt_ref[...]
    lane = _iota(rt.shape, 1)
    lanef = lane.astype(F32)
    ps = ps_ref[...]

    def dest(e_lane, r_lane):
        start = jnp.sum(jnp.where(lanef == rt[:, e_lane:e_lane + 1], ps, 0.0), axis=-1, keepdims=True)
        return start + rt[:, r_lane:r_lane + 1]

    d = jnp.where(lane == 0, dest(RT_E1, RT_R1), jnp.where(lane == 1, dest(RT_E2, RT_R2), 0.0))
    d_ref[...] = d.astype(I32)


def _dest_call(rt, pstart_row):
    T = rt.shape[0]
    tm = TM_MERGE
    return pl.pallas_call(
        _dest_kernel, grid=(T // tm,),
        in_specs=[pl.BlockSpec((tm, LANES), lambda i: (i, 0)), pl.BlockSpec((1, LANES), lambda i: (0, 0))],
        out_specs=pl.BlockSpec((tm, LANES), lambda i: (i, 0)), out_shape=jax.ShapeDtypeStruct((T, LANES), I32),
        compiler_params=_cparams(("arbitrary",)),
    )(rt, pstart_row)


DMA_ISSUE_UNROLL = 8


def _dispatch_kernel(d1_ref, d2_ref, zero_blk_ref, xn_ref, xs_ref, zero_ref, sem, zero_sem):
    base = pl.program_id(0) * TD_DISPATCH

    @pl.when(pl.program_id(0) == 0)
    def _():
        zero_ref[...] = jnp.zeros_like(zero_ref)

        def zero_copy(j):
            row0 = pl.multiple_of(j * MOE_ROW_BLOCK, MOE_ROW_BLOCK)
            return pltpu.make_async_copy(zero_ref, xs_ref.at[pl.ds(row0, MOE_ROW_BLOCK)], zero_sem)

        def start_zero(j, c):
            pl.when(zero_blk_ref[j] != 0)(lambda: zero_copy(j).start())
            return c

        def wait_zero(j, c):
            pl.when(zero_blk_ref[j] != 0)(lambda: zero_copy(j).wait())
            return c

        n_blocks = xs_ref.shape[0] // MOE_ROW_BLOCK
        lax.fori_loop(0, n_blocks, start_zero, 0)
        lax.fori_loop(0, n_blocks, wait_zero, 0)

    def copies(r):
        t = base + r
        return (_row_copy(xn_ref, r, xs_ref, d1_ref[t], sem), _row_copy(xn_ref, r, xs_ref, d2_ref[t], sem))

    def start(r, c):
        for priority, cp in enumerate(copies(r)):
            cp.start(priority=priority)
        return c

    def wait(r, c):
        for cp in copies(r):
            cp.wait()
        return c

    lax.fori_loop(0, TD_DISPATCH, start, 0, unroll=DMA_ISSUE_UNROLL)
    lax.fori_loop(0, TD_DISPATCH, wait, 0, unroll=DMA_ISSUE_UNROLL)


def _dispatch_call(d1, d2, zero_blk, xn, n_rows):
    T, D = xn.shape
    grid_spec = pltpu.PrefetchScalarGridSpec(
        num_scalar_prefetch=3, grid=(T // TD_DISPATCH,),
        in_specs=[pl.BlockSpec((TD_DISPATCH, D), lambda i, *_: (i, 0))],
        out_specs=pl.BlockSpec(memory_space=pl.ANY),
        scratch_shapes=[pltpu.VMEM((MOE_ROW_BLOCK, D), xn.dtype), pltpu.SemaphoreType.DMA(()),
                        pltpu.SemaphoreType.DMA(())])
    return pl.pallas_call(
        _dispatch_kernel, grid_spec=grid_spec, out_shape=jax.ShapeDtypeStruct((n_rows, D), xn.dtype),
        compiler_params=pltpu.CompilerParams(dimension_semantics=("arbitrary",), has_side_effects=True,
                                             vmem_limit_bytes=VMEM_LIMIT),
    )(d1, d2, zero_blk, xn)


def _expert_kernel(be_ref, used_ref, x_ref, wg_ref, wu_ref, wd_ref, y_ref):
    i = pl.program_id(0)

    @pl.when(i < used_ref[0])
    def _():
        x = x_ref[...].astype(BF16)
        g = _dot(x, wg_ref[0, 0].astype(BF16))
        u = _dot(x, wu_ref[0, 0].astype(BF16))
        act = g / (1.0 + jnp.exp(-g)) * u
        y_ref[...] = _dot(act.astype(BF16), wd_ref[0, 0].astype(BF16))

    @pl.when(i >= used_ref[0])
    def _():
        y_ref[...] = jnp.zeros_like(y_ref)


def _expert_call(blk_expert, used, xs, wg, wu, wd, layer):
    rows, D = xs.shape
    rb = MOE_ROW_BLOCK
    DE = wg.shape[3]
    xmap = lambda i, be, used: (jnp.maximum(jnp.minimum(i, used[0] - 1), 0), 0)
    wmap = lambda i, be, used: (layer, be[i], 0, 0)
    grid_spec = pltpu.PrefetchScalarGridSpec(
        num_scalar_prefetch=2, grid=(rows // rb,),
        in_specs=[pl.BlockSpec((rb, D), xmap), pl.BlockSpec((1, 1, D, DE), wmap), pl.BlockSpec((1, 1, D, DE), wmap),
                  pl.BlockSpec((1, 1, DE, D), wmap)],
        out_specs=pl.BlockSpec((rb, D), lambda i, be, used: (i, 0)))
    return pl.pallas_call(
        _expert_kernel, grid_spec=grid_spec, out_shape=jax.ShapeDtypeStruct((rows, D), F32),
        compiler_params=_cparams(("arbitrary",)),
    )(blk_expert, used, xs, wg, wu, wd)


def _combine_kernel(final_norm, d1_ref, d2_ref, h_ref, rt_ref, gain_ref, ys_ref, o_ref, buf_ref, sem):
    base = pl.program_id(0) * TC_COMBINE

    def copies(r):
        t = base + r
        return (_row_copy(ys_ref, d1_ref[t], buf_ref.at[0], r, sem),
                _row_copy(ys_ref, d2_ref[t], buf_ref.at[1], r, sem))

    def start(r, c):
        for priority, cp in enumerate(copies(r)):
            cp.start(priority=priority)
        return c

    def wait(r, c):
        for cp in copies(r):
            cp.wait()
        return c

    lax.fori_loop(0, TC_COMBINE, start, 0, unroll=DMA_ISSUE_UNROLL)
    lax.fori_loop(0, TC_COMBINE, wait, 0, unroll=DMA_ISSUE_UNROLL)
    rt = rt_ref[...]
    out = h_ref[...] + rt[:, RT_W1:RT_W1 + 1] * buf_ref[0] + rt[:, RT_W2:RT_W2 + 1] * buf_ref[1]
    if final_norm:
        out = out * lax.rsqrt(jnp.mean(out * out, axis=-1, keepdims=True) + NORM_EPS) * gain_ref[...]
    o_ref[...] = out


def _combine_call(d1, d2, h, rt, gain, ys, final_norm):
    T, D = h.shape
    tc = TC_COMBINE
    tile = lambda i, *_: (i, 0)
    grid_spec = pltpu.PrefetchScalarGridSpec(
        num_scalar_prefetch=2, grid=(T // tc,),
        in_specs=[pl.BlockSpec((tc, D), tile), pl.BlockSpec((tc, LANES), tile),
                  pl.BlockSpec((1, D), lambda i, *_: (0, 0)), pl.BlockSpec(memory_space=pl.ANY)],
        out_specs=pl.BlockSpec((tc, D), tile),
        scratch_shapes=[pltpu.VMEM((2, tc, D), F32), pltpu.SemaphoreType.DMA(())])
    return pl.pallas_call(
        functools.partial(_combine_kernel, final_norm), grid_spec=grid_spec,
        out_shape=jax.ShapeDtypeStruct((T, D), F32),
        compiler_params=_cparams(("arbitrary",)),
    )(d1, d2, h, rt, gain, ys)


def _in_proj_columns():
    widths = (('q_a', 256), ('k_cmp', 64), ('v_cmp', 64), ('k_sel', 64), ('v_sel', 64), ('k_win', 64), ('v_win', 64),
              ('g_a', 12), ('q_b', 256), ('k_b', 256), ('v_b', 256), ('f_b', 4), ('q_c', 256), ('ckv_c', 128),
              ('qi_c', 256), ('ki_c', 64), ('wi_c', 4), ('q_d', 256), ('k_d', 128), ('v_d', 128))
    off, cols = 0, {}
    for name, w in widths:
        cols[name] = np.arange(off, off + w)
        off += w
    cat = lambda names: np.concatenate([cols[n] for n in names])
    rope = cat(('q_a', 'q_c', 'qi_c', 'q_d', 'k_d', 'k_cmp', 'k_sel', 'k_win', 'ki_c'))
    plain = cat(('q_b', 'k_b', 'v_b', 'v_d', 'ckv_c', 'v_cmp', 'v_sel', 'v_win'))
    small = cat(('g_a', 'f_b', 'wi_c'))
    assert rope.size == N_ROPE_COLS and plain.size == N_PLAIN_COLS
    return np.concatenate([rope, plain, small]), N_SMALL_COLS - small.size


def _static_tables(S):
    n_cmp_pad = S // NSA_CMP_STRIDE
    n_blk = S // NSA_SEL_BLOCK
    c0 = np.arange(n_cmp_pad) * NSA_CMP_STRIDE
    b0 = np.arange(n_blk) * NSA_SEL_BLOCK
    ov = ((c0[:, None] < b0[None, :] + NSA_SEL_BLOCK) & (c0[:, None] + NSA_CMP_LEN > b0[None, :])).astype(np.float32)
    overlap = np.ascontiguousarray(ov.T)
    expand = np.zeros((LANES, S), np.float32)
    expand[np.arange(S) // NSA_SEL_BLOCK, np.arange(S)] = 1.0
    strict_lower = (np.arange(LANES)[:, None] > np.arange(LANES)[None, :]).astype(np.float32)
    cntmat = np.concatenate([strict_lower, np.ones((16, LANES), np.float32)], axis=0)
    return jnp.asarray(overlap, BF16), jnp.asarray(expand, BF16), jnp.asarray(cntmat, BF16)


def kernel(x, positions, attn_norm, w_in, nsa_pe_k, nsa_w1_k, nsa_w2_k, nsa_pe_v, nsa_w1_v, nsa_w2_v, fox_forget_bias, dsa_kv_norm, dsa_w_ukv, swa_sinks, w_branch, w_gate, w_out, ffn_norm, moe_w_group, moe_b_group, moe_w_expert, moe_b_expert, moe_w_gate, moe_w_up, moe_w_down, final_norm):
    B, S, D = x.shape
    T = B * S
    depth = w_in.shape[0]
    perm, n_pad = _in_proj_columns()
    overlap, expand, cntmat = _static_tables(S)
    half = HEAD_DIM // 2
    inv_freq = ROPE_THETA ** (-jnp.arange(half, dtype=F32) / half)
    invf = jnp.tile(inv_freq, LANES // half).reshape(1, LANES)
    pos = positions.reshape(T, 1).astype(I32)
    small_pad = lambda v, off: jnp.zeros((1, LANES), F32).at[0, off:off + v.shape[0]].set(v.astype(F32))

    n_rows = -(-(T * 2 + MOE_N_EXPERTS * (MOE_ROW_BLOCK - 1)) // MOE_ROW_BLOCK) * MOE_ROW_BLOCK
    h = x.reshape(T, D)
    for l in range(depth):
        w = jnp.pad(w_in[l][:, perm], ((0, 0), (0, n_pad))).astype(BF16)
        (qa, qc, qi, qd, kd, kcmp, ksel, kwin, ki, qb, kb, vb, vd, vcmp, vsel, vwin, kc, vc, sm) = _proj_call(
            h, pos, invf, attn_norm[l].reshape(1, D), w, small_pad(fox_forget_bias[l], SM_F0),
            dsa_kv_norm[l].reshape(1, DSA_KV_RANK), dsa_w_ukv[l].astype(BF16), B, S)

        cw = NSA_CMP_STRIDE * HEAD_DIM
        pe_rows = lambda pe: jnp.broadcast_to(pe.reshape(1, 2 * cw), (8, 2 * cw)).astype(BF16)
        kcc, vcc = _cmp_call(kcmp.reshape(T // NSA_CMP_STRIDE, cw), vcmp.reshape(T // NSA_CMP_STRIDE, cw),
                             pe_rows(nsa_pe_k[l]), nsa_w1_k[l].astype(BF16), nsa_w2_k[l].astype(BF16),
                             pe_rows(nsa_pe_v[l]), nsa_w1_v[l].astype(BF16), nsa_w2_v[l].astype(BF16), B, S)
        o_a = _nsa_call(qa, sm, kcc, vcc, overlap, expand, ksel, vsel, kwin, vwin, B, S)

        key_bias = sm[:, SM_F0:SM_F0 + N_HEADS].reshape(B, S, N_HEADS).transpose(0, 2, 1)
        o_b, o_d = _fox_swa_call(qb, key_bias, kb, vb, qd, small_pad(swa_sinks[l], 0), kd, vd, B, S)
        idx_w = sm[:, SM_W0:SM_W0 + N_HEADS].reshape(B, S, N_HEADS).transpose(0, 2, 1)
        o_c = _dsa_call(qc, qi, idx_w, ki, cntmat, kc, vc, B, S)

        wr = jnp.zeros((D, LANES), F32).at[:, :MOE_GROUPS].set(moe_w_group[l]) \
            .at[:, ROUTER_E0:ROUTER_E0 + MOE_N_EXPERTS].set(moe_w_expert[l])
        br = jnp.zeros((1, LANES), F32).at[0, :MOE_GROUPS].set(moe_b_group[l]) \
            .at[0, ROUTER_E0:ROUTER_E0 + MOE_N_EXPERTS].set(moe_b_expert[l])
        h2, xn, rt, cnt = _merge_call(h, o_a, o_b, o_c, o_d, attn_norm[l].reshape(1, D), w_gate[l].astype(BF16),
                                      w_branch[l].astype(BF16), w_out[l].astype(BF16), ffn_norm[l].reshape(1, D),
                                      wr, br)

        counts = cnt[0, ROUTER_E0:ROUTER_E0 + MOE_N_EXPERTS].astype(I32)
        padded = (counts + MOE_ROW_BLOCK - 1) // MOE_ROW_BLOCK * MOE_ROW_BLOCK
        p_end = jnp.cumsum(padded)
        p_start = p_end - padded
        n_blocks = n_rows // MOE_ROW_BLOCK
        blk_first_row = jnp.arange(n_blocks, dtype=I32) * MOE_ROW_BLOCK
        blk_expert = jnp.minimum(jnp.sum((p_end[None, :] <= blk_first_row[:, None]).astype(I32), axis=1),
                                 MOE_N_EXPERTS - 1).astype(I32)
        used = (p_end[-1:] // MOE_ROW_BLOCK).astype(I32)
        dest = _dest_call(rt, small_pad(p_start, 0))
        d1, d2 = dest[:, 0], dest[:, 1]

        blk = jnp.arange(n_blocks, dtype=I32)
        last_of_segment = jnp.any((blk[:, None] == (p_end // MOE_ROW_BLOCK - 1)[None, :]) & (padded > 0)[None, :], axis=1)
        zero_blk = (last_of_segment | (blk >= used[0])).astype(I32)
        xs = _dispatch_call(d1, d2, zero_blk, xn, n_rows)
        ys = _expert_call(blk_expert, used, xs, moe_w_gate, moe_w_up, moe_w_down, l)
        last = l == depth - 1
        h = _combine_call(d1, d2, h2, rt, final_norm.reshape(1, D), ys, last)
    return h.reshape(B, S, D)
```

```python
import functools
import math

import numpy as np
import jax
import jax.numpy as jnp
from jax import lax
from jax.experimental import pallas as pl
from jax.experimental.pallas import tpu as pltpu

F32 = jnp.float32
BF16 = jnp.bfloat16
I32 = jnp.int32

HEAD_DIM = 64
N_HEADS = 4
MIX_WIDTH = N_HEADS * HEAD_DIM
ROPE_THETA = 10000.0
NORM_EPS = 1e-6
NEG_INF = -1e30
INT_MIN = -2 ** 31
LOG2E = float(np.log2(np.e))

NSA_CMP_LEN = 32
NSA_CMP_STRIDE = 16
NSA_CMP_HIDDEN = 256
NSA_SEL_BLOCK = 64
NSA_N_SEL = 16
NSA_WINDOW = 512
NSA_FORCE_SCORE = 1e4
DSA_KV_RANK = 128
DSA_TOPK_MAX = 256
DSA_COUNT_CHAINS = 4
SWA_WINDOW = 128
SWA_KV_HEADS = 2
MOE_GROUPS = 8
MOE_EXPERTS_PER_GROUP = 8
MOE_N_EXPERTS = 64
MOE_D_EXPERT = 256

LANES = 128
VMEM_LIMIT = 48 * 1024 * 1024

N_ROPE_COLS = 1408
N_PLAIN_COLS = 1216
N_SMALL_COLS = 128
N_PROJ_COLS = N_ROPE_COLS + N_PLAIN_COLS + N_SMALL_COLS
SM_GATE0, SM_F0, SM_W0 = 0, 12, 16

TM_PROJ = 512
PROJ_CHUNK = 128
TQ = 256
TM_MERGE = 512
MERGE_CHUNK = 128
MOE_ROW_BLOCK = 256
TD_DISPATCH = 256
TC_COMBINE = 256


def _cparams(sem):
    return pltpu.CompilerParams(dimension_semantics=sem, vmem_limit_bytes=VMEM_LIMIT)


def _dot(a, b):
    return jnp.dot(a, b, preferred_element_type=F32)


def _dot_t(a, b):
    return lax.dot_general(a, b, (((1,), (1,)), ((), ())), preferred_element_type=F32)


def _iota(shape, dim):
    return lax.broadcasted_iota(I32, shape, dim)


def _rope_slab(x, cos, sin_signed, first_half):
    rot = jnp.where(first_half, pltpu.roll(x, 96, 1), pltpu.roll(x, 32, 1))
    return x * cos + rot * sin_signed


def _proj_kernel(h_ref, pos_ref, invf_ref, gain_ref, w_ref, fbias_ref, kvn_ref, wukv_ref,
                 qa_ref, qc_ref, qi_ref, qd_ref, kd_ref, kcmp_ref, ksel_ref, kwin_ref, ki_ref,
                 qb_ref, kb_ref, vb_ref, vd_ref, vcmp_ref, vsel_ref, vwin_ref, kc_ref, vc_ref,
                 sm_ref, carry_ref):
    j = pl.program_id(1)
    tm = h_ref.shape[0]
    tc = PROJ_CHUNK

    @pl.when(j == 0)
    def _():
        carry_ref[...] = jnp.zeros_like(carry_ref)

    lane = _iota((tc, LANES), 1)
    first_half = (lane % HEAD_DIM) < (HEAD_DIM // 2)
    is_f = (lane >= SM_F0) & (lane < SM_W0)
    tri = (_iota((tc, tc), 0) >= _iota((tc, tc), 1)).astype(F32)
    scale = HEAD_DIM ** -0.5
    softmax_scale = scale * LOG2E

    for rows in (slice(c * tc, (c + 1) * tc) for c in range(tm // tc)):
        h = h_ref[rows, :]
        hn = h * lax.rsqrt(jnp.mean(h * h, axis=-1, keepdims=True) + NORM_EPS) * gain_ref[...]
        hb = hn.astype(BF16)

        ang = pos_ref[rows, :].astype(F32) * invf_ref[...]
        cos = jnp.cos(ang)
        sin = jnp.sin(ang)
        sin_signed = jnp.where(first_half, -sin, sin)
        rope = lambda x: _rope_slab(x, cos, sin_signed, first_half)

        zr = _dot(hb, w_ref[:, 0:N_ROPE_COLS])
        slab = lambda k: rope(zr[:, k * LANES:(k + 1) * LANES])
        for q_ref, k0, sc in ((qa_ref, 0, softmax_scale), (qc_ref, 2, softmax_scale), (qi_ref, 4, scale),
                              (qd_ref, 6, softmax_scale)):
            q_ref[rows, :] = (jnp.concatenate([slab(k0), slab(k0 + 1)], axis=1) * sc).astype(BF16)
        s8 = slab(8).astype(BF16)
        kd_ref[0, rows, :] = s8[:, :HEAD_DIM]
        kd_ref[1, rows, :] = s8[:, HEAD_DIM:]
        s9 = slab(9).astype(BF16)
        kcmp_ref[rows, :] = s9[:, :HEAD_DIM]
        ksel_ref[rows, :] = s9[:, HEAD_DIM:]
        s10 = slab(10).astype(BF16)
        kwin_ref[rows, :] = s10[:, :HEAD_DIM]
        ki_ref[rows, :] = s10[:, HEAD_DIM:]

        zp = _dot(hb, w_ref[:, N_ROPE_COLS:N_ROPE_COLS + N_PLAIN_COLS])
        qb_ref[rows, :] = (zp[:, 0:256] * softmax_scale).astype(BF16)
        for hh in range(N_HEADS):
            kb_ref[hh, rows, :] = zp[:, 256 + hh * 64:256 + (hh + 1) * 64].astype(BF16)
            vb_ref[hh, rows, :] = zp[:, 512 + hh * 64:512 + (hh + 1) * 64].astype(BF16)
        vd_ref[0, rows, :] = zp[:, 768:832].astype(BF16)
        vd_ref[1, rows, :] = zp[:, 832:896].astype(BF16)
        ckv = zp[:, 896:1024]
        vcmp_ref[rows, :] = zp[:, 1024:1088].astype(BF16)
        vsel_ref[rows, :] = zp[:, 1088:1152].astype(BF16)
        vwin_ref[rows, :] = zp[:, 1152:1216].astype(BF16)

        ckvn = ckv * lax.rsqrt(jnp.mean(ckv * ckv, axis=-1, keepdims=True) + NORM_EPS) * kvn_ref[...]
        kvc = _dot(ckvn.astype(BF16), wukv_ref[...])
        for k in range(2):
            kk = rope(kvc[:, k * LANES:(k + 1) * LANES]).astype(BF16)
            kc_ref[2 * k, rows, :] = kk[:, :HEAD_DIM]
            kc_ref[2 * k + 1, rows, :] = kk[:, HEAD_DIM:]
        for hh in range(N_HEADS):
            vc_ref[hh, rows, :] = kvc[:, 256 + hh * 64:256 + (hh + 1) * 64].astype(BF16)

        zs = _dot(hb, w_ref[:, N_ROPE_COLS + N_PLAIN_COLS:N_PROJ_COLS])
        sig = 1.0 / (1.0 + jnp.exp(-zs))
        xf = zs + fbias_ref[...]
        logf = jnp.minimum(xf, 0.0) - jnp.log(1.0 + jnp.exp(-jnp.abs(xf)))
        logf = jnp.where(is_f, logf, 0.0)
        cum = jnp.dot(tri, logf, preferred_element_type=F32, precision=lax.Precision.HIGHEST)
        cum = cum + carry_ref[0:1, :]
        carry_ref[0:1, :] = cum[tc - 1:tc, :]
        sm_ref[rows, :] = jnp.where(lane < SM_F0, sig,
                                    jnp.where(is_f, -LOG2E * cum, jnp.where(lane < SM_W0 + 4, 0.5 * zs, 0.0)))


def _proj_call(h, pos, invf, gain, w, fbias, kvn, wukv, B, S):
    T = B * S
    tm = TM_PROJ
    nj = S // tm
    row = lambda b, j: (b * nj + j, 0)
    hrow = lambda b, j: (0, b * nj + j, 0)
    const2 = lambda b, j: (0, 0)
    tok = lambda w_, dt: jax.ShapeDtypeStruct((T, w_), dt)
    hm = lambda n: jax.ShapeDtypeStruct((n, T, HEAD_DIM), BF16)
    out_shape = [tok(256, BF16)] * 4 + [hm(2)] + [tok(64, BF16)] * 4 + [tok(256, BF16), hm(4), hm(4), hm(2)] \
        + [tok(64, BF16)] * 3 + [hm(4), hm(4), tok(LANES, F32)]
    spec_tok = lambda w_: pl.BlockSpec((tm, w_), row)
    spec_hm = lambda n: pl.BlockSpec((n, tm, HEAD_DIM), hrow)
    out_specs = [spec_tok(256)] * 4 + [spec_hm(2)] + [spec_tok(64)] * 4 + [spec_tok(256), spec_hm(4), spec_hm(4), spec_hm(2)] \
        + [spec_tok(64)] * 3 + [spec_hm(4), spec_hm(4), spec_tok(LANES)]
    in_specs = [
        pl.BlockSpec((tm, h.shape[1]), row),
        pl.BlockSpec((tm, 1), row),
        pl.BlockSpec((1, LANES), const2),
        pl.BlockSpec((1, h.shape[1]), const2),
        pl.BlockSpec(w.shape, const2),
        pl.BlockSpec((1, LANES), const2),
        pl.BlockSpec((1, DSA_KV_RANK), const2),
        pl.BlockSpec(wukv.shape, const2),
    ]
    return pl.pallas_call(
        _proj_kernel, grid=(B, nj), in_specs=in_specs, out_specs=out_specs, out_shape=out_shape,
        scratch_shapes=[pltpu.VMEM((8, LANES), F32)],
        compiler_params=_cparams(("arbitrary", "arbitrary")),
    )(h, pos, invf, gain, w, fbias, kvn, wukv)


def _gelu_tanh(x):
    return 0.5 * x * (1.0 + jnp.tanh(np.sqrt(2.0 / np.pi).astype(np.float32) * (x + 0.044715 * (x * x * x))))


def _cmp_kernel(ck_ref, cv_ref, pek_ref, w1k_ref, w2k_ref, pev_ref, w1v_ref, w2v_ref, kc_ref, vc_ref):
    half = NSA_CMP_STRIDE * HEAD_DIM
    for c_ref, pe_ref, w1_ref, w2_ref, o_ref in ((ck_ref, pek_ref, w1k_ref, w2k_ref, kc_ref),
                                                 (cv_ref, pev_ref, w1v_ref, w2v_ref, vc_ref)):
        c = c_ref[...]
        a = _dot(c, w1_ref[0:half, :])
        b = _dot(c, w1_ref[half:2 * half, :])
        n = b.shape[0]
        pe_h = _dot(pe_ref[...], w1_ref[...])
        hid = a + pltpu.roll(b, n - 1, 0) + pe_h[0:1, :]
        o_ref[...] = _dot(_gelu_tanh(hid).astype(BF16), w2_ref[...]).astype(BF16)


def _cmp_call(ck, cv, pek, w1k, w2k, pev, w1v, w2v, B, S):
    nch = S // NSA_CMP_STRIDE
    blk = lambda b: (b, 0)
    const = lambda b: (0, 0)
    cw = NSA_CMP_STRIDE * HEAD_DIM
    in_specs = [pl.BlockSpec((nch, cw), blk), pl.BlockSpec((nch, cw), blk)]
    for _ in range(2):
        in_specs += [pl.BlockSpec((8, 2 * cw), const), pl.BlockSpec((2 * cw, NSA_CMP_HIDDEN), const),
                     pl.BlockSpec((NSA_CMP_HIDDEN, HEAD_DIM), const)]
    out = jax.ShapeDtypeStruct((B * nch, HEAD_DIM), BF16)
    return pl.pallas_call(
        _cmp_kernel, grid=(B,), in_specs=in_specs,
        out_specs=[pl.BlockSpec((nch, HEAD_DIM), blk)] * 2, out_shape=[out, out],
        compiler_params=_cparams(("arbitrary",)),
    )(ck, cv, pek, w1k, w2k, pev, w1v, w2v)


def _attend(q, segments, extra_logit=None):
    scores = []
    for k, _, bias, allowed in segments:
        s = _dot_t(q, k)
        if bias is not None:
            s = s + bias
        if allowed is not None:
            s = jnp.where(allowed, s, NEG_INF)
        scores.append(s)
    m = functools.reduce(jnp.maximum, [jnp.max(s, axis=-1, keepdims=True) for s in scores])
    if extra_logit is not None:
        m = jnp.maximum(m, extra_logit)
    den = jnp.exp2(extra_logit - m) if extra_logit is not None else 0.0
    o = 0.0
    for s, (_, v, _, _) in zip(scores, segments):
        e = jnp.exp2(s - m)
        den = den + jnp.sum(e, axis=-1, keepdims=True)
        o = o + _dot(e.astype(BF16), v)
    return o / den


def _causal_split(i, tq, nk, S):
    n_full = nk - S // N_CAUSAL_CLASSES
    t_col = i * tq + _iota((tq, 1), 0)
    tail_causal = (n_full + _iota((1, nk - n_full), 1)) <= t_col
    return n_full, tail_causal


N_CAUSAL_CLASSES = 8


def _for_causal_class(i, tq, S, body):
    step = S // N_CAUSAL_CLASSES
    cls = (i * tq + tq - 1) // step
    for c in range(N_CAUSAL_CLASSES):
        pl.when(cls == c)(functools.partial(body, (c + 1) * step))


def _nsa_kernel(q_ref, sm_ref, kc_ref, vc_ref, ov_ref, ex_ref, ksel_ref, vsel_ref, kwin_ref, vwin_ref, o_ref):
    i = pl.program_id(1)
    tq = q_ref.shape[0]
    S = ksel_ref.shape[0]
    ncmp = kc_ref.shape[0]
    q = q_ref[...]
    sm = sm_ref[...]
    t_col = i * tq + _iota((tq, 1), 0)

    c_end = _iota((1, ncmp), 1) * NSA_CMP_STRIDE + (NSA_CMP_LEN - 1)
    allowed_c = c_end <= t_col
    kc = kc_ref[...]
    vc = vc_ref[...]
    o_cmp = []
    p_cmp = []
    for hh in range(N_HEADS):
        qh = q[:, hh * HEAD_DIM:(hh + 1) * HEAD_DIM]
        s = jnp.where(allowed_c, _dot_t(qh, kc), NEG_INF)
        m = jnp.max(s, axis=-1, keepdims=True)
        e = jnp.where(allowed_c, jnp.exp2(s - m), 0.0)
        den = jnp.sum(e, axis=-1, keepdims=True)
        p = e * jnp.where(den > 0.0, 1.0 / den, 0.0)
        p_cmp.append(p.astype(BF16))
        o_cmp.append(_dot(p_cmp[-1], vc))

    n_blk = S // NSA_SEL_BLOCK
    n_sel = min(NSA_N_SEL, n_blk)

    def unselected_block_bias():
        imp = sum(_dot_t(ov_ref[...], p) for p in p_cmp)
        jb = _iota((n_blk, tq), 0)
        cur = (i * tq + _iota((1, tq), 1)) // NSA_SEL_BLOCK
        forced = (jb == 0) | (jb == cur) | (jb == cur - 1)
        imp = jnp.where(forced, NSA_FORCE_SCORE, imp)
        imp = jnp.where(jb > cur, -NSA_FORCE_SCORE, imp)
        rank = jnp.zeros((n_blk, tq), F32)
        for jp in range(n_blk):
            row = imp[jp:jp + 1, :]
            ahead = (row > imp) | ((row == imp) & (jp < jb))
            rank = rank + jnp.where(ahead, 1.0, 0.0)
        unsel = jnp.transpose(jnp.where(rank < float(n_sel), 0.0, NEG_INF))
        return jnp.concatenate([unsel, jnp.zeros((tq, LANES - n_blk), F32)], axis=1).astype(BF16)

    nband = NSA_WINDOW + tq
    start = pl.multiple_of(jnp.maximum(i * tq - NSA_WINDOW, 0), math.gcd(tq, NSA_WINDOW))
    kwin = kwin_ref[pl.ds(start, nband), :]
    vwin = vwin_ref[pl.ds(start, nband), :]
    s_band = start + _iota((1, nband), 1)
    allowed_win = (s_band <= t_col) & (s_band > t_col - NSA_WINDOW)
    gate = lambda hh, k: sm[:, SM_GATE0 + 3 * hh + k:SM_GATE0 + 3 * hh + k + 1]
    partial_out = []
    for hh in range(N_HEADS):
        o_win = _attend(q[:, hh * HEAD_DIM:(hh + 1) * HEAD_DIM], [(kwin, vwin, None, allowed_win)])
        partial_out.append(gate(hh, 0) * o_cmp[hh] + gate(hh, 2) * o_win)

    def selected(nk):
        n_full, tail_causal = _causal_split(i, tq, nk, S)
        if nk <= n_sel * NSA_SEL_BLOCK:
            part_bias = lambda lo, hi: None
        else:
            bias = _dot(unselected_block_bias(), ex_ref[:, 0:nk])
            part_bias = lambda lo, hi: bias[:, lo:hi]
        segments = [(ksel_ref[n_full:nk, :], vsel_ref[n_full:nk, :], part_bias(n_full, nk), tail_causal)]
        if n_full:
            segments.append((ksel_ref[0:n_full, :], vsel_ref[0:n_full, :], part_bias(0, n_full), None))
        outs = []
        for hh in range(N_HEADS):
            o_sel = _attend(q[:, hh * HEAD_DIM:(hh + 1) * HEAD_DIM], segments)
            outs.append(partial_out[hh] + gate(hh, 1) * o_sel)
        o_ref[...] = jnp.concatenate(outs, axis=1).astype(BF16)

    _for_causal_class(i, tq, S, selected)


def _nsa_call(qa, sm, kcc, vcc, overlap, expand, ksel, vsel, kwin, vwin, B, S):
    T = B * S
    nq = S // TQ
    ncmp = S // NSA_CMP_STRIDE
    tile = lambda b, i: (b * nq + i, 0)
    seq = lambda b, i: (b, 0)
    const = lambda b, i: (0, 0)
    in_specs = [pl.BlockSpec((TQ, MIX_WIDTH), tile), pl.BlockSpec((TQ, LANES), tile),
                pl.BlockSpec((ncmp, 64), seq), pl.BlockSpec((ncmp, 64), seq),
                pl.BlockSpec(overlap.shape, const), pl.BlockSpec(expand.shape, const)] \
        + [pl.BlockSpec((S, 64), seq)] * 4
    return pl.pallas_call(
        _nsa_kernel, grid=(B, nq), in_specs=in_specs, out_specs=pl.BlockSpec((TQ, MIX_WIDTH), tile),
        out_shape=jax.ShapeDtypeStruct((T, MIX_WIDTH), BF16),
        compiler_params=_cparams(("arbitrary", "arbitrary")),
    )(qa, sm, kcc, vcc, overlap, expand, ksel, vsel, kwin, vwin)


def _fox_swa_kernel(q_ref, nbias_ref, k_ref, v_ref, qd_ref, sink_ref, kd_ref, vd_ref, o_ref, od_ref):
    i = pl.program_id(1)
    tq = q_ref.shape[0]
    S = k_ref.shape[1]
    q = q_ref[...]

    def attend(nk):
        n_full, tail_causal = _causal_split(i, tq, nk, S)
        outs = []
        for hh in range(N_HEADS):
            segments = [(k_ref[hh, n_full:nk, :], v_ref[hh, n_full:nk, :], nbias_ref[0, hh:hh + 1, n_full:nk],
                         tail_causal)]
            if n_full:
                segments.append((k_ref[hh, 0:n_full, :], v_ref[hh, 0:n_full, :], nbias_ref[0, hh:hh + 1, 0:n_full],
                                 None))
            outs.append(_attend(q[:, hh * HEAD_DIM:(hh + 1) * HEAD_DIM], segments))
        o_ref[...] = jnp.concatenate(outs, axis=1).astype(BF16)
        _swa_tile(i, qd_ref, sink_ref, kd_ref, vd_ref, od_ref)

    _for_causal_class(i, tq, S, attend)


def _fox_swa_call(qb, nbias, kb, vb, qd, sinks, kd, vd, B, S):
    T = B * S
    nq = S // TQ
    tile = lambda b, i: (b * nq + i, 0)
    heads = lambda n: pl.BlockSpec((n, S, 64), lambda b, i: (0, b, 0))
    in_specs = [pl.BlockSpec((TQ, MIX_WIDTH), tile), pl.BlockSpec((1, N_HEADS, S), lambda b, i: (b, 0, 0)),
                heads(N_HEADS), heads(N_HEADS),
                pl.BlockSpec((TQ, MIX_WIDTH), tile), pl.BlockSpec((1, LANES), lambda b, i: (0, 0)),
                heads(SWA_KV_HEADS), heads(SWA_KV_HEADS)]
    out = jax.ShapeDtypeStruct((T, MIX_WIDTH), BF16)
    return pl.pallas_call(
        _fox_swa_kernel, grid=(B, nq), in_specs=in_specs, out_specs=[pl.BlockSpec((TQ, MIX_WIDTH), tile)] * 2,
        out_shape=[out, out], compiler_params=_cparams(("arbitrary", "arbitrary")),
    )(qb, nbias, kb, vb, qd, sinks, kd, vd)


def _dsa_kernel(q_ref, qi_ref, wrow_ref, ki_ref, cnt_ref, k_ref, v_ref, o_ref, score_ref):
    i = pl.program_id(1)
    tq = q_ref.shape[0]
    S = ki_ref.shape[0]
    qi = qi_ref[...]
    t_col = i * tq + _iota((tq, 1), 0)
    top_k = min(DSA_TOPK_MAX, S // 4)
    q = q_ref[...]

    def attend(nk, bias, allowed):
        outs = [_attend(q[:, hh * HEAD_DIM:(hh + 1) * HEAD_DIM], [(k_ref[hh, 0:nk, :], v_ref[hh, 0:nk, :], bias, allowed)])
                for hh in range(N_HEADS)]
        o_ref[...] = jnp.concatenate(outs, axis=1).astype(BF16)

    def float_of_ordered_bits(u):
        k = u ^ INT_MIN
        return lax.bitcast_convert_type(jnp.where(k >= 0, k, k ^ 0x7FFFFFFF), F32)

    def select_and_attend(nk):
        t_row = i * tq + _iota((1, tq), 1)
        k_eff = jnp.minimum(t_row + 1, top_k).astype(F32)
        w_rows = wrow_ref[0]
        score = jnp.zeros((nk, tq), F32)
        for hh in range(N_HEADS):
            lg = _dot_t(ki_ref[0:nk, :], qi[:, hh * HEAD_DIM:(hh + 1) * HEAD_DIM])
            score = score + w_rows[hh:hh + 1, :] * jnp.maximum(lg, 0.0)
        score_ref[0:nk, :] = jnp.where(_iota((nk, 1), 0) <= t_row, score, -jnp.inf)

        def search(it, u):
            cand = u | lax.shift_left(jnp.int32(1), 31 - it)
            thr = float_of_ordered_bits(cand)
            part = nk // DSA_COUNT_CHAINS
            cnt = sum(jnp.sum(jnp.where(score_ref[g * part:(g + 1) * part, :] >= thr, 1.0, 0.0), axis=0, keepdims=True)
                      for g in range(DSA_COUNT_CHAINS))
            return jnp.where(cnt >= k_eff, cand, u)

        u = lax.fori_loop(0, 32, search, jnp.zeros((1, tq), I32))
        thr = float_of_ordered_bits(u)
        score = score_ref[0:nk, :]
        gt = score > thr
        eq = score == thr
        need = k_eff - jnp.sum(jnp.where(gt, 1.0, 0.0), axis=0, keepdims=True)
        eqb = jnp.where(eq, 1.0, 0.0).astype(BF16)
        run = jnp.zeros((1, tq), F32)
        bias_chunks = []
        for c in range(nk // LANES):
            sl = slice(c * LANES, (c + 1) * LANES)
            r = _dot(cnt_ref[...], eqb[sl, :])
            prefix = r[0:LANES, :] + run
            run = run + r[LANES:LANES + 1, :]
            keep = gt[sl, :] | (eq[sl, :] & (prefix < need))
            bias_chunks.append(jnp.transpose(jnp.where(keep, 0.0, NEG_INF)))
        attend(nk, jnp.concatenate(bias_chunks, axis=1), None)

    def body(nk):
        if nk * N_CAUSAL_CLASSES == S:
            all_kept = (i + 1) * tq <= top_k
            pl.when(all_kept)(lambda: attend(nk, None, _iota((1, nk), 1) <= t_col))
            pl.when(jnp.logical_not(all_kept))(lambda: select_and_attend(nk))
        else:
            select_and_attend(nk)

    _for_causal_class(i, tq, S, body)


def _dsa_call(qc, qi, w_rows, ki, cntmat, kc, vc, B, S):
    T = B * S
    nq = S // TQ
    tile = lambda b, i: (b * nq + i, 0)
    in_specs = [pl.BlockSpec((TQ, MIX_WIDTH), tile), pl.BlockSpec((TQ, MIX_WIDTH), tile),
                pl.BlockSpec((1, N_HEADS, TQ), lambda b, i: (b, 0, i)),
                pl.BlockSpec((S, 64), lambda b, i: (b, 0)),
                pl.BlockSpec(cntmat.shape, lambda b, i: (0, 0)),
                pl.BlockSpec((N_HEADS, S, 64), lambda b, i: (0, b, 0)),
                pl.BlockSpec((N_HEADS, S, 64), lambda b, i: (0, b, 0))]
    return pl.pallas_call(
        _dsa_kernel, grid=(B, nq), in_specs=in_specs, out_specs=pl.BlockSpec((TQ, MIX_WIDTH), tile),
        out_shape=jax.ShapeDtypeStruct((T, MIX_WIDTH), BF16),
        scratch_shapes=[pltpu.VMEM((S, TQ), F32)],
        compiler_params=_cparams(("arbitrary", "arbitrary")),
    )(qc, qi, w_rows, ki, cntmat, kc, vc)


def _swa_tile(i, q_ref, sink_ref, k_ref, v_ref, o_ref):
    tq = q_ref.shape[0]
    q = q_ref[...]
    t_col = i * tq + _iota((tq, 1), 0)
    nband = SWA_WINDOW + tq
    start = pl.multiple_of(jnp.maximum(i * tq - SWA_WINDOW, 0), math.gcd(tq, SWA_WINDOW))
    s_band = start + _iota((1, nband), 1)
    allowed = (s_band <= t_col) & (s_band > t_col - SWA_WINDOW)
    group = N_HEADS // SWA_KV_HEADS
    outs = []
    for hh in range(N_HEADS):
        k = k_ref[hh // group, pl.ds(start, nband), :]
        v = v_ref[hh // group, pl.ds(start, nband), :]
        sink = jnp.zeros((tq, 1), F32) + LOG2E * sink_ref[0:1, hh:hh + 1]
        outs.append(_attend(q[:, hh * HEAD_DIM:(hh + 1) * HEAD_DIM], [(k, v, None, allowed)], extra_logit=sink))
    o_ref[...] = jnp.concatenate(outs, axis=1).astype(BF16)


RT_E1, RT_E2, RT_W1, RT_W2, RT_R1, RT_R2 = 0, 1, 2, 3, 4, 5
ROUTER_E0 = MOE_GROUPS


def _merge_kernel(h_ref, oa_ref, ob_ref, oc_ref, od_ref, gain_ref, wg_ref, wb_ref, wo_ref,
                  fgain_ref, wr_ref, br_ref, h2_ref, xn_ref, rt_ref, cnt_ref, base_ref):
    step = pl.program_id(0)
    tm = h_ref.shape[0]
    logit_chunks = []
    for rows in (slice(c * MERGE_CHUNK, (c + 1) * MERGE_CHUNK) for c in range(tm // MERGE_CHUNK)):
        h = h_ref[rows, :]
        hn = h * lax.rsqrt(jnp.mean(h * h, axis=-1, keepdims=True) + NORM_EPS) * gain_ref[...]
        hb = hn.astype(BF16)
        merged = jnp.zeros_like(h)
        for i, o_ref in enumerate((oa_ref, ob_ref, oc_ref, od_ref)):
            gate = 1.0 / (1.0 + jnp.exp(-_dot(hb, wg_ref[i])))
            merged = merged + gate * _dot(o_ref[rows, :], wb_ref[i])
        h2 = h + _dot(merged.astype(BF16), wo_ref[...])
        h2_ref[rows, :] = h2
        xn = h2 * lax.rsqrt(jnp.mean(h2 * h2, axis=-1, keepdims=True) + NORM_EPS) * fgain_ref[...]
        xn_ref[rows, :] = xn
        logit_chunks.append(jnp.dot(xn, wr_ref[...], preferred_element_type=F32, precision=lax.Precision.HIGHEST))
    logits = jnp.concatenate(logit_chunks, axis=0) + br_ref[...]

    lane = _iota((tm, LANES), 1)
    lanef = lane.astype(F32)
    big = float(LANES)
    is_g = lane < MOE_GROUPS
    gl = jnp.where(is_g, logits, NEG_INF)
    gmax = jnp.max(gl, axis=-1, keepdims=True)
    grp = jnp.min(jnp.where(is_g & (gl == gmax), lanef, big), axis=-1, keepdims=True)
    p_grp = 1.0 / jnp.sum(jnp.where(is_g, jnp.exp(gl - gmax), 0.0), axis=-1, keepdims=True)
    lo = ROUTER_E0 + grp * MOE_EXPERTS_PER_GROUP
    in_grp = (lanef >= lo) & (lanef < lo + MOE_EXPERTS_PER_GROUP)
    el = jnp.where(in_grp, logits, NEG_INF)
    v1 = jnp.max(el, axis=-1, keepdims=True)
    l1 = jnp.min(jnp.where(in_grp & (el == v1), lanef, big), axis=-1, keepdims=True)
    el2 = jnp.where(lanef == l1, NEG_INF, el)
    v2 = jnp.max(el2, axis=-1, keepdims=True)
    l2 = jnp.min(jnp.where(in_grp & (lanef != l1) & (el2 == v2), lanef, big), axis=-1, keepdims=True)
    e21 = jnp.exp(v2 - v1)
    w1 = p_grp / (1.0 + e21)
    w2 = p_grp * e21 / (1.0 + e21)

    oh1 = jnp.where(lanef == l1, 1.0, 0.0)
    oh2 = jnp.where(lanef == l2, 1.0, 0.0)

    @pl.when(step == 0)
    def _():
        base_ref[...] = jnp.zeros_like(base_ref)

    both = oh1 + oh2
    strict = (_iota((tm, tm), 0) > _iota((tm, tm), 1)).astype(BF16)
    before = _dot(strict, both.astype(BF16)) + base_ref[0:1, :]
    r1 = jnp.sum(oh1 * before, axis=-1, keepdims=True)
    r2 = jnp.sum(oh2 * before, axis=-1, keepdims=True)
    total = base_ref[0:1, :] + jnp.sum(both, axis=0, keepdims=True)
    base_ref[0:1, :] = total
    cnt_ref[...] = jnp.broadcast_to(total, cnt_ref.shape)

    rt = jnp.zeros((tm, LANES), F32)
    for ln, val in ((RT_E1, l1 - ROUTER_E0), (RT_E2, l2 - ROUTER_E0), (RT_W1, w1), (RT_W2, w2), (RT_R1, r1), (RT_R2, r2)):
        rt = jnp.where(lane == ln, val, rt)
    rt_ref[...] = rt


def _merge_call(h, oa, ob, oc, od, gain, wg, wb, wo, fgain, wr, br):
    T, D = h.shape
    tm = TM_MERGE
    tile = lambda i: (i, 0)
    c2 = lambda i: (0, 0)
    c3 = lambda i: (0, 0, 0)
    once = pl.Buffered(1)
    in_specs = [pl.BlockSpec((tm, D), tile)] + [pl.BlockSpec((tm, 256), tile)] * 4 + [
        pl.BlockSpec((1, D), c2),
        pl.BlockSpec(wg.shape, c3, pipeline_mode=once),
        pl.BlockSpec(wb.shape, c3, pipeline_mode=once),
        pl.BlockSpec(wo.shape, c2, pipeline_mode=once),
        pl.BlockSpec((1, D), c2),
        pl.BlockSpec(wr.shape, c2),
        pl.BlockSpec((1, LANES), c2),
    ]
    out_shape = [jax.ShapeDtypeStruct((T, D), F32), jax.ShapeDtypeStruct((T, D), F32),
                 jax.ShapeDtypeStruct((T, LANES), F32), jax.ShapeDtypeStruct((8, LANES), F32)]
    out_specs = [pl.BlockSpec((tm, D), tile), pl.BlockSpec((tm, D), tile), pl.BlockSpec((tm, LANES), tile),
                 pl.BlockSpec((8, LANES), c2)]
    return pl.pallas_call(
        _merge_kernel, grid=(T // tm,), in_specs=in_specs, out_specs=out_specs, out_shape=out_shape,
        scratch_shapes=[pltpu.VMEM((8, LANES), F32)],
        compiler_params=_cparams(("arbitrary",)),
    )(h, oa, ob, oc, od, gain, wg, wb, wo, fgain, wr, br)


def _row_copy(src, src_row, dst, dst_row, sem):
    return pltpu.make_async_copy(src.at[pl.ds(src_row, 1)], dst.at[pl.ds(dst_row, 1)], sem)


def _dest_kernel(rt_ref, ps_ref, d_ref):
    rt = rt_ref[...]
    lane = _iota(rt.shape, 1)
    lanef = lane.astype(F32)
    ps = ps_ref[...]

    def dest(e_lane, r_lane):
        start = jnp.sum(jnp.where(lanef == rt[:, e_lane:e_lane + 1], ps, 0.0), axis=-1, keepdims=True)
        return start + rt[:, r_lane:r_lane + 1]

    d = jnp.where(lane == 0, dest(RT_E1, RT_R1), jnp.where(lane == 1, dest(RT_E2, RT_R2), 0.0))
    d_ref[...] = d.astype(I32)


def _dest_call(rt, pstart_row):
    T = rt.shape[0]
    tm = TM_MERGE
    return pl.pallas_call(
        _dest_kernel, grid=(T // tm,),
        in_specs=[pl.BlockSpec((tm, LANES), lambda i: (i, 0)), pl.BlockSpec((1, LANES), lambda i: (0, 0))],
        out_specs=pl.BlockSpec((tm, LANES), lambda i: (i, 0)), out_shape=jax.ShapeDtypeStruct((T, LANES), I32),
        compiler_params=_cparams(("arbitrary",)),
    )(rt, pstart_row)


DMA_ISSUE_UNROLL = 8


def _dispatch_kernel(d1_ref, d2_ref, zero_blk_ref, xn_ref, xs_ref, zero_ref, sem, zero_sem):
    base = pl.program_id(0) * TD_DISPATCH

    @pl.when(pl.program_id(0) == 0)
    def _():
        zero_ref[...] = jnp.zeros_like(zero_ref)

        def zero_copy(j):
            row0 = pl.multiple_of(j * MOE_ROW_BLOCK, MOE_ROW_BLOCK)
            return pltpu.make_async_copy(zero_ref, xs_ref.at[pl.ds(row0, MOE_ROW_BLOCK)], zero_sem)

        def start_zero(j, c):
            pl.when(zero_blk_ref[j] != 0)(lambda: zero_copy(j).start())
            return c

        def wait_zero(j, c):
            pl.when(zero_blk_ref[j] != 0)(lambda: zero_copy(j).wait())
            return c

        n_blocks = xs_ref.shape[0] // MOE_ROW_BLOCK
        lax.fori_loop(0, n_blocks, start_zero, 0)
        lax.fori_loop(0, n_blocks, wait_zero, 0)

    def copies(r):
        t = base + r
        return (_row_copy(xn_ref, r, xs_ref, d1_ref[t], sem), _row_copy(xn_ref, r, xs_ref, d2_ref[t], sem))

    def start(r, c):
        for priority, cp in enumerate(copies(r)):
            cp.start(priority=priority)
        return c

    def wait(r, c):
        for cp in copies(r):
            cp.wait()
        return c

    lax.fori_loop(0, TD_DISPATCH, start, 0, unroll=DMA_ISSUE_UNROLL)
    lax.fori_loop(0, TD_DISPATCH, wait, 0, unroll=DMA_ISSUE_UNROLL)


def _dispatch_call(d1, d2, zero_blk, xn, n_rows):
    T, D = xn.shape
    grid_spec = pltpu.PrefetchScalarGridSpec(
        num_scalar_prefetch=3, grid=(T // TD_DISPATCH,),
        in_specs=[pl.BlockSpec((TD_DISPATCH, D), lambda i, *_: (i, 0))],
        out_specs=pl.BlockSpec(memory_space=pl.ANY),
        scratch_shapes=[pltpu.VMEM((MOE_ROW_BLOCK, D), xn.dtype), pltpu.SemaphoreType.DMA(()),
                        pltpu.SemaphoreType.DMA(())])
    return pl.pallas_call(
        _dispatch_kernel, grid_spec=grid_spec, out_shape=jax.ShapeDtypeStruct((n_rows, D), xn.dtype),
        compiler_params=pltpu.CompilerParams(dimension_semantics=("arbitrary",), has_side_effects=True,
                                             vmem_limit_bytes=VMEM_LIMIT),
    )(d1, d2, zero_blk, xn)


def _expert_kernel(be_ref, used_ref, x_ref, wg_ref, wu_ref, wd_ref, y_ref):
    i = pl.program_id(0)

    @pl.when(i < used_ref[0])
    def _():
        x = x_ref[...].astype(BF16)
        g = _dot(x, wg_ref[0, 0].astype(BF16))
        u = _dot(x, wu_ref[0, 0].astype(BF16))
        act = g / (1.0 + jnp.exp(-g)) * u
        y_ref[...] = _dot(act.astype(BF16), wd_ref[0, 0].astype(BF16))

    @pl.when(i >= used_ref[0])
    def _():
        y_ref[...] = jnp.zeros_like(y_ref)


def _expert_call(blk_expert, used, xs, wg, wu, wd, layer):
    rows, D = xs.shape
    rb = MOE_ROW_BLOCK
    DE = wg.shape[3]
    xmap = lambda i, be, used: (jnp.maximum(jnp.minimum(i, used[0] - 1), 0), 0)
    wmap = lambda i, be, used: (layer, be[i], 0, 0)
    grid_spec = pltpu.PrefetchScalarGridSpec(
        num_scalar_prefetch=2, grid=(rows // rb,),
        in_specs=[pl.BlockSpec((rb, D), xmap), pl.BlockSpec((1, 1, D, DE), wmap), pl.BlockSpec((1, 1, D, DE), wmap),
                  pl.BlockSpec((1, 1, DE, D), wmap)],
        out_specs=pl.BlockSpec((rb, D), lambda i, be, used: (i, 0)))
    return pl.pallas_call(
        _expert_kernel, grid_spec=grid_spec, out_shape=jax.ShapeDtypeStruct((rows, D), F32),
        compiler_params=_cparams(("arbitrary",)),
    )(blk_expert, used, xs, wg, wu, wd)


def _combine_kernel(final_norm, d1_ref, d2_ref, h_ref, rt_ref, gain_ref, ys_ref, o_ref, buf_ref, sem):
    base = pl.program_id(0) * TC_COMBINE

    def copies(r):
        t = base + r
        return (_row_copy(ys_ref, d1_ref[t], buf_ref.at[0], r, sem),
                _row_copy(ys_ref, d2_ref[t], buf_ref.at[1], r, sem))

    def start(r, c):
        for priority, cp in enumerate(copies(r)):
            cp.start(priority=priority)
        return c

    def wait(r, c):
        for cp in copies(r):
            cp.wait()
        return c

    lax.fori_loop(0, TC_COMBINE, start, 0, unroll=DMA_ISSUE_UNROLL)
    lax.fori_loop(0, TC_COMBINE, wait, 0, unroll=DMA_ISSUE_UNROLL)
    rt = rt_ref[...]
    out = h_ref[...] + rt[:, RT_W1:RT_W1 + 1] * buf_ref[0] + rt[:, RT_W2:RT_W2 + 1] * buf_ref[1]
    if final_norm:
        out = out * lax.rsqrt(jnp.mean(out * out, axis=-1, keepdims=True) + NORM_EPS) * gain_ref[...]
    o_ref[...] = out


def _combine_call(d1, d2, h, rt, gain, ys, final_norm):
    T, D = h.shape
    tc = TC_COMBINE
    tile = lambda i, *_: (i, 0)
    grid_spec = pltpu.PrefetchScalarGridSpec(
        num_scalar_prefetch=2, grid=(T // tc,),
        in_specs=[pl.BlockSpec((tc, D), tile), pl.BlockSpec((tc, LANES), tile),
                  pl.BlockSpec((1, D), lambda i, *_: (0, 0)), pl.BlockSpec(memory_space=pl.ANY)],
        out_specs=pl.BlockSpec((tc, D), tile),
        scratch_shapes=[pltpu.VMEM((2, tc, D), F32), pltpu.SemaphoreType.DMA(())])
    return pl.pallas_call(
        functools.partial(_combine_kernel, final_norm), grid_spec=grid_spec,
        out_shape=jax.ShapeDtypeStruct((T, D), F32),
        compiler_params=_cparams(("arbitrary",)),
    )(d1, d2, h, rt, gain, ys)


def _in_proj_columns():
    widths = (('q_a', 256), ('k_cmp', 64), ('v_cmp', 64), ('k_sel', 64), ('v_sel', 64), ('k_win', 64), ('v_win', 64),
              ('g_a', 12), ('q_b', 256), ('k_b', 256), ('v_b', 256), ('f_b', 4), ('q_c', 256), ('ckv_c', 128),
              ('qi_c', 256), ('ki_c', 64), ('wi_c', 4), ('q_d', 256), ('k_d', 128), ('v_d', 128))
    off, cols = 0, {}
    for name, w in widths:
        cols[name] = (off, off + w)
        off += w
    rope = ('q_a', 'q_c', 'qi_c', 'q_d', 'k_d', 'k_cmp', 'k_sel', 'k_win', 'ki_c')
    plain = ('q_b', 'k_b', 'v_b', 'v_d', 'ckv_c', 'v_cmp', 'v_sel', 'v_win')
    small = ('g_a', 'f_b', 'wi_c')
    width = lambda names: sum(cols[n][1] - cols[n][0] for n in names)
    assert width(rope) == N_ROPE_COLS and width(plain) == N_PLAIN_COLS
    return [cols[n] for n in rope + plain + small], N_SMALL_COLS - width(small)


def _static_tables(S):
    n_cmp_pad = S // NSA_CMP_STRIDE
    n_blk = S // NSA_SEL_BLOCK
    c0 = np.arange(n_cmp_pad) * NSA_CMP_STRIDE
    b0 = np.arange(n_blk) * NSA_SEL_BLOCK
    ov = ((c0[:, None] < b0[None, :] + NSA_SEL_BLOCK) & (c0[:, None] + NSA_CMP_LEN > b0[None, :])).astype(np.float32)
    overlap = np.ascontiguousarray(ov.T)
    expand = np.zeros((LANES, S), np.float32)
    expand[np.arange(S) // NSA_SEL_BLOCK, np.arange(S)] = 1.0
    strict_lower = (np.arange(LANES)[:, None] > np.arange(LANES)[None, :]).astype(np.float32)
    cntmat = np.concatenate([strict_lower, np.ones((16, LANES), np.float32)], axis=0)
    return jnp.asarray(overlap, BF16), jnp.asarray(expand, BF16), jnp.asarray(cntmat, BF16)


def kernel(x, positions, attn_norm, w_in, nsa_pe_k, nsa_w1_k, nsa_w2_k, nsa_pe_v, nsa_w1_v, nsa_w2_v, fox_forget_bias, dsa_kv_norm, dsa_w_ukv, swa_sinks, w_branch, w_gate, w_out, ffn_norm, moe_w_group, moe_b_group, moe_w_expert, moe_b_expert, moe_w_gate, moe_w_up, moe_w_down, final_norm):
    B, S, D = x.shape
    T = B * S
    depth = w_in.shape[0]
    col_ranges, n_pad = _in_proj_columns()
    overlap, expand, cntmat = _static_tables(S)
    half = HEAD_DIM // 2
    inv_freq = ROPE_THETA ** (-jnp.arange(half, dtype=F32) / half)
    invf = jnp.tile(inv_freq, LANES // half).reshape(1, LANES)
    pos = positions.reshape(T, 1).astype(I32)
    small_pad = lambda v, off: jnp.zeros((1, LANES), F32).at[0, off:off + v.shape[0]].set(v.astype(F32))

    n_rows = -(-(T * 2 + MOE_N_EXPERTS * (MOE_ROW_BLOCK - 1)) // MOE_ROW_BLOCK) * MOE_ROW_BLOCK
    h = x.reshape(T, D)
    for l in range(depth):
        w = jnp.concatenate([w_in[l][:, a:b] for a, b in col_ranges] + [jnp.zeros((D, n_pad), F32)],
                            axis=1).astype(BF16)
        (qa, qc, qi, qd, kd, kcmp, ksel, kwin, ki, qb, kb, vb, vd, vcmp, vsel, vwin, kc, vc, sm) = _proj_call(
            h, pos, invf, attn_norm[l].reshape(1, D), w, small_pad(fox_forget_bias[l], SM_F0),
            dsa_kv_norm[l].reshape(1, DSA_KV_RANK), dsa_w_ukv[l].astype(BF16), B, S)

        cw = NSA_CMP_STRIDE * HEAD_DIM
        pe_rows = lambda pe: jnp.broadcast_to(pe.reshape(1, 2 * cw), (8, 2 * cw)).astype(BF16)
        kcc, vcc = _cmp_call(kcmp.reshape(T // NSA_CMP_STRIDE, cw), vcmp.reshape(T // NSA_CMP_STRIDE, cw),
                             pe_rows(nsa_pe_k[l]), nsa_w1_k[l].astype(BF16), nsa_w2_k[l].astype(BF16),
                             pe_rows(nsa_pe_v[l]), nsa_w1_v[l].astype(BF16), nsa_w2_v[l].astype(BF16), B, S)
        o_a = _nsa_call(qa, sm, kcc, vcc, overlap, expand, ksel, vsel, kwin, vwin, B, S)

        key_bias = sm[:, SM_F0:SM_F0 + N_HEADS].reshape(B, S, N_HEADS).transpose(0, 2, 1)
        o_b, o_d = _fox_swa_call(qb, key_bias, kb, vb, qd, small_pad(swa_sinks[l], 0), kd, vd, B, S)
        idx_w = sm[:, SM_W0:SM_W0 + N_HEADS].reshape(B, S, N_HEADS).transpose(0, 2, 1)
        o_c = _dsa_call(qc, qi, idx_w, ki, cntmat, kc, vc, B, S)

        wr = jnp.zeros((D, LANES), F32).at[:, :MOE_GROUPS].set(moe_w_group[l]) \
            .at[:, ROUTER_E0:ROUTER_E0 + MOE_N_EXPERTS].set(moe_w_expert[l])
        br = jnp.zeros((1, LANES), F32).at[0, :MOE_GROUPS].set(moe_b_group[l]) \
            .at[0, ROUTER_E0:ROUTER_E0 + MOE_N_EXPERTS].set(moe_b_expert[l])
        h2, xn, rt, cnt = _merge_call(h, o_a, o_b, o_c, o_d, attn_norm[l].reshape(1, D), w_gate[l].astype(BF16),
                                      w_branch[l].astype(BF16), w_out[l].astype(BF16), ffn_norm[l].reshape(1, D),
                                      wr, br)

        counts = cnt[0, ROUTER_E0:ROUTER_E0 + MOE_N_EXPERTS].astype(I32)
        padded = (counts + MOE_ROW_BLOCK - 1) // MOE_ROW_BLOCK * MOE_ROW_BLOCK
        p_end = jnp.cumsum(padded)
        p_start = p_end - padded
        n_blocks = n_rows // MOE_ROW_BLOCK
        blk_first_row = jnp.arange(n_blocks, dtype=I32) * MOE_ROW_BLOCK
        blk_expert = jnp.minimum(jnp.sum((p_end[None, :] <= blk_first_row[:, None]).astype(I32), axis=1),
                                 MOE_N_EXPERTS - 1).astype(I32)
        used = (p_end[-1:] // MOE_ROW_BLOCK).astype(I32)
        dest = _dest_call(rt, small_pad(p_start, 0))
        d1, d2 = dest[:, 0], dest[:, 1]

        blk = jnp.arange(n_blocks, dtype=I32)
        last_of_segment = jnp.any((blk[:, None] == (p_end // MOE_ROW_BLOCK - 1)[None, :]) & (padded > 0)[None, :], axis=1)
        zero_blk = (last_of_segment | (blk >= used[0])).astype(I32)
        xs = _dispatch_call(d1, d2, zero_blk, xn, n_rows)
        ys = _expert_call(blk_expert, used, xs, moe_w_gate, moe_w_up, moe_w_down, l)
        last = l == depth - 1
        h = _combine_call(d1, d2, h2, rt, final_norm.reshape(1, D), ys, last)
    return h.reshape(B, S, D)
```

```python
import functools
import math

import numpy as np
import jax
import jax.numpy as jnp
from jax import lax
from jax.experimental import pallas as pl
from jax.experimental.pallas import tpu as pltpu

F32 = jnp.float32
BF16 = jnp.bfloat16
I32 = jnp.int32

HEAD_DIM = 64
N_HEADS = 4
MIX_WIDTH = N_HEADS * HEAD_DIM
ROPE_THETA = 10000.0
NORM_EPS = 1e-6
NEG_INF = -1e30
INT_MIN = -2 ** 31
LOG2E = float(np.log2(np.e))

NSA_CMP_LEN = 32
NSA_CMP_STRIDE = 16
NSA_CMP_HIDDEN = 256
NSA_SEL_BLOCK = 64
NSA_N_SEL = 16
NSA_WINDOW = 512
NSA_FORCE_SCORE = 1e4
DSA_KV_RANK = 128
DSA_TOPK_MAX = 256
DSA_COUNT_CHAINS = 4
SWA_WINDOW = 128
SWA_KV_HEADS = 2
MOE_GROUPS = 8
MOE_EXPERTS_PER_GROUP = 8
MOE_N_EXPERTS = 64
MOE_D_EXPERT = 256

LANES = 128
VMEM_LIMIT = 48 * 1024 * 1024

N_ROPE_COLS = 1408
N_PLAIN_COLS = 1216
N_SMALL_COLS = 128
N_PROJ_COLS = N_ROPE_COLS + N_PLAIN_COLS + N_SMALL_COLS
SM_GATE0, SM_F0, SM_W0 = 0, 12, 16

TM_PROJ = 512
PROJ_CHUNK = 128
TQ = 256
TM_MERGE = 512
MERGE_CHUNK = 128
MOE_ROW_BLOCK = 256
TD_DISPATCH = 256
TC_COMBINE = 256


def _cparams(sem):
    return pltpu.CompilerParams(dimension_semantics=sem, vmem_limit_bytes=VMEM_LIMIT)


def _dot(a, b):
    return jnp.dot(a, b, preferred_element_type=F32)


def _dot_t(a, b):
    return lax.dot_general(a, b, (((1,), (1,)), ((), ())), preferred_element_type=F32)


def _iota(shape, dim):
    return lax.broadcasted_iota(I32, shape, dim)


def _rope_slab(x, cos, sin_signed, first_half):
    rot = jnp.where(first_half, pltpu.roll(x, 96, 1), pltpu.roll(x, 32, 1))
    return x * cos + rot * sin_signed


def _proj_kernel(h_ref, pos_ref, invf_ref, gain_ref, w_ref, fbias_ref, kvn_ref, wukv_ref,
                 qa_ref, qc_ref, qi_ref, qd_ref, kd_ref, kcmp_ref, ksel_ref, kwin_ref, ki_ref,
                 qb_ref, kb_ref, vb_ref, vd_ref, vcmp_ref, vsel_ref, vwin_ref, kc_ref, vc_ref,
                 sm_ref, carry_ref):
    j = pl.program_id(1)
    tm = h_ref.shape[0]
    tc = PROJ_CHUNK

    @pl.when(j == 0)
    def _():
        carry_ref[...] = jnp.zeros_like(carry_ref)

    lane = _iota((tc, LANES), 1)
    first_half = (lane % HEAD_DIM) < (HEAD_DIM // 2)
    is_f = (lane >= SM_F0) & (lane < SM_W0)
    tri = (_iota((tc, tc), 0) >= _iota((tc, tc), 1)).astype(F32)
    scale = HEAD_DIM ** -0.5
    softmax_scale = scale * LOG2E

    for rows in (slice(c * tc, (c + 1) * tc) for c in range(tm // tc)):
        h = h_ref[rows, :]
        hn = h * lax.rsqrt(jnp.mean(h * h, axis=-1, keepdims=True) + NORM_EPS) * gain_ref[...]
        hb = hn.astype(BF16)

        ang = pos_ref[rows, :].astype(F32) * invf_ref[...]
        cos = jnp.cos(ang)
        sin = jnp.sin(ang)
        sin_signed = jnp.where(first_half, -sin, sin)
        rope = lambda x: _rope_slab(x, cos, sin_signed, first_half)

        zr = _dot(hb, w_ref[:, 0:N_ROPE_COLS])
        slab = lambda k: rope(zr[:, k * LANES:(k + 1) * LANES])
        for q_ref, k0, sc in ((qa_ref, 0, softmax_scale), (qc_ref, 2, softmax_scale), (qi_ref, 4, scale),
                              (qd_ref, 6, softmax_scale)):
            q_ref[rows, :] = (jnp.concatenate([slab(k0), slab(k0 + 1)], axis=1) * sc).astype(BF16)
        s8 = slab(8).astype(BF16)
        kd_ref[0, rows, :] = s8[:, :HEAD_DIM]
        kd_ref[1, rows, :] = s8[:, HEAD_DIM:]
        s9 = slab(9).astype(BF16)
        kcmp_ref[rows, :] = s9[:, :HEAD_DIM]
        ksel_ref[rows, :] = s9[:, HEAD_DIM:]
        s10 = slab(10).astype(BF16)
        kwin_ref[rows, :] = s10[:, :HEAD_DIM]
        ki_ref[rows, :] = s10[:, HEAD_DIM:]

        zp = _dot(hb, w_ref[:, N_ROPE_COLS:N_ROPE_COLS + N_PLAIN_COLS])
        qb_ref[rows, :] = (zp[:, 0:256] * softmax_scale).astype(BF16)
        for hh in range(N_HEADS):
            kb_ref[hh, rows, :] = zp[:, 256 + hh * 64:256 + (hh + 1) * 64].astype(BF16)
            vb_ref[hh, rows, :] = zp[:, 512 + hh * 64:512 + (hh + 1) * 64].astype(BF16)
        vd_ref[0, rows, :] = zp[:, 768:832].astype(BF16)
        vd_ref[1, rows, :] = zp[:, 832:896].astype(BF16)
        ckv = zp[:, 896:1024]
        vcmp_ref[rows, :] = zp[:, 1024:1088].astype(BF16)
        vsel_ref[rows, :] = zp[:, 1088:1152].astype(BF16)
        vwin_ref[rows, :] = zp[:, 1152:1216].astype(BF16)

        ckvn = ckv * lax.rsqrt(jnp.mean(ckv * ckv, axis=-1, keepdims=True) + NORM_EPS) * kvn_ref[...]
        kvc = _dot(ckvn.astype(BF16), wukv_ref[...])
        for k in range(2):
            kk = rope(kvc[:, k * LANES:(k + 1) * LANES]).astype(BF16)
            kc_ref[2 * k, rows, :] = kk[:, :HEAD_DIM]
            kc_ref[2 * k + 1, rows, :] = kk[:, HEAD_DIM:]
        for hh in range(N_HEADS):
            vc_ref[hh, rows, :] = kvc[:, 256 + hh * 64:256 + (hh + 1) * 64].astype(BF16)

        zs = _dot(hb, w_ref[:, N_ROPE_COLS + N_PLAIN_COLS:N_PROJ_COLS])
        sig = 1.0 / (1.0 + jnp.exp(-zs))
        xf = zs + fbias_ref[...]
        logf = jnp.minimum(xf, 0.0) - jnp.log(1.0 + jnp.exp(-jnp.abs(xf)))
        logf = jnp.where(is_f, logf, 0.0)
        cum = jnp.dot(tri, logf, preferred_element_type=F32, precision=lax.Precision.HIGHEST)
        cum = cum + carry_ref[0:1, :]
        carry_ref[0:1, :] = cum[tc - 1:tc, :]
        sm_ref[rows, :] = jnp.where(lane < SM_F0, sig,
                                    jnp.where(is_f, -LOG2E * cum, jnp.where(lane < SM_W0 + 4, 0.5 * zs, 0.0)))


def _proj_call(h, pos, invf, gain, w, fbias, kvn, wukv, B, S):
    T = B * S
    tm = TM_PROJ
    nj = S // tm
    row = lambda b, j: (b * nj + j, 0)
    hrow = lambda b, j: (0, b * nj + j, 0)
    const2 = lambda b, j: (0, 0)
    tok = lambda w_, dt: jax.ShapeDtypeStruct((T, w_), dt)
    hm = lambda n: jax.ShapeDtypeStruct((n, T, HEAD_DIM), BF16)
    out_shape = [tok(256, BF16)] * 4 + [hm(2)] + [tok(64, BF16)] * 4 + [tok(256, BF16), hm(4), hm(4), hm(2)] \
        + [tok(64, BF16)] * 3 + [hm(4), hm(4), tok(LANES, F32)]
    spec_tok = lambda w_: pl.BlockSpec((tm, w_), row)
    spec_hm = lambda n: pl.BlockSpec((n, tm, HEAD_DIM), hrow)
    out_specs = [spec_tok(256)] * 4 + [spec_hm(2)] + [spec_tok(64)] * 4 + [spec_tok(256), spec_hm(4), spec_hm(4), spec_hm(2)] \
        + [spec_tok(64)] * 3 + [spec_hm(4), spec_hm(4), spec_tok(LANES)]
    in_specs = [
        pl.BlockSpec((tm, h.shape[1]), row),
        pl.BlockSpec((tm, 1), row),
        pl.BlockSpec((1, LANES), const2),
        pl.BlockSpec((1, h.shape[1]), const2),
        pl.BlockSpec(w.shape, const2),
        pl.BlockSpec((1, LANES), const2),
        pl.BlockSpec((1, DSA_KV_RANK), const2),
        pl.BlockSpec(wukv.shape, const2),
    ]
    return pl.pallas_call(
        _proj_kernel, grid=(B, nj), in_specs=in_specs, out_specs=out_specs, out_shape=out_shape,
        scratch_shapes=[pltpu.VMEM((8, LANES), F32)],
        compiler_params=_cparams(("arbitrary", "arbitrary")),
    )(h, pos, invf, gain, w, fbias, kvn, wukv)


def _gelu_tanh(x):
    return 0.5 * x * (1.0 + jnp.tanh(np.sqrt(2.0 / np.pi).astype(np.float32) * (x + 0.044715 * (x * x * x))))


def _cmp_kernel(ck_ref, cv_ref, pek_ref, w1k_ref, w2k_ref, pev_ref, w1v_ref, w2v_ref, kc_ref, vc_ref):
    half = NSA_CMP_STRIDE * HEAD_DIM
    for c_ref, pe_ref, w1_ref, w2_ref, o_ref in ((ck_ref, pek_ref, w1k_ref, w2k_ref, kc_ref),
                                                 (cv_ref, pev_ref, w1v_ref, w2v_ref, vc_ref)):
        c = c_ref[...]
        a = _dot(c, w1_ref[0:half, :])
        b = _dot(c, w1_ref[half:2 * half, :])
        n = b.shape[0]
        pe_h = _dot(pe_ref[...], w1_ref[...])
        hid = a + pltpu.roll(b, n - 1, 0) + pe_h[0:1, :]
        o_ref[...] = _dot(_gelu_tanh(hid).astype(BF16), w2_ref[...]).astype(BF16)


def _cmp_call(ck, cv, pek, w1k, w2k, pev, w1v, w2v, B, S):
    nch = S // NSA_CMP_STRIDE
    blk = lambda b: (b, 0)
    const = lambda b: (0, 0)
    cw = NSA_CMP_STRIDE * HEAD_DIM
    in_specs = [pl.BlockSpec((nch, cw), blk), pl.BlockSpec((nch, cw), blk)]
    for _ in range(2):
        in_specs += [pl.BlockSpec((8, 2 * cw), const), pl.BlockSpec((2 * cw, NSA_CMP_HIDDEN), const),
                     pl.BlockSpec((NSA_CMP_HIDDEN, HEAD_DIM), const)]
    out = jax.ShapeDtypeStruct((B * nch, HEAD_DIM), BF16)
    return pl.pallas_call(
        _cmp_kernel, grid=(B,), in_specs=in_specs,
        out_specs=[pl.BlockSpec((nch, HEAD_DIM), blk)] * 2, out_shape=[out, out],
        compiler_params=_cparams(("arbitrary",)),
    )(ck, cv, pek, w1k, w2k, pev, w1v, w2v)


def _attend(q, segments, extra_logit=None):
    scores = []
    for k, _, bias, allowed in segments:
        s = _dot_t(q, k)
        if bias is not None:
            s = s + bias
        if allowed is not None:
            s = jnp.where(allowed, s, NEG_INF)
        scores.append(s)
    m = functools.reduce(jnp.maximum, [jnp.max(s, axis=-1, keepdims=True) for s in scores])
    if extra_logit is not None:
        m = jnp.maximum(m, extra_logit)
    den = jnp.exp2(extra_logit - m) if extra_logit is not None else 0.0
    o = 0.0
    for s, (_, v, _, _) in zip(scores, segments):
        e = jnp.exp2(s - m)
        den = den + jnp.sum(e, axis=-1, keepdims=True)
        o = o + _dot(e.astype(BF16), v)
    return o / den


def _causal_split(i, tq, nk, S):
    n_full = nk - S // N_CAUSAL_CLASSES
    t_col = i * tq + _iota((tq, 1), 0)
    tail_causal = (n_full + _iota((1, nk - n_full), 1)) <= t_col
    return n_full, tail_causal


N_CAUSAL_CLASSES = 8


def _for_causal_class(i, tq, S, body):
    step = S // N_CAUSAL_CLASSES
    cls = (i * tq + tq - 1) // step
    for c in range(N_CAUSAL_CLASSES):
        pl.when(cls == c)(functools.partial(body, (c + 1) * step))


def _nsa_kernel(q_ref, sm_ref, kc_ref, vc_ref, ov_ref, ex_ref, ksel_ref, vsel_ref, kwin_ref, vwin_ref, o_ref):
    i = pl.program_id(1)
    tq = q_ref.shape[0]
    S = ksel_ref.shape[0]
    ncmp = kc_ref.shape[0]
    q = q_ref[...]
    sm = sm_ref[...]
    t_col = i * tq + _iota((tq, 1), 0)

    c_end = _iota((1, ncmp), 1) * NSA_CMP_STRIDE + (NSA_CMP_LEN - 1)
    allowed_c = c_end <= t_col
    kc = kc_ref[...]
    vc = vc_ref[...]
    o_cmp = []
    p_cmp = []
    for hh in range(N_HEADS):
        qh = q[:, hh * HEAD_DIM:(hh + 1) * HEAD_DIM]
        s = jnp.where(allowed_c, _dot_t(qh, kc), NEG_INF)
        m = jnp.max(s, axis=-1, keepdims=True)
        e = jnp.where(allowed_c, jnp.exp2(s - m), 0.0)
        den = jnp.sum(e, axis=-1, keepdims=True)
        p = e * jnp.where(den > 0.0, 1.0 / den, 0.0)
        p_cmp.append(p.astype(BF16))
        o_cmp.append(_dot(p_cmp[-1], vc))

    n_blk = S // NSA_SEL_BLOCK
    n_sel = min(NSA_N_SEL, n_blk)

    def unselected_block_bias():
        imp = sum(_dot_t(ov_ref[...], p) for p in p_cmp)
        jb = _iota((n_blk, tq), 0)
        cur = (i * tq + _iota((1, tq), 1)) // NSA_SEL_BLOCK
        forced = (jb == 0) | (jb == cur) | (jb == cur - 1)
        imp = jnp.where(forced, NSA_FORCE_SCORE, imp)
        imp = jnp.where(jb > cur, -NSA_FORCE_SCORE, imp)
        rank = jnp.zeros((n_blk, tq), F32)
        for jp in range(n_blk):
            row = imp[jp:jp + 1, :]
            ahead = (row > imp) | ((row == imp) & (jp < jb))
            rank = rank + jnp.where(ahead, 1.0, 0.0)
        unsel = jnp.transpose(jnp.where(rank < float(n_sel), 0.0, NEG_INF))
        return jnp.concatenate([unsel, jnp.zeros((tq, LANES - n_blk), F32)], axis=1).astype(BF16)

    nband = NSA_WINDOW + tq
    start = pl.multiple_of(jnp.maximum(i * tq - NSA_WINDOW, 0), math.gcd(tq, NSA_WINDOW))
    kwin = kwin_ref[pl.ds(start, nband), :]
    vwin = vwin_ref[pl.ds(start, nband), :]
    s_band = start + _iota((1, nband), 1)
    allowed_win = (s_band <= t_col) & (s_band > t_col - NSA_WINDOW)
    gate = lambda hh, k: sm[:, SM_GATE0 + 3 * hh + k:SM_GATE0 + 3 * hh + k + 1]
    partial_out = []
    for hh in range(N_HEADS):
        o_win = _attend(q[:, hh * HEAD_DIM:(hh + 1) * HEAD_DIM], [(kwin, vwin, None, allowed_win)])
        partial_out.append(gate(hh, 0) * o_cmp[hh] + gate(hh, 2) * o_win)

    def selected(nk):
        n_full, tail_causal = _causal_split(i, tq, nk, S)
        if nk <= n_sel * NSA_SEL_BLOCK:
            part_bias = lambda lo, hi: None
        else:
            bias = _dot(unselected_block_bias(), ex_ref[:, 0:nk])
            part_bias = lambda lo, hi: bias[:, lo:hi]
        segments = [(ksel_ref[n_full:nk, :], vsel_ref[n_full:nk, :], part_bias(n_full, nk), tail_causal)]
        if n_full:
            segments.append((ksel_ref[0:n_full, :], vsel_ref[0:n_full, :], part_bias(0, n_full), None))
        outs = []
        for hh in range(N_HEADS):
            o_sel = _attend(q[:, hh * HEAD_DIM:(hh + 1) * HEAD_DIM], segments)
            outs.append(partial_out[hh] + gate(hh, 1) * o_sel)
        o_ref[...] = jnp.concatenate(outs, axis=1).astype(BF16)

    _for_causal_class(i, tq, S, selected)


def _nsa_call(qa, sm, kcc, vcc, overlap, expand, ksel, vsel, kwin, vwin, B, S):
    T = B * S
    nq = S // TQ
    ncmp = S // NSA_CMP_STRIDE
    tile = lambda b, i: (b * nq + i, 0)
    seq = lambda b, i: (b, 0)
    const = lambda b, i: (0, 0)
    in_specs = [pl.BlockSpec((TQ, MIX_WIDTH), tile), pl.BlockSpec((TQ, LANES), tile),
                pl.BlockSpec((ncmp, HEAD_DIM), seq), pl.BlockSpec((ncmp, HEAD_DIM), seq),
                pl.BlockSpec(overlap.shape, const), pl.BlockSpec(expand.shape, const)] \
        + [pl.BlockSpec((S, HEAD_DIM), seq)] * 4
    return pl.pallas_call(
        _nsa_kernel, grid=(B, nq), in_specs=in_specs, out_specs=pl.BlockSpec((TQ, MIX_WIDTH), tile),
        out_shape=jax.ShapeDtypeStruct((T, MIX_WIDTH), BF16),
        compiler_params=_cparams(("arbitrary", "arbitrary")),
    )(qa, sm, kcc, vcc, overlap, expand, ksel, vsel, kwin, vwin)


def _fox_swa_kernel(q_ref, nbias_ref, k_ref, v_ref, qd_ref, sink_ref, kd_ref, vd_ref, o_ref, od_ref):
    i = pl.program_id(1)
    tq = q_ref.shape[0]
    S = k_ref.shape[1]
    q = q_ref[...]

    def attend(nk):
        n_full, tail_causal = _causal_split(i, tq, nk, S)
        outs = []
        for hh in range(N_HEADS):
            segments = [(k_ref[hh, n_full:nk, :], v_ref[hh, n_full:nk, :], nbias_ref[0, hh:hh + 1, n_full:nk],
                         tail_causal)]
            if n_full:
                segments.append((k_ref[hh, 0:n_full, :], v_ref[hh, 0:n_full, :], nbias_ref[0, hh:hh + 1, 0:n_full],
                                 None))
            outs.append(_attend(q[:, hh * HEAD_DIM:(hh + 1) * HEAD_DIM], segments))
        o_ref[...] = jnp.concatenate(outs, axis=1).astype(BF16)
        _swa_tile(i, qd_ref, sink_ref, kd_ref, vd_ref, od_ref)

    _for_causal_class(i, tq, S, attend)


def _fox_swa_call(qb, nbias, kb, vb, qd, sinks, kd, vd, B, S):
    T = B * S
    nq = S // TQ
    tile = lambda b, i: (b * nq + i, 0)
    heads = lambda n: pl.BlockSpec((n, S, HEAD_DIM), lambda b, i: (0, b, 0))
    in_specs = [pl.BlockSpec((TQ, MIX_WIDTH), tile), pl.BlockSpec((1, N_HEADS, S), lambda b, i: (b, 0, 0)),
                heads(N_HEADS), heads(N_HEADS),
                pl.BlockSpec((TQ, MIX_WIDTH), tile), pl.BlockSpec((1, LANES), lambda b, i: (0, 0)),
                heads(SWA_KV_HEADS), heads(SWA_KV_HEADS)]
    out = jax.ShapeDtypeStruct((T, MIX_WIDTH), BF16)
    return pl.pallas_call(
        _fox_swa_kernel, grid=(B, nq), in_specs=in_specs, out_specs=[pl.BlockSpec((TQ, MIX_WIDTH), tile)] * 2,
        out_shape=[out, out], compiler_params=_cparams(("arbitrary", "arbitrary")),
    )(qb, nbias, kb, vb, qd, sinks, kd, vd)


def _dsa_kernel(q_ref, qi_ref, wrow_ref, ki_ref, cnt_ref, k_ref, v_ref, o_ref, score_ref):
    i = pl.program_id(1)
    tq = q_ref.shape[0]
    S = ki_ref.shape[0]
    qi = qi_ref[...]
    t_col = i * tq + _iota((tq, 1), 0)
    top_k = min(DSA_TOPK_MAX, S // 4)
    q = q_ref[...]

    def attend(nk, bias, allowed):
        outs = [_attend(q[:, hh * HEAD_DIM:(hh + 1) * HEAD_DIM], [(k_ref[hh, 0:nk, :], v_ref[hh, 0:nk, :], bias, allowed)])
                for hh in range(N_HEADS)]
        o_ref[...] = jnp.concatenate(outs, axis=1).astype(BF16)

    def float_of_ordered_bits(u):
        k = u ^ INT_MIN
        return lax.bitcast_convert_type(jnp.where(k >= 0, k, k ^ 0x7FFFFFFF), F32)

    def select_and_attend(nk):
        t_row = i * tq + _iota((1, tq), 1)
        k_eff = jnp.minimum(t_row + 1, top_k).astype(F32)
        w_rows = wrow_ref[0]
        score = jnp.zeros((nk, tq), F32)
        for hh in range(N_HEADS):
            lg = _dot_t(ki_ref[0:nk, :], qi[:, hh * HEAD_DIM:(hh + 1) * HEAD_DIM])
            score = score + w_rows[hh:hh + 1, :] * jnp.maximum(lg, 0.0)
        score_ref[0:nk, :] = jnp.where(_iota((nk, 1), 0) <= t_row, score, -jnp.inf)

        def search(it, u):
            cand = u | lax.shift_left(jnp.int32(1), 31 - it)
            thr = float_of_ordered_bits(cand)
            part = nk // DSA_COUNT_CHAINS
            cnt = sum(jnp.sum(jnp.where(score_ref[g * part:(g + 1) * part, :] >= thr, 1.0, 0.0), axis=0, keepdims=True)
                      for g in range(DSA_COUNT_CHAINS))
            return jnp.where(cnt >= k_eff, cand, u)

        u = lax.fori_loop(0, 32, search, jnp.zeros((1, tq), I32))
        thr = float_of_ordered_bits(u)
        score = score_ref[0:nk, :]
        gt = score > thr
        eq = score == thr
        need = k_eff - jnp.sum(jnp.where(gt, 1.0, 0.0), axis=0, keepdims=True)
        eqb = jnp.where(eq, 1.0, 0.0).astype(BF16)
        run = jnp.zeros((1, tq), F32)
        bias_chunks = []
        for c in range(nk // LANES):
            sl = slice(c * LANES, (c + 1) * LANES)
            r = _dot(cnt_ref[...], eqb[sl, :])
            prefix = r[0:LANES, :] + run
            run = run + r[LANES:LANES + 1, :]
            keep = gt[sl, :] | (eq[sl, :] & (prefix < need))
            bias_chunks.append(jnp.transpose(jnp.where(keep, 0.0, NEG_INF)))
        attend(nk, jnp.concatenate(bias_chunks, axis=1), None)

    def body(nk):
        if nk * N_CAUSAL_CLASSES == S:
            all_kept = (i + 1) * tq <= top_k
            pl.when(all_kept)(lambda: attend(nk, None, _iota((1, nk), 1) <= t_col))
            pl.when(jnp.logical_not(all_kept))(lambda: select_and_attend(nk))
        else:
            select_and_attend(nk)

    _for_causal_class(i, tq, S, body)


def _dsa_call(qc, qi, w_rows, ki, cntmat, kc, vc, B, S):
    T = B * S
    nq = S // TQ
    tile = lambda b, i: (b * nq + i, 0)
    in_specs = [pl.BlockSpec((TQ, MIX_WIDTH), tile), pl.BlockSpec((TQ, MIX_WIDTH), tile),
                pl.BlockSpec((1, N_HEADS, TQ), lambda b, i: (b, 0, i)),
                pl.BlockSpec((S, HEAD_DIM), lambda b, i: (b, 0)),
                pl.BlockSpec(cntmat.shape, lambda b, i: (0, 0)),
                pl.BlockSpec((N_HEADS, S, HEAD_DIM), lambda b, i: (0, b, 0)),
                pl.BlockSpec((N_HEADS, S, HEAD_DIM), lambda b, i: (0, b, 0))]
    return pl.pallas_call(
        _dsa_kernel, grid=(B, nq), in_specs=in_specs, out_specs=pl.BlockSpec((TQ, MIX_WIDTH), tile),
        out_shape=jax.ShapeDtypeStruct((T, MIX_WIDTH), BF16),
        scratch_shapes=[pltpu.VMEM((S, TQ), F32)],
        compiler_params=_cparams(("arbitrary", "arbitrary")),
    )(qc, qi, w_rows, ki, cntmat, kc, vc)


def _swa_tile(i, q_ref, sink_ref, k_ref, v_ref, o_ref):
    tq = q_ref.shape[0]
    q = q_ref[...]
    t_col = i * tq + _iota((tq, 1), 0)
    nband = SWA_WINDOW + tq
    start = pl.multiple_of(jnp.maximum(i * tq - SWA_WINDOW, 0), math.gcd(tq, SWA_WINDOW))
    s_band = start + _iota((1, nband), 1)
    allowed = (s_band <= t_col) & (s_band > t_col - SWA_WINDOW)
    group = N_HEADS // SWA_KV_HEADS
    outs = []
    for hh in range(N_HEADS):
        k = k_ref[hh // group, pl.ds(start, nband), :]
        v = v_ref[hh // group, pl.ds(start, nband), :]
        sink = jnp.zeros((tq, 1), F32) + LOG2E * sink_ref[0:1, hh:hh + 1]
        outs.append(_attend(q[:, hh * HEAD_DIM:(hh + 1) * HEAD_DIM], [(k, v, None, allowed)], extra_logit=sink))
    o_ref[...] = jnp.concatenate(outs, axis=1).astype(BF16)


RT_E1, RT_E2, RT_W1, RT_W2, RT_R1, RT_R2 = 0, 1, 2, 3, 4, 5
ROUTER_E0 = MOE_GROUPS


def _merge_kernel(h_ref, oa_ref, ob_ref, oc_ref, od_ref, gain_ref, wg_ref, wb_ref, wo_ref,
                  fgain_ref, wr_ref, br_ref, h2_ref, xn_ref, rt_ref, cnt_ref, base_ref):
    step = pl.program_id(0)
    tm = h_ref.shape[0]
    logit_chunks = []
    for rows in (slice(c * MERGE_CHUNK, (c + 1) * MERGE_CHUNK) for c in range(tm // MERGE_CHUNK)):
        h = h_ref[rows, :]
        hn = h * lax.rsqrt(jnp.mean(h * h, axis=-1, keepdims=True) + NORM_EPS) * gain_ref[...]
        hb = hn.astype(BF16)
        merged = jnp.zeros_like(h)
        for i, o_ref in enumerate((oa_ref, ob_ref, oc_ref, od_ref)):
            gate = 1.0 / (1.0 + jnp.exp(-_dot(hb, wg_ref[i])))
            merged = merged + gate * _dot(o_ref[rows, :], wb_ref[i])
        h2 = h + _dot(merged.astype(BF16), wo_ref[...])
        h2_ref[rows, :] = h2
        xn = h2 * lax.rsqrt(jnp.mean(h2 * h2, axis=-1, keepdims=True) + NORM_EPS) * fgain_ref[...]
        xn_ref[rows, :] = xn
        logit_chunks.append(jnp.dot(xn, wr_ref[...], preferred_element_type=F32, precision=lax.Precision.HIGHEST))
    logits = jnp.concatenate(logit_chunks, axis=0) + br_ref[...]

    lane = _iota((tm, LANES), 1)
    lanef = lane.astype(F32)
    big = float(LANES)
    is_g = lane < MOE_GROUPS
    gl = jnp.where(is_g, logits, NEG_INF)
    gmax = jnp.max(gl, axis=-1, keepdims=True)
    grp = jnp.min(jnp.where(is_g & (gl == gmax), lanef, big), axis=-1, keepdims=True)
    p_grp = 1.0 / jnp.sum(jnp.where(is_g, jnp.exp(gl - gmax), 0.0), axis=-1, keepdims=True)
    lo = ROUTER_E0 + grp * MOE_EXPERTS_PER_GROUP
    in_grp = (lanef >= lo) & (lanef < lo + MOE_EXPERTS_PER_GROUP)
    el = jnp.where(in_grp, logits, NEG_INF)
    v1 = jnp.max(el, axis=-1, keepdims=True)
    l1 = jnp.min(jnp.where(in_grp & (el == v1), lanef, big), axis=-1, keepdims=True)
    el2 = jnp.where(lanef == l1, NEG_INF, el)
    v2 = jnp.max(el2, axis=-1, keepdims=True)
    l2 = jnp.min(jnp.where(in_grp & (lanef != l1) & (el2 == v2), lanef, big), axis=-1, keepdims=True)
    e21 = jnp.exp(v2 - v1)
    w1 = p_grp / (1.0 + e21)
    w2 = p_grp * e21 / (1.0 + e21)

    oh1 = jnp.where(lanef == l1, 1.0, 0.0)
    oh2 = jnp.where(lanef == l2, 1.0, 0.0)

    @pl.when(step == 0)
    def _():
        base_ref[...] = jnp.zeros_like(base_ref)

    both = oh1 + oh2
    strict = (_iota((tm, tm), 0) > _iota((tm, tm), 1)).astype(BF16)
    before = _dot(strict, both.astype(BF16)) + base_ref[0:1, :]
    r1 = jnp.sum(oh1 * before, axis=-1, keepdims=True)
    r2 = jnp.sum(oh2 * before, axis=-1, keepdims=True)
    total = base_ref[0:1, :] + jnp.sum(both, axis=0, keepdims=True)
    base_ref[0:1, :] = total
    cnt_ref[...] = jnp.broadcast_to(total, cnt_ref.shape)

    rt = jnp.zeros((tm, LANES), F32)
    for ln, val in ((RT_E1, l1 - ROUTER_E0), (RT_E2, l2 - ROUTER_E0), (RT_W1, w1), (RT_W2, w2), (RT_R1, r1), (RT_R2, r2)):
        rt = jnp.where(lane == ln, val, rt)
    rt_ref[...] = rt


def _merge_call(h, oa, ob, oc, od, gain, wg, wb, wo, fgain, wr, br):
    T, D = h.shape
    tm = TM_MERGE
    tile = lambda i: (i, 0)
    c2 = lambda i: (0, 0)
    c3 = lambda i: (0, 0, 0)
    once = pl.Buffered(1)
    in_specs = [pl.BlockSpec((tm, D), tile)] + [pl.BlockSpec((tm, MIX_WIDTH), tile)] * 4 + [
        pl.BlockSpec((1, D), c2),
        pl.BlockSpec(wg.shape, c3, pipeline_mode=once),
        pl.BlockSpec(wb.shape, c3, pipeline_mode=once),
        pl.BlockSpec(wo.shape, c2, pipeline_mode=once),
        pl.BlockSpec((1, D), c2),
        pl.BlockSpec(wr.shape, c2),
        pl.BlockSpec((1, LANES), c2),
    ]
    out_shape = [jax.ShapeDtypeStruct((T, D), F32), jax.ShapeDtypeStruct((T, D), F32),
                 jax.ShapeDtypeStruct((T, LANES), F32), jax.ShapeDtypeStruct((8, LANES), F32)]
    out_specs = [pl.BlockSpec((tm, D), tile), pl.BlockSpec((tm, D), tile), pl.BlockSpec((tm, LANES), tile),
                 pl.BlockSpec((8, LANES), c2)]
    return pl.pallas_call(
        _merge_kernel, grid=(T // tm,), in_specs=in_specs, out_specs=out_specs, out_shape=out_shape,
        scratch_shapes=[pltpu.VMEM((8, LANES), F32)],
        compiler_params=_cparams(("arbitrary",)),
    )(h, oa, ob, oc, od, gain, wg, wb, wo, fgain, wr, br)


def _row_copy(src, src_row, dst, dst_row, sem):
    return pltpu.make_async_copy(src.at[pl.ds(src_row, 1)], dst.at[pl.ds(dst_row, 1)], sem)


def _dest_kernel(rt_ref, ps_ref, d_ref):
    rt = rt_ref[...]
    lane = _iota(rt.shape, 1)
    lanef = lane.astype(F32)
    ps = ps_ref[...]

    def dest(e_lane, r_lane):
        start = jnp.sum(jnp.where(lanef == rt[:, e_lane:e_lane + 1], ps, 0.0), axis=-1, keepdims=True)
        return start + rt[:, r_lane:r_lane + 1]

    d = jnp.where(lane == 0, dest(RT_E1, RT_R1), jnp.where(lane == 1, dest(RT_E2, RT_R2), 0.0))
    d_ref[...] = d.astype(I32)


def _dest_call(rt, pstart_row):
    T = rt.shape[0]
    tm = TM_MERGE
    return pl.pallas_call(
        _dest_kernel, grid=(T // tm,),
        in_specs=[pl.BlockSpec((tm, LANES), lambda i: (i, 0)), pl.BlockSpec((1, LANES), lambda i: (0, 0))],
        out_specs=pl.BlockSpec((tm, LANES), lambda i: (i, 0)), out_shape=jax.ShapeDtypeStruct((T, LANES), I32),
        compiler_params=_cparams(("arbitrary",)),
    )(rt, pstart_row)


DMA_ISSUE_UNROLL = 8


def _dispatch_kernel(d1_ref, d2_ref, zero_blk_ref, xn_ref, xs_ref, zero_ref, sem, zero_sem):
    base = pl.program_id(0) * TD_DISPATCH

    @pl.when(pl.program_id(0) == 0)
    def _():
        zero_ref[...] = jnp.zeros_like(zero_ref)

        def zero_copy(j):
            row0 = pl.multiple_of(j * MOE_ROW_BLOCK, MOE_ROW_BLOCK)
            return pltpu.make_async_copy(zero_ref, xs_ref.at[pl.ds(row0, MOE_ROW_BLOCK)], zero_sem)

        def start_zero(j, c):
            pl.when(zero_blk_ref[j] != 0)(lambda: zero_copy(j).start())
            return c

        def wait_zero(j, c):
            pl.when(zero_blk_ref[j] != 0)(lambda: zero_copy(j).wait())
            return c

        n_blocks = xs_ref.shape[0] // MOE_ROW_BLOCK
        lax.fori_loop(0, n_blocks, start_zero, 0)
        lax.fori_loop(0, n_blocks, wait_zero, 0)

    def copies(r):
        t = base + r
        return (_row_copy(xn_ref, r, xs_ref, d1_ref[t], sem), _row_copy(xn_ref, r, xs_ref, d2_ref[t], sem))

    def start(r, c):
        for priority, cp in enumerate(copies(r)):
            cp.start(priority=priority)
        return c

    def wait(r, c):
        for cp in copies(r):
            cp.wait()
        return c

    lax.fori_loop(0, TD_DISPATCH, start, 0, unroll=DMA_ISSUE_UNROLL)
    lax.fori_loop(0, TD_DISPATCH, wait, 0, unroll=DMA_ISSUE_UNROLL)


def _dispatch_call(d1, d2, zero_blk, xn, n_rows):
    T, D = xn.shape
    grid_spec = pltpu.PrefetchScalarGridSpec(
        num_scalar_prefetch=3, grid=(T // TD_DISPATCH,),
        in_specs=[pl.BlockSpec((TD_DISPATCH, D), lambda i, *_: (i, 0))],
        out_specs=pl.BlockSpec(memory_space=pl.ANY),
        scratch_shapes=[pltpu.VMEM((MOE_ROW_BLOCK, D), xn.dtype), pltpu.SemaphoreType.DMA(()),
                        pltpu.SemaphoreType.DMA(())])
    return pl.pallas_call(
        _dispatch_kernel, grid_spec=grid_spec, out_shape=jax.ShapeDtypeStruct((n_rows, D), xn.dtype),
        compiler_params=pltpu.CompilerParams(dimension_semantics=("arbitrary",), has_side_effects=True,
                                             vmem_limit_bytes=VMEM_LIMIT),
    )(d1, d2, zero_blk, xn)


def _expert_kernel(be_ref, used_ref, x_ref, wg_ref, wu_ref, wd_ref, y_ref):
    i = pl.program_id(0)

    @pl.when(i < used_ref[0])
    def _():
        x = x_ref[...].astype(BF16)
        g = _dot(x, wg_ref[0, 0].astype(BF16))
        u = _dot(x, wu_ref[0, 0].astype(BF16))
        act = g / (1.0 + jnp.exp(-g)) * u
        y_ref[...] = _dot(act.astype(BF16), wd_ref[0, 0].astype(BF16))

    @pl.when(i >= used_ref[0])
    def _():
        y_ref[...] = jnp.zeros_like(y_ref)


def _expert_call(blk_expert, used, xs, wg, wu, wd, layer):
    rows, D = xs.shape
    rb = MOE_ROW_BLOCK
    DE = wg.shape[3]
    xmap = lambda i, be, used: (jnp.maximum(jnp.minimum(i, used[0] - 1), 0), 0)
    wmap = lambda i, be, used: (layer, be[i], 0, 0)
    grid_spec = pltpu.PrefetchScalarGridSpec(
        num_scalar_prefetch=2, grid=(rows // rb,),
        in_specs=[pl.BlockSpec((rb, D), xmap), pl.BlockSpec((1, 1, D, DE), wmap), pl.BlockSpec((1, 1, D, DE), wmap),
                  pl.BlockSpec((1, 1, DE, D), wmap)],
        out_specs=pl.BlockSpec((rb, D), lambda i, be, used: (i, 0)))
    return pl.pallas_call(
        _expert_kernel, grid_spec=grid_spec, out_shape=jax.ShapeDtypeStruct((rows, D), F32),
        compiler_params=_cparams(("arbitrary",)),
    )(blk_expert, used, xs, wg, wu, wd)


def _combine_kernel(final_norm, d1_ref, d2_ref, h_ref, rt_ref, gain_ref, ys_ref, o_ref, buf_ref, sem):
    base = pl.program_id(0) * TC_COMBINE

    def copies(r):
        t = base + r
        return (_row_copy(ys_ref, d1_ref[t], buf_ref.at[0], r, sem),
                _row_copy(ys_ref, d2_ref[t], buf_ref.at[1], r, sem))

    def start(r, c):
        for priority, cp in enumerate(copies(r)):
            cp.start(priority=priority)
        return c

    def wait(r, c):
        for cp in copies(r):
            cp.wait()
        return c

    lax.fori_loop(0, TC_COMBINE, start, 0, unroll=DMA_ISSUE_UNROLL)
    lax.fori_loop(0, TC_COMBINE, wait, 0, unroll=DMA_ISSUE_UNROLL)
    rt = rt_ref[...]
    out = h_ref[...] + rt[:, RT_W1:RT_W1 + 1] * buf_ref[0] + rt[:, RT_W2:RT_W2 + 1] * buf_ref[1]
    if final_norm:
        out = out * lax.rsqrt(jnp.mean(out * out, axis=-1, keepdims=True) + NORM_EPS) * gain_ref[...]
    o_ref[...] = out


def _combine_call(d1, d2, h, rt, gain, ys, final_norm):
    T, D = h.shape
    tc = TC_COMBINE
    tile = lambda i, *_: (i, 0)
    grid_spec = pltpu.PrefetchScalarGridSpec(
        num_scalar_prefetch=2, grid=(T // tc,),
        in_specs=[pl.BlockSpec((tc, D), tile), pl.BlockSpec((tc, LANES), tile),
                  pl.BlockSpec((1, D), lambda i, *_: (0, 0)), pl.BlockSpec(memory_space=pl.ANY)],
        out_specs=pl.BlockSpec((tc, D), tile),
        scratch_shapes=[pltpu.VMEM((2, tc, D), F32), pltpu.SemaphoreType.DMA(())])
    return pl.pallas_call(
        functools.partial(_combine_kernel, final_norm), grid_spec=grid_spec,
        out_shape=jax.ShapeDtypeStruct((T, D), F32),
        compiler_params=_cparams(("arbitrary",)),
    )(d1, d2, h, rt, gain, ys)


def _in_proj_columns():
    widths = (('q_a', 256), ('k_cmp', 64), ('v_cmp', 64), ('k_sel', 64), ('v_sel', 64), ('k_win', 64), ('v_win', 64),
              ('g_a', 12), ('q_b', 256), ('k_b', 256), ('v_b', 256), ('f_b', 4), ('q_c', 256), ('ckv_c', 128),
              ('qi_c', 256), ('ki_c', 64), ('wi_c', 4), ('q_d', 256), ('k_d', 128), ('v_d', 128))
    off, cols = 0, {}
    for name, w in widths:
        cols[name] = np.arange(off, off + w)
        off += w
    cat = lambda names: np.concatenate([cols[n] for n in names])
    rope = cat(('q_a', 'q_c', 'qi_c', 'q_d', 'k_d', 'k_cmp', 'k_sel', 'k_win', 'ki_c'))
    plain = cat(('q_b', 'k_b', 'v_b', 'v_d', 'ckv_c', 'v_cmp', 'v_sel', 'v_win'))
    small = cat(('g_a', 'f_b', 'wi_c'))
    assert rope.size == N_ROPE_COLS and plain.size == N_PLAIN_COLS
    return np.concatenate([rope, plain, small]), N_SMALL_COLS - small.size


def _static_tables(S):
    n_cmp_pad = S // NSA_CMP_STRIDE
    n_blk = S // NSA_SEL_BLOCK
    c0 = np.arange(n_cmp_pad) * NSA_CMP_STRIDE
    b0 = np.arange(n_blk) * NSA_SEL_BLOCK
    ov = ((c0[:, None] < b0[None, :] + NSA_SEL_BLOCK) & (c0[:, None] + NSA_CMP_LEN > b0[None, :])).astype(np.float32)
    overlap = np.ascontiguousarray(ov.T)
    expand = np.zeros((LANES, S), np.float32)
    expand[np.arange(S) // NSA_SEL_BLOCK, np.arange(S)] = 1.0
    strict_lower = (np.arange(LANES)[:, None] > np.arange(LANES)[None, :]).astype(np.float32)
    cntmat = np.concatenate([strict_lower, np.ones((16, LANES), np.float32)], axis=0)
    return jnp.asarray(overlap, BF16), jnp.asarray(expand, BF16), jnp.asarray(cntmat, BF16)


def kernel(x, positions, attn_norm, w_in, nsa_pe_k, nsa_w1_k, nsa_w2_k, nsa_pe_v, nsa_w1_v, nsa_w2_v, fox_forget_bias, dsa_kv_norm, dsa_w_ukv, swa_sinks, w_branch, w_gate, w_out, ffn_norm, moe_w_group, moe_b_group, moe_w_expert, moe_b_expert, moe_w_gate, moe_w_up, moe_w_down, final_norm):
    B, S, D = x.shape
    T = B * S
    depth = w_in.shape[0]
    perm, n_pad = _in_proj_columns()
    overlap, expand, cntmat = _static_tables(S)
    half = HEAD_DIM // 2
    inv_freq = ROPE_THETA ** (-jnp.arange(half, dtype=F32) / half)
    invf = jnp.tile(inv_freq, LANES // half).reshape(1, LANES)
    pos = positions.reshape(T, 1).astype(I32)
    small_pad = lambda v, off: jnp.zeros((1, LANES), F32).at[0, off:off + v.shape[0]].set(v.astype(F32))

    n_rows = -(-(T * 2 + MOE_N_EXPERTS * (MOE_ROW_BLOCK - 1)) // MOE_ROW_BLOCK) * MOE_ROW_BLOCK
    h = x.reshape(T, D)
    for l in range(depth):
        w = jnp.pad(w_in[l][:, perm], ((0, 0), (0, n_pad))).astype(BF16)
        (qa, qc, qi, qd, kd, kcmp, ksel, kwin, ki, qb, kb, vb, vd, vcmp, vsel, vwin, kc, vc, sm) = _proj_call(
            h, pos, invf, attn_norm[l].reshape(1, D), w, small_pad(fox_forget_bias[l], SM_F0),
            dsa_kv_norm[l].reshape(1, DSA_KV_RANK), dsa_w_ukv[l].astype(BF16), B, S)

        cw = NSA_CMP_STRIDE * HEAD_DIM
        pe_rows = lambda pe: jnp.broadcast_to(pe.reshape(1, 2 * cw), (8, 2 * cw)).astype(BF16)
        kcc, vcc = _cmp_call(kcmp.reshape(T // NSA_CMP_STRIDE, cw), vcmp.reshape(T // NSA_CMP_STRIDE, cw),
                             pe_rows(nsa_pe_k[l]), nsa_w1_k[l].astype(BF16), nsa_w2_k[l].astype(BF16),
                             pe_rows(nsa_pe_v[l]), nsa_w1_v[l].astype(BF16), nsa_w2_v[l].astype(BF16), B, S)
        o_a = _nsa_call(qa, sm, kcc, vcc, overlap, expand, ksel, vsel, kwin, vwin, B, S)

        key_bias = sm[:, SM_F0:SM_F0 + N_HEADS].reshape(B, S, N_HEADS).transpose(0, 2, 1)
        o_b, o_d = _fox_swa_call(qb, key_bias, kb, vb, qd, small_pad(swa_sinks[l], 0), kd, vd, B, S)
        idx_w = sm[:, SM_W0:SM_W0 + N_HEADS].reshape(B, S, N_HEADS).transpose(0, 2, 1)
        o_c = _dsa_call(qc, qi, idx_w, ki, cntmat, kc, vc, B, S)

        wr = jnp.zeros((D, LANES), F32).at[:, :MOE_GROUPS].set(moe_w_group[l]) \
            .at[:, ROUTER_E0:ROUTER_E0 + MOE_N_EXPERTS].set(moe_w_expert[l])
        br = jnp.zeros((1, LANES), F32).at[0, :MOE_GROUPS].set(moe_b_group[l]) \
            .at[0, ROUTER_E0:ROUTER_E0 + MOE_N_EXPERTS].set(moe_b_expert[l])
        h2, xn, rt, cnt = _merge_call(h, o_a, o_b, o_c, o_d, attn_norm[l].reshape(1, D), w_gate[l].astype(BF16),
                                      w_branch[l].astype(BF16), w_out[l].astype(BF16), ffn_norm[l].reshape(1, D),
                                      wr, br)

        counts = cnt[0, ROUTER_E0:ROUTER_E0 + MOE_N_EXPERTS].astype(I32)
        padded = (counts + MOE_ROW_BLOCK - 1) // MOE_ROW_BLOCK * MOE_ROW_BLOCK
        p_end = jnp.cumsum(padded)
        p_start = p_end - padded
        n_blocks = n_rows // MOE_ROW_BLOCK
        blk_first_row = jnp.arange(n_blocks, dtype=I32) * MOE_ROW_BLOCK
        blk_expert = jnp.minimum(jnp.sum((p_end[None, :] <= blk_first_row[:, None]).astype(I32), axis=1),
                                 MOE_N_EXPERTS - 1).astype(I32)
        used = (p_end[-1:] // MOE_ROW_BLOCK).astype(I32)
        dest = _dest_call(rt, small_pad(p_start, 0))
        d1, d2 = dest[:, 0], dest[:, 1]

        blk = jnp.arange(n_blocks, dtype=I32)
        last_of_segment = jnp.any((blk[:, None] == (p_end // MOE_ROW_BLOCK - 1)[None, :]) & (padded > 0)[None, :], axis=1)
        zero_blk = (last_of_segment | (blk >= used[0])).astype(I32)
        xs = _dispatch_call(d1, d2, zero_blk, xn, n_rows)
        ys = _expert_call(blk_expert, used, xs, moe_w_gate, moe_w_up, moe_w_down, l)
        last = l == depth - 1
        h = _combine_call(d1, d2, h2, rt, final_norm.reshape(1, D), ys, last)
    return h.reshape(B, S, D)
```
